```python
import math
import jax, jax.numpy as jnp
from jax import lax
import numpy as np

D_MODEL = 1024
BATCH = 8
SEQ = 4096
DEPTH = 2

N_HEADS_MLA = 8
MLA_QK_NOPE = 64
MLA_QK_ROPE = 32
MLA_V = 64
MLA_Q_RANK = 768
MLA_KV_RANK = 256
MLA_QK = MLA_QK_NOPE + MLA_QK_ROPE

N_HEADS_SB = 8
SB_HEAD = 64

N_HEADS_MOBA = 8
MOBA_HEAD = 64
MOBA_BLOCK = 256
MOBA_TOPK = 3
MOBA_Q_CHUNK = 32

Q_BLOCK = 128
ROPE_THETA = 500000.0
PARTIAL_ROPE_DIM = MOBA_HEAD // 4
D_FF = 4 * D_MODEL
N_BRANCH = 3
EPS = 1e-6
NEG_INF = -1e30

W_MLA = N_HEADS_MLA * MLA_V
W_SB = N_HEADS_SB * SB_HEAD
W_MOBA = N_HEADS_MOBA * MOBA_HEAD

IN_SPLITS = (MLA_Q_RANK, MLA_KV_RANK, MLA_QK_ROPE, 3 * W_SB, 3 * W_MOBA, N_BRANCH * D_MODEL)
D_IN = sum(IN_SPLITS)
SPLIT_POINTS = [int(v) for v in np.cumsum(IN_SPLITS)[:-1]]

kernel_name = "hybrid_mla_stickbreak_moba_adaln"


def rmsnorm(x, g):
    xf = x.astype(jnp.float32)
    y = xf * lax.rsqrt(jnp.mean(xf * xf, axis=-1, keepdims=True) + EPS)
    return y.astype(x.dtype) * g


def modulate(h, shift, scale):
    return h * (1.0 + scale[:, None, :]) + shift[:, None, :]


def rope_tables(positions, rot_dim, dtype):
    inv = ROPE_THETA ** (-jnp.arange(0, rot_dim, 2, dtype=jnp.float32) / rot_dim)
    ang = positions.astype(jnp.float32)[..., None] * inv
    return jnp.cos(ang)[:, None].astype(dtype), jnp.sin(ang)[:, None].astype(dtype)


def apply_rope(x, cos, sin):
    rd = 2 * cos.shape[-1]
    xr, xp = x[..., :rd], x[..., rd:]
    x1, x2 = xr[..., : rd // 2], xr[..., rd // 2:]
    rot = jnp.concatenate([x1 * cos - x2 * sin, x1 * sin + x2 * cos], axis=-1)
    return jnp.concatenate([rot, xp], axis=-1)


def to_heads(t, n_heads):
    b, s, w = t.shape
    return t.reshape(b, s, n_heads, w // n_heads).transpose(0, 2, 1, 3)


def from_heads(t):
    b, h, s, d = t.shape
    return t.transpose(0, 2, 1, 3).reshape(b, s, h * d)


def causal_softmax_attention(q, k, v, scale):
    b, h, s, dq = q.shape
    nq = s // Q_BLOCK
    qb = jnp.moveaxis(q.reshape(b, h, nq, Q_BLOCK, dq), 2, 0)
    kpos = jnp.arange(s)

    def body(args):
        qi, i = args
        sc = jnp.einsum('bhqd,bhkd->bhqk', qi, k).astype(jnp.float32) * scale
        qpos = i * Q_BLOCK + jnp.arange(Q_BLOCK)
        sc = jnp.where(kpos[None, :] <= qpos[:, None], sc, NEG_INF)
        p = jax.nn.softmax(sc, axis=-1).astype(v.dtype)
        return jnp.einsum('bhqk,bhkd->bhqd', p, v)

    o = lax.map(body, (qb, jnp.arange(nq)))
    return jnp.moveaxis(o, 0, 2).reshape(b, h, s, v.shape[-1])


def stick_breaking_attention(q, k, v):
    b, h, s, d = q.shape
    scale = 1.0 / math.sqrt(d)
    nq = s // Q_BLOCK
    qb = jnp.moveaxis(q.reshape(b, h, nq, Q_BLOCK, d), 2, 0)
    kpos = jnp.arange(s)

    def body(args):
        qi, i = args
        z = jnp.einsum('bhqd,bhkd->bhqk', qi, k).astype(jnp.float32) * scale
        qpos = i * Q_BLOCK + jnp.arange(Q_BLOCK)
        past = kpos[None, :] < qpos[:, None]
        log_beta = jax.nn.log_sigmoid(z)
        log_keep = jnp.where(past, jax.nn.log_sigmoid(-z), 0.0)
        later = lax.cumsum(log_keep, axis=3, reverse=True) - log_keep
        a = jnp.where(past, jnp.exp(log_beta + later), 0.0).astype(v.dtype)
        return jnp.einsum('bhqk,bhkd->bhqd', a, v)

    o = lax.map(body, (qb, jnp.arange(nq)))
    return jnp.moveaxis(o, 0, 2).reshape(b, h, s, d)


def moba_attention(q, k, v):
    b, h, s, d = q.shape
    scale = 1.0 / math.sqrt(d)
    nb = -(-s // MOBA_BLOCK)
    pad = nb * MOBA_BLOCK - s
    kb = jnp.pad(k, ((0, 0), (0, 0), (0, pad), (0, 0))).reshape(b, h, nb, MOBA_BLOCK, d)
    vb = jnp.pad(v, ((0, 0), (0, 0), (0, pad), (0, 0))).reshape(b, h, nb, MOBA_BLOCK, d)
    kmean = jnp.mean(kb.astype(jnp.float32), axis=3)

    cur = jnp.arange(s) // MOBA_BLOCK
    gate = jnp.einsum('bhsd,bhnd->bhsn', q.astype(jnp.float32), kmean)
    past_blk = jnp.arange(nb)[None, :] < cur[:, None]
    gate = jnp.where(past_blk, gate, NEG_INF)
    topk = min(MOBA_TOPK, nb)
    _, idx = lax.top_k(gate, topk)
    valid = jnp.arange(topk)[None, :] < cur[:, None]

    nc = s // MOBA_Q_CHUNK
    qc = jnp.moveaxis(q.reshape(b, h, nc, MOBA_Q_CHUNK, d), 2, 0)
    ic = jnp.moveaxis(idx.reshape(b, h, nc, MOBA_Q_CHUNK, topk), 2, 0)
    vc = valid.reshape(nc, MOBA_Q_CHUNK, topk)
    gather = jax.vmap(jax.vmap(lambda blocks, ids: blocks[ids]))

    def body(args):
        qi, ii, vi, ci = args
        kg = gather(kb, ii)
        vg = gather(vb, ii)
        qpos = ci * MOBA_Q_CHUNK + jnp.arange(MOBA_Q_CHUNK)
        s_sel = jnp.einsum('bhqd,bhqnjd->bhqnj', qi, kg).astype(jnp.float32) * scale
        s_sel = jnp.where(vi[:, :, None], s_sel, NEG_INF).reshape(b, h, MOBA_Q_CHUNK, topk * MOBA_BLOCK)
        ob = (ci * MOBA_Q_CHUNK) // MOBA_BLOCK
        k_own = lax.dynamic_index_in_dim(kb, ob, axis=2, keepdims=False)
        v_own = lax.dynamic_index_in_dim(vb, ob, axis=2, keepdims=False)
        s_own = jnp.einsum('bhqd,bhjd->bhqj', qi, k_own).astype(jnp.float32) * scale
        kpos = ob * MOBA_BLOCK + jnp.arange(MOBA_BLOCK)
        s_own = jnp.where(kpos[None, :] <= qpos[:, None], s_own, NEG_INF)
        p = jax.nn.softmax(jnp.concatenate([s_sel, s_own], axis=-1), axis=-1).astype(v.dtype)
        p_sel = p[..., : topk * MOBA_BLOCK].reshape(b, h, MOBA_Q_CHUNK, topk, MOBA_BLOCK)
        p_own = p[..., topk * MOBA_BLOCK:]
        return (jnp.einsum('bhqnj,bhqnjd->bhqd', p_sel, vg)
                + jnp.einsum('bhqj,bhjd->bhqd', p_own, v_own))

    o = lax.map(body, (qc, ic, vc, jnp.arange(nc)))
    return jnp.moveaxis(o, 0, 2).reshape(b, h, s, d)


def setup_inputs(seed: int = 0) -> dict:
    key = jax.random.key(seed)
    ks = jax.random.split(key, 24)
    f32 = jnp.float32

    def nrm(k, shape, fan_in):
        return jax.random.normal(k, shape, f32) * (fan_in ** -0.5)

    def gain(k, shape):
        return 1.0 + 0.02 * jax.random.normal(k, shape, f32)

    L = DEPTH
    return {
        "x": jax.random.normal(ks[0], (BATCH, SEQ, D_MODEL), f32),
        "c": jax.random.normal(ks[1], (BATCH, D_MODEL), f32),
        "positions": jnp.broadcast_to(jnp.arange(SEQ, dtype=jnp.int32)[None, :], (BATCH, SEQ)),
        "w_ada": nrm(ks[2], (L, D_MODEL, 6 * D_MODEL), D_MODEL),
        "b_ada": 0.02 * jax.random.normal(ks[3], (L, 6 * D_MODEL), f32),
        "norm1_g": gain(ks[4], (L, D_MODEL)),
        "norm2_g": gain(ks[5], (L, D_MODEL)),
        "w_in": nrm(ks[6], (L, D_MODEL, D_IN), D_MODEL),
        "q_norm_g": gain(ks[7], (L, MLA_Q_RANK)),
        "w_uq": nrm(ks[8], (L, MLA_Q_RANK, N_HEADS_MLA * MLA_QK), MLA_Q_RANK),
        "kv_norm_g": gain(ks[9], (L, MLA_KV_RANK)),
        "w_ukv": nrm(ks[10], (L, MLA_KV_RANK, N_HEADS_MLA * (MLA_QK_NOPE + MLA_V)), MLA_KV_RANK),
        "w_o_mla": nrm(ks[11], (L, W_MLA, D_MODEL), W_MLA),
        "w_o_sb": nrm(ks[12], (L, W_SB, D_MODEL), W_SB),
        "w_o_moba": nrm(ks[13], (L, W_MOBA, D_MODEL), W_MOBA),
        "w_out": nrm(ks[14], (L, D_MODEL, D_MODEL), D_MODEL),
        "w_ff1": nrm(ks[15], (L, D_MODEL, D_FF), D_MODEL),
        "w_ff2": nrm(ks[16], (L, D_FF, D_MODEL), D_FF),
        "final_norm_g": gain(ks[17], (D_MODEL,)),
    }


def reference(x, c, positions, w_ada, b_ada, norm1_g, norm2_g, w_in, q_norm_g, w_uq,
              kv_norm_g, w_ukv, w_o_mla, w_o_sb, w_o_moba, w_out, w_ff1, w_ff2, final_norm_g):
    b, s, _ = x.shape
    cos_mla, sin_mla = rope_tables(positions, MLA_QK_ROPE, x.dtype)
    cos_mb, sin_mb = rope_tables(positions, PARTIAL_ROPE_DIM, x.dtype)
    c_act = jax.nn.silu(c)
    mla_scale = 1.0 / math.sqrt(MLA_QK)

    for l in range(DEPTH):
        mod = c_act @ w_ada[l] + b_ada[l]
        shift1, scale1, gate1, shift2, scale2, gate2 = jnp.split(mod, 6, axis=-1)

        hdn = modulate(rmsnorm(x, norm1_g[l]), shift1, scale1)
        proj = hdn @ w_in[l]
        q_lat, c_kv, k_pe, sb_qkv, mb_qkv, gates = jnp.split(proj, SPLIT_POINTS, axis=-1)

        q = to_heads(rmsnorm(q_lat, q_norm_g[l]) @ w_uq[l], N_HEADS_MLA)
        q_nope, q_pe = q[..., :MLA_QK_NOPE], q[..., MLA_QK_NOPE:]
        kv = to_heads(rmsnorm(c_kv, kv_norm_g[l]) @ w_ukv[l], N_HEADS_MLA)
        k_nope, v_mla = kv[..., :MLA_QK_NOPE], kv[..., MLA_QK_NOPE:]
        q_pe = apply_rope(q_pe, cos_mla, sin_mla)
        k_pe = apply_rope(k_pe[:, None], cos_mla, sin_mla)
        q_mla = jnp.concatenate([q_nope, q_pe], axis=-1)
        k_mla = jnp.concatenate([k_nope, jnp.broadcast_to(k_pe, k_nope.shape[:3] + (MLA_QK_ROPE,))], axis=-1)
        o_mla = from_heads(causal_softmax_attention(q_mla, k_mla, v_mla, mla_scale))

        q_sb, k_sb, v_sb = [to_heads(t, N_HEADS_SB) for t in jnp.split(sb_qkv, 3, axis=-1)]
        o_sb = from_heads(stick_breaking_attention(q_sb, k_sb, v_sb))

        q_mb, k_mb, v_mb = [to_heads(t, N_HEADS_MOBA) for t in jnp.split(mb_qkv, 3, axis=-1)]
        q_mb = apply_rope(q_mb, cos_mb, sin_mb)
        k_mb = apply_rope(k_mb, cos_mb, sin_mb)
        o_mb = from_heads(moba_attention(q_mb, k_mb, v_mb))

        g_a, g_b, g_c = jnp.split(jax.nn.sigmoid(gates), N_BRANCH, axis=-1)
        merged = (g_a * (o_mla @ w_o_mla[l]) + g_b * (o_sb @ w_o_sb[l])
                  + g_c * (o_mb @ w_o_moba[l]))
        x = x + gate1[:, None, :] * (merged @ w_out[l])

        hdn = modulate(rmsnorm(x, norm2_g[l]), shift2, scale2)
        ff = jnp.square(jax.nn.relu(hdn @ w_ff1[l])) @ w_ff2[l]
        x = x + gate2[:, None, :] * ff

    return rmsnorm(x, final_norm_g)
```

```python
import functools
import math

import jax
import jax.numpy as jnp
from jax import lax
from jax.experimental import pallas as pl
from jax.experimental.pallas import tpu as pltpu

F32 = jnp.float32
BF16 = jnp.bfloat16

D_MODEL = 1024
N_HEADS = 8
MLA_NOPE = 64
MLA_ROPE = 32
MLA_V = 64
MLA_QK = MLA_NOPE + MLA_ROPE
MLA_Q_RANK = 768
MLA_KV_RANK = 256
HEAD = 64
W_BRANCH = N_HEADS * HEAD
MOBA_BLOCK = 256
MOBA_TOPK = 3
MOBA_ROT = HEAD // 4
ROPE_THETA = 500000.0
D_FF = 4 * D_MODEL
EPS = 1e-6
NEG_INF = -1e30

LANE = 128
TQ = 256
VMEM_LIMIT = 56 * 1024 * 1024

C_QLAT = 0
C_CKV = C_QLAT + MLA_Q_RANK
C_KPE = C_CKV + MLA_KV_RANK
C_SB = C_KPE + LANE
C_MB = C_SB + 3 * W_BRANCH
C_GATE = C_MB + 3 * W_BRANCH
C_END = C_GATE + 3 * D_MODEL
W_MLA_OUT = 2 * N_HEADS * LANE + N_HEADS * MLA_V


def _rms(xf, g):
    return xf * lax.rsqrt(jnp.mean(xf * xf, axis=-1, keepdims=True) + EPS) * g


def _dot(a, b):
    return jnp.dot(a, b, preferred_element_type=F32)


def _dot_nt(a, b):
    return lax.dot_general(a, b, (((1,), (1,)), ((), ())), preferred_element_type=F32)


def _rope_lanes(xb, cos, sin, half, first_half):
    rot = jnp.where(first_half, pltpu.roll(xb, LANE - half, 1), pltpu.roll(xb, half, 1))
    return xb * cos + rot * sin


def _mod_kernel(c_ref, w_ref, b_ref, o_ref):
    c = c_ref[...]
    c_act = c * (1.0 / (1.0 + jnp.exp(-c)))
    o_ref[0] = _dot(c_act.astype(BF16), w_ref[0].astype(BF16)) + b_ref[0]


def _adaln_mod(c, w_ada, b_ada):
    depth, d, n = w_ada.shape
    b = c.shape[0]
    tn = 1536
    return pl.pallas_call(
        _mod_kernel,
        grid=(depth, n // tn),
        in_specs=[
            pl.BlockSpec((b, d), lambda l, j: (0, 0)),
            pl.BlockSpec((1, d, tn), lambda l, j: (l, 0, j)),
            pl.BlockSpec((1, 1, tn), lambda l, j: (l, 0, j)),
        ],
        out_specs=pl.BlockSpec((1, b, tn), lambda l, j: (l, 0, j)),
        out_shape=jax.ShapeDtypeStruct((depth, b, n), F32),
        compiler_params=pltpu.CompilerParams(
            dimension_semantics=("arbitrary", "arbitrary"), vmem_limit_bytes=VMEM_LIMIT),
        name="adaln_mod",
    )(c, w_ada, b_ada.reshape(depth, 1, n))


def _inproj_kernel(x_ref, shift_ref, scale_ref, g1_ref, w_ref, gq_ref, wuq_ref, gkv_ref, wukv_ref,
                   cosa_ref, sina_ref, cosm_ref, sinm_ref,
                   mla_ref, sb_ref, mb_ref, gate_ref):
    x = x_ref[0]
    h = _rms(x, g1_ref[...]) * (1.0 + scale_ref[0]) + shift_ref[0]
    h = h.astype(BF16)
    lane = lax.broadcasted_iota(jnp.int32, (1, LANE), 1)

    cosa, sina = cosa_ref[0], sina_ref[0]
    mla_first = lane < MLA_NOPE + MLA_ROPE // 2
    mla_scale = 1.0 / math.sqrt(MLA_QK)
    qlat = _dot(h, w_ref[:, C_QLAT:C_CKV])
    q = _dot(_rms(qlat, gq_ref[...]).astype(BF16), wuq_ref[...])
    for hd in range(N_HEADS):
        qb = _rope_lanes(q[:, hd * LANE:(hd + 1) * LANE], cosa, sina, MLA_ROPE // 2, mla_first)
        mla_ref[0, :, hd * LANE:(hd + 1) * LANE] = (qb * mla_scale).astype(BF16)

    kpe = _rope_lanes(_dot(h, w_ref[:, C_KPE:C_SB]), cosa, sina, MLA_ROPE // 2, mla_first)
    ckv = _dot(h, w_ref[:, C_CKV:C_KPE])
    kv = _dot(_rms(ckv, gkv_ref[...]).astype(BF16), wukv_ref[...])
    k_off = N_HEADS * LANE
    for hd in range(N_HEADS):
        kb = kv[:, hd * LANE:(hd + 1) * LANE] + kpe
        mla_ref[0, :, k_off + hd * LANE:k_off + (hd + 1) * LANE] = kb.astype(BF16)
    mla_ref[0, :, 2 * k_off:] = kv[:, k_off:].astype(BF16)

    sb_scale = 1.0 / math.sqrt(HEAD)
    sb_ref[0, :, :W_BRANCH] = (_dot(h, w_ref[:, C_SB:C_SB + W_BRANCH]) * sb_scale).astype(BF16)
    sb_ref[0, :, W_BRANCH:] = _dot(h, w_ref[:, C_SB + W_BRANCH:C_MB]).astype(BF16)

    cosm, sinm = cosm_ref[0], sinm_ref[0]
    mb_first = (lane % HEAD) < MOBA_ROT // 2
    for part, scale in ((0, sb_scale), (1, 1.0)):
        c0 = C_MB + part * W_BRANCH
        acc = _dot(h, w_ref[:, c0:c0 + W_BRANCH])
        for cb in range(W_BRANCH // LANE):
            blk = _rope_lanes(acc[:, cb * LANE:(cb + 1) * LANE], cosm, sinm, MOBA_ROT // 2, mb_first)
            mb_ref[0, :, part * W_BRANCH + cb * LANE:part * W_BRANCH + (cb + 1) * LANE] = (
                blk * scale).astype(BF16)
    mb_ref[0, :, 2 * W_BRANCH:] = _dot(h, w_ref[:, C_MB + 2 * W_BRANCH:C_GATE]).astype(BF16)

    for br in range(3):
        c0 = C_GATE + br * D_MODEL
        gl = _dot(h, w_ref[:, c0:c0 + D_MODEL])
        gate_ref[0, :, br * D_MODEL:(br + 1) * D_MODEL] = (1.0 / (1.0 + jnp.exp(-gl))).astype(BF16)


def _const_spec(shape):
    nd = len(shape)
    return pl.BlockSpec(shape, lambda *_: (0,) * nd, pipeline_mode=pl.Buffered(1))


def _inproj(x, shift, scale, g1, w_in_r, gq, wuq, gkv, wukv, cosa, sina, cosm, sinm):
    b, s, d = x.shape
    tm = TQ
    row = lambda w: pl.BlockSpec((1, tm, w), lambda i, j: (i, j, 0))
    vec = pl.BlockSpec((1, 1, d), lambda i, j: (i, 0, 0))
    return pl.pallas_call(
        _inproj_kernel,
        grid=(b, s // tm),
        in_specs=[row(d), vec, vec, _const_spec((1, d)), _const_spec(w_in_r.shape),
                  _const_spec(gq.shape), _const_spec(wuq.shape), _const_spec(gkv.shape),
                  _const_spec(wukv.shape), row(LANE), row(LANE), row(LANE), row(LANE)],
        out_specs=[row(W_MLA_OUT), row(3 * W_BRANCH), row(3 * W_BRANCH), row(3 * D_MODEL)],
        out_shape=[jax.ShapeDtypeStruct((b, s, W_MLA_OUT), BF16),
                   jax.ShapeDtypeStruct((b, s, 3 * W_BRANCH), BF16),
                   jax.ShapeDtypeStruct((b, s, 3 * W_BRANCH), BF16),
                   jax.ShapeDtypeStruct((b, s, 3 * D_MODEL), BF16)],
        compiler_params=pltpu.CompilerParams(
            dimension_semantics=("arbitrary", "arbitrary"), vmem_limit_bytes=VMEM_LIMIT),
        name="inproj",
    )(x, shift, scale, g1, w_in_r, gq, wuq, gkv, wukv, cosa, sina, cosm, sinm)


def _transpose_values(v_ref, vt_ref, nkv):
    for j in range(nkv):
        vt_ref[j] = v_ref[0, j * TQ:(j + 1) * TQ, :].astype(F32).T.astype(BF16)


def _store_heads(o_ref, o0, o1):
    o_ref[0] = jnp.concatenate([o0, o1], axis=0).T.astype(BF16)


def _split_heads(q):
    lane = lax.broadcasted_iota(jnp.int32, (1, LANE), 1)
    return [jnp.where((lane >= a * HEAD) & (lane < (a + 1) * HEAD), q, jnp.zeros_like(q))
            for a in range(2)]


def _causal(strict):
    key = lax.broadcasted_iota(jnp.int32, (TQ, TQ), 0)
    qry = lax.broadcasted_iota(jnp.int32, (TQ, TQ), 1)
    return key < qry if strict else key <= qry


def _softmax_step(state, s, vt):
    m, l, acc = state
    m_new = jnp.maximum(m, jnp.max(s, axis=0, keepdims=True))
    alpha = jnp.exp(m - m_new)
    p = jnp.exp(s - m_new)
    l = alpha * l + jnp.sum(p, axis=0, keepdims=True)
    acc = alpha * acc + _dot(vt, p.astype(BF16))
    return m_new, l, acc


def _softmax_init(dv):
    return (jnp.full((1, TQ), NEG_INF, F32), jnp.zeros((1, TQ), F32), jnp.zeros((dv, TQ), F32))


def _mla_kernel(q_ref, k_ref, v_ref, o_ref, vt_ref, *, nkv):
    qi = pl.program_id(2)

    @pl.when(qi == 0)
    def _():
        _transpose_values(v_ref, vt_ref, nkv)

    q = q_ref[0]
    qs = [q[:, a * LANE:(a + 1) * LANE] for a in range(2)]

    def block(j, states, mask):
        row0 = pl.multiple_of(j * TQ, TQ)
        out = []
        for a in range(2):
            s = _dot_nt(k_ref[0, pl.ds(row0, TQ), a * LANE:(a + 1) * LANE], qs[a])
            if mask is not None:
                s = jnp.where(mask, s, NEG_INF)
            out.append(_softmax_step(states[a], s, vt_ref[j, a * MLA_V:(a + 1) * MLA_V, :]))
        return tuple(out)

    states = block(qi, (_softmax_init(MLA_V),) * 2, _causal(False))
    states = lax.fori_loop(0, qi, lambda j, st: block(j, st, None), states)
    _store_heads(o_ref, *[acc / l for (_, l, acc) in states])


def _sb_kernel(q_ref, k_ref, v_ref, o_ref, vt_ref, *, nkv):
    qi = pl.program_id(2)

    @pl.when(qi == 0)
    def _():
        _transpose_values(v_ref, vt_ref, nkv)

    qs = _split_heads(q_ref[0])
    past = _causal(True)
    suffix = jnp.where(_causal(False), 1.0, 0.0).astype(BF16)

    def block(j, states, diag):
        row0 = pl.multiple_of(j * TQ, TQ)
        out = []
        for a in range(2):
            run, acc = states[a]
            z = _dot_nt(k_ref[0, pl.ds(row0, TQ), :], qs[a])
            log_beta = jnp.minimum(z, 0.0) - jnp.log(1.0 + jnp.exp(-jnp.abs(z)))
            log_keep = log_beta - z
            if diag:
                log_keep = jnp.where(past, log_keep, 0.0)
            hi = log_keep.astype(BF16)
            lo = (log_keep - hi.astype(F32)).astype(BF16)
            incl = _dot(suffix, hi) + _dot(suffix, lo)
            w = jnp.exp(log_beta + (incl - log_keep + run))
            if diag:
                w = jnp.where(past, w, 0.0)
            acc = acc + _dot(vt_ref[j, a * HEAD:(a + 1) * HEAD, :], w.astype(BF16))
            out.append((run + incl[0:1, :], acc))
        return tuple(out)

    init = (jnp.zeros((1, TQ), F32), jnp.zeros((HEAD, TQ), F32))
    states = block(qi, (init,) * 2, True)
    states = lax.fori_loop(0, qi, lambda i, st: block(qi - 1 - i, st, False), states)
    _store_heads(o_ref, states[0][1], states[1][1])


def _moba_kernel(q_ref, k_ref, v_ref, o_ref, vt_ref, km_ref, sel_ref, *, nkv, nbp):
    qi = pl.program_id(2)

    @pl.when(qi == 0)
    def _():
        _transpose_values(v_ref, vt_ref, nkv)
        km_ref[...] = jnp.zeros_like(km_ref)
        for j in range(nkv):
            km_ref[j:j + 1, :] = jnp.mean(k_ref[0, j * TQ:(j + 1) * TQ, :].astype(F32), axis=0,
                                          keepdims=True)

    q = q_ref[0]
    qs = _split_heads(q)
    qf = q.astype(F32)
    lane = lax.broadcasted_iota(jnp.int32, (1, LANE), 1)
    blk = lax.broadcasted_iota(jnp.int32, (nbp, TQ), 0)
    for a in range(2):
        km = jnp.where((lane >= a * HEAD) & (lane < (a + 1) * HEAD), km_ref[...], 0.0)
        gate = lax.dot_general(km, qf, (((1,), (1,)), ((), ())), precision=lax.Precision.HIGHEST,
                               preferred_element_type=F32)
        gate = jnp.where(blk < qi, gate, NEG_INF)
        rank = jnp.zeros((nbp, TQ), jnp.int32)
        for j in range(nkv):
            gj = gate[j:j + 1, :]
            ahead = (gj > gate) | ((gj == gate) & (blk > j))
            rank = rank + jnp.where(ahead, 1, 0)
        sel = (rank < MOBA_TOPK) & (rank < qi)
        sel_ref[a] = jnp.where(sel, 1.0, 0.0)

    def block(j, states, diag):
        row0 = pl.multiple_of(j * TQ, TQ)
        out = []
        for a in range(2):
            s = _dot_nt(k_ref[0, pl.ds(row0, TQ), :], qs[a])
            if diag:
                s = jnp.where(_causal(False), s, NEG_INF)
            else:
                s = jnp.where(sel_ref[a, pl.ds(j, 1), :] > 0.5, s, NEG_INF)
            out.append(_softmax_step(states[a], s, vt_ref[j, a * HEAD:(a + 1) * HEAD, :]))
        return tuple(out)

    states = block(qi, (_softmax_init(HEAD),) * 2, True)
    states = lax.fori_loop(0, qi, lambda j, st: block(j, st, False), states)
    _store_heads(o_ref, *[acc / l for (_, l, acc) in states])


def _attention(kind, src, q_col, k_col, v_col, qk_width):
    b, s, _ = src.shape
    nkv = s // TQ
    hp = N_HEADS // 2
    scratch = [pltpu.VMEM((nkv, LANE, TQ), BF16)]
    if kind == "mla":
        body = functools.partial(_mla_kernel, nkv=nkv)
    elif kind == "sb":
        body = functools.partial(_sb_kernel, nkv=nkv)
    else:
        nbp = -(-nkv // 8) * 8
        body = functools.partial(_moba_kernel, nkv=nkv, nbp=nbp)
        scratch += [pltpu.VMEM((nbp, LANE), F32), pltpu.VMEM((2, nbp, TQ), F32)]
    return pl.pallas_call(
        body,
        grid=(b, hp, nkv),
        in_specs=[
            pl.BlockSpec((1, TQ, qk_width), lambda i, p, t: (i, t, q_col + p)),
            pl.BlockSpec((1, s, qk_width), lambda i, p, t: (i, 0, k_col + p)),
            pl.BlockSpec((1, s, LANE), lambda i, p, t: (i, 0, v_col + p)),
        ],
        out_specs=pl.BlockSpec((1, TQ, LANE), lambda i, p, t: (i, t, p)),
        out_shape=jax.ShapeDtypeStruct((b, s, W_BRANCH), BF16),
        scratch_shapes=scratch,
        compiler_params=pltpu.CompilerParams(
            dimension_semantics=("arbitrary", "arbitrary", "arbitrary"),
            vmem_limit_bytes=VMEM_LIMIT),
        name="attn_" + kind,
    )(src, src, src)


def _merge_kernel(x_ref, gate1_ref, oa_ref, ob_ref, oc_ref, g_ref, wa_ref, wb_ref, wc_ref, wo_ref,
                  o_ref):
    merged = g_ref[0, :, :D_MODEL].astype(F32) * _dot(oa_ref[0], wa_ref[...])
    merged = merged + g_ref[0, :, D_MODEL:2 * D_MODEL].astype(F32) * _dot(ob_ref[0], wb_ref[...])
    merged = merged + g_ref[0, :, 2 * D_MODEL:].astype(F32) * _dot(oc_ref[0], wc_ref[...])
    o_ref[0] = x_ref[0] + gate1_ref[0] * _dot(merged.astype(BF16), wo_ref[...])


def _merge(x, gate1, oa, ob, oc, gates, wa, wb, wc, wo):
    b, s, d = x.shape
    tm = 512
    row = lambda w: pl.BlockSpec((1, tm, w), lambda i, j: (i, j, 0))
    vec = pl.BlockSpec((1, 1, d), lambda i, j: (i, 0, 0))
    return pl.pallas_call(
        _merge_kernel,
        grid=(b, s // tm),
        in_specs=[row(d), vec, row(W_BRANCH), row(W_BRANCH), row(W_BRANCH), row(3 * d),
                  _const_spec(wa.shape), _const_spec(wb.shape), _const_spec(wc.shape),
                  _const_spec(wo.shape)],
        out_specs=row(d),
        out_shape=jax.ShapeDtypeStruct((b, s, d), F32),
        compiler_params=pltpu.CompilerParams(
            dimension_semantics=("arbitrary", "arbitrary"), vmem_limit_bytes=VMEM_LIMIT),
        name="merge",
    )(x, gate1, oa, ob, oc, gates, wa, wb, wc, wo)


def _ffn_kernel(x_ref, shift_ref, scale_ref, gate_ref, g2_ref, w1_ref, w2_ref, gf_ref, o_ref, *,
                final_norm):
    x = x_ref[0]
    h = (_rms(x, g2_ref[...]) * (1.0 + scale_ref[0]) + shift_ref[0]).astype(BF16)
    acc = jnp.zeros_like(x)
    chunk = D_MODEL
    for c in range(D_FF // chunk):
        u = jnp.maximum(_dot(h, w1_ref[:, c * chunk:(c + 1) * chunk]), 0.0)
        acc = acc + _dot((u * u).astype(BF16), w2_ref[c * chunk:(c + 1) * chunk, :])
    y = x + gate_ref[0] * acc
    if final_norm:
        y = _rms(y, gf_ref[...])
    o_ref[0] = y


def _ffn(x, shift, scale, gate, g2, w1, w2, gf, final_norm):
    b, s, d = x.shape
    tm = 512
    row = pl.BlockSpec((1, tm, d), lambda i, j: (i, j, 0))
    vec = pl.BlockSpec((1, 1, d), lambda i, j: (i, 0, 0))
    return pl.pallas_call(
        functools.partial(_ffn_kernel, final_norm=final_norm),
        grid=(b, s // tm),
        in_specs=[row, vec, vec, vec, _const_spec((1, d)), _const_spec(w1.shape),
                  _const_spec(w2.shape), _const_spec((1, d))],
        out_specs=row,
        out_shape=jax.ShapeDtypeStruct((b, s, d), F32),
        compiler_params=pltpu.CompilerParams(
            dimension_semantics=("arbitrary", "arbitrary"), vmem_limit_bytes=VMEM_LIMIT),
        name="ffn",
    )(x, shift, scale, gate, g2, w1, w2, gf)


def _rope_tables(positions):
    pos = positions.astype(F32)[..., None]

    def angles(rot_dim):
        inv = ROPE_THETA ** (-jnp.arange(0, rot_dim, 2, dtype=F32) / rot_dim)
        ang = pos * inv
        return jnp.cos(ang), jnp.sin(ang)

    b, s = positions.shape
    ca, sa = angles(MLA_ROPE)
    one = lambda n: jnp.ones((b, s, n), F32)
    zero = lambda n: jnp.zeros((b, s, n), F32)
    cosa = jnp.concatenate([one(MLA_NOPE), ca, ca, one(LANE - MLA_QK)], axis=-1)
    sina = jnp.concatenate([zero(MLA_NOPE), -sa, sa, zero(LANE - MLA_QK)], axis=-1)
    cm, sm = angles(MOBA_ROT)
    cos_h = jnp.concatenate([cm, cm, one(HEAD - MOBA_ROT)], axis=-1)
    sin_h = jnp.concatenate([-sm, sm, zero(HEAD - MOBA_ROT)], axis=-1)
    cosm = jnp.concatenate([cos_h, cos_h], axis=-1)
    sinm = jnp.concatenate([sin_h, sin_h], axis=-1)
    return cosa, sina, cosm, sinm


def _layer_weights(w_in, w_uq, w_ukv):
    d = w_in.shape[0]
    o_kpe = MLA_Q_RANK + MLA_KV_RANK
    o_sb = o_kpe + MLA_ROPE
    kpe = jnp.concatenate([jnp.zeros((d, MLA_NOPE), F32), w_in[:, o_kpe:o_sb],
                           jnp.zeros((d, LANE - MLA_QK), F32)], axis=1)
    w_in_r = jnp.concatenate([w_in[:, :o_kpe], kpe, w_in[:, o_sb:]], axis=1).astype(BF16)
    wuq = jnp.pad(w_uq.reshape(MLA_Q_RANK, N_HEADS, MLA_QK),
                  ((0, 0), (0, 0), (0, LANE - MLA_QK))).reshape(MLA_Q_RANK, N_HEADS * LANE)
    ukv = w_ukv.reshape(MLA_KV_RANK, N_HEADS, MLA_NOPE + MLA_V)
    wk = jnp.pad(ukv[:, :, :MLA_NOPE], ((0, 0), (0, 0), (0, LANE - MLA_NOPE)))
    wukv = jnp.concatenate([wk.reshape(MLA_KV_RANK, N_HEADS * LANE),
                            ukv[:, :, MLA_NOPE:].reshape(MLA_KV_RANK, N_HEADS * MLA_V)], axis=1)
    return w_in_r, wuq.astype(BF16), wukv.astype(BF16)


def kernel(x, c, positions, w_ada, b_ada, norm1_g, norm2_g, w_in, q_norm_g, w_uq, kv_norm_g, w_ukv,
           w_o_mla, w_o_sb, w_o_moba, w_out, w_ff1, w_ff2, final_norm_g):
    b, s, d = x.shape
    depth = w_ada.shape[0]
    assert d == D_MODEL and s % 512 == 0 and w_in.shape[-1] == C_END - (LANE - MLA_ROPE)
    cosa, sina, cosm, sinm = _rope_tables(positions)
    mod = _adaln_mod(c, w_ada, b_ada)
    gf = final_norm_g.reshape(1, d)
    for l in range(depth):
        shift1, scale1, gate1, shift2, scale2, gate2 = [
            m.reshape(b, 1, d) for m in jnp.split(mod[l], 6, axis=-1)]
        w_in_r, wuq, wukv = _layer_weights(w_in[l], w_uq[l], w_ukv[l])
        mla, sb, mb, gates = _inproj(
            x, shift1, scale1, norm1_g[l].reshape(1, d), w_in_r, q_norm_g[l].reshape(1, -1), wuq,
            kv_norm_g[l].reshape(1, -1), wukv, cosa, sina, cosm, sinm)
        hp = N_HEADS // 2
        o_mla = _attention("mla", mla, 0, hp, 4 * hp, 2 * LANE)
        o_sb = _attention("sb", sb, 0, hp, 2 * hp, LANE)
        o_mb = _attention("moba", mb, 0, hp, 2 * hp, LANE)
        x = _merge(x, gate1, o_mla, o_sb, o_mb, gates, w_o_mla[l].astype(BF16),
                   w_o_sb[l].astype(BF16), w_o_moba[l].astype(BF16), w_out[l].astype(BF16))
        x = _ffn(x, shift2, scale2, gate2, norm2_g[l].reshape(1, d), w_ff1[l].astype(BF16),
                 w_ff2[l].astype(BF16), gf, l == depth - 1)
    return x
```

```python
import functools
import math

import jax
import jax.numpy as jnp
from jax import lax
from jax.experimental import pallas as pl
from jax.experimental.pallas import tpu as pltpu

F32 = jnp.float32
BF16 = jnp.bfloat16

D_MODEL = 1024
N_HEADS = 8
MLA_NOPE = 64
MLA_ROPE = 32
MLA_V = 64
MLA_QK = MLA_NOPE + MLA_ROPE
MLA_Q_RANK = 768
MLA_KV_RANK = 256
HEAD = 64
W_BRANCH = N_HEADS * HEAD
MOBA_BLOCK = 256
MOBA_TOPK = 3
MOBA_ROT = HEAD // 4
ROPE_THETA = 500000.0
D_FF = 4 * D_MODEL
EPS = 1e-6
NEG_INF = -1e30

LANE = 128
TQ = 256
VMEM_LIMIT = 56 * 1024 * 1024

C_QLAT = 0
C_CKV = C_QLAT + MLA_Q_RANK
C_KPE = C_CKV + MLA_KV_RANK
C_SB = C_KPE + LANE
C_MB = C_SB + 3 * W_BRANCH
C_GATE = C_MB + 3 * W_BRANCH
C_END = C_GATE + 3 * D_MODEL
W_MLA_OUT = 2 * N_HEADS * LANE + N_HEADS * MLA_V


def _rms(xf, g):
    return xf * lax.rsqrt(jnp.mean(xf * xf, axis=-1, keepdims=True) + EPS) * g


def _dot(a, b):
    return jnp.dot(a, b, preferred_element_type=F32)


def _dot_nt(a, b):
    return lax.dot_general(a, b, (((1,), (1,)), ((), ())), preferred_element_type=F32)


def _rope_lanes(xb, cos, sin, half, first_half):
    rot = jnp.where(first_half, pltpu.roll(xb, LANE - half, 1), pltpu.roll(xb, half, 1))
    return xb * cos + rot * sin


def _mod_kernel(c_ref, w_ref, b_ref, o_ref):
    c = c_ref[...]
    c_act = c * (1.0 / (1.0 + jnp.exp(-c)))
    o_ref[0] = _dot(c_act.astype(BF16), w_ref[0].astype(BF16)) + b_ref[0]


def _adaln_mod(c, w_ada, b_ada):
    depth, d, n = w_ada.shape
    b = c.shape[0]
    tn = 1536
    return pl.pallas_call(
        _mod_kernel,
        grid=(depth, n // tn),
        in_specs=[
            pl.BlockSpec((b, d), lambda l, j: (0, 0)),
            pl.BlockSpec((1, d, tn), lambda l, j: (l, 0, j)),
            pl.BlockSpec((1, 1, tn), lambda l, j: (l, 0, j)),
        ],
        out_specs=pl.BlockSpec((1, b, tn), lambda l, j: (l, 0, j)),
        out_shape=jax.ShapeDtypeStruct((depth, b, n), F32),
        compiler_params=pltpu.CompilerParams(
            dimension_semantics=("arbitrary", "arbitrary"), vmem_limit_bytes=VMEM_LIMIT),
        name="adaln_mod",
    )(c, w_ada, b_ada.reshape(depth, 1, n))


def _inproj_kernel(x_ref, shift_ref, scale_ref, g1_ref, w_ref, gq_ref, wuq_ref, gkv_ref, wukv_ref,
                   cosa_ref, sina_ref, cosm_ref, sinm_ref,
                   mla_ref, sb_ref, mb_ref, gate_ref):
    x = x_ref[0]
    h = _rms(x, g1_ref[...]) * (1.0 + scale_ref[0]) + shift_ref[0]
    h = h.astype(BF16)
    lane = lax.broadcasted_iota(jnp.int32, (1, LANE), 1)

    cosa, sina = cosa_ref[0], sina_ref[0]
    mla_first = lane < MLA_NOPE + MLA_ROPE // 2
    mla_scale = 1.0 / math.sqrt(MLA_QK)
    qlat = _dot(h, w_ref[:, C_QLAT:C_CKV])
    q = _dot(_rms(qlat, gq_ref[...]).astype(BF16), wuq_ref[...])
    for hd in range(N_HEADS):
        qb = _rope_lanes(q[:, hd * LANE:(hd + 1) * LANE], cosa, sina, MLA_ROPE // 2, mla_first)
        mla_ref[0, :, hd * LANE:(hd + 1) * LANE] = (qb * mla_scale).astype(BF16)

    kpe = _rope_lanes(_dot(h, w_ref[:, C_KPE:C_SB]), cosa, sina, MLA_ROPE // 2, mla_first)
    ckv = _dot(h, w_ref[:, C_CKV:C_KPE])
    kv = _dot(_rms(ckv, gkv_ref[...]).astype(BF16), wukv_ref[...])
    k_off = N_HEADS * LANE
    for hd in range(N_HEADS):
        kb = kv[:, hd * LANE:(hd + 1) * LANE] + kpe
        mla_ref[0, :, k_off + hd * LANE:k_off + (hd + 1) * LANE] = kb.astype(BF16)
    mla_ref[0, :, 2 * k_off:] = kv[:, k_off:].astype(BF16)

    sb_scale = 1.0 / math.sqrt(HEAD)
    sb_ref[0, :, :W_BRANCH] = (_dot(h, w_ref[:, C_SB:C_SB + W_BRANCH]) * sb_scale).astype(BF16)
    sb_ref[0, :, W_BRANCH:] = _dot(h, w_ref[:, C_SB + W_BRANCH:C_MB]).astype(BF16)

    cosm, sinm = cosm_ref[0], sinm_ref[0]
    mb_first = (lane % HEAD) < MOBA_ROT // 2
    for part, scale in ((0, sb_scale), (1, 1.0)):
        c0 = C_MB + part * W_BRANCH
        acc = _dot(h, w_ref[:, c0:c0 + W_BRANCH])
        for cb in range(W_BRANCH // LANE):
            blk = _rope_lanes(acc[:, cb * LANE:(cb + 1) * LANE], cosm, sinm, MOBA_ROT // 2, mb_first)
            mb_ref[0, :, part * W_BRANCH + cb * LANE:part * W_BRANCH + (cb + 1) * LANE] = (
                blk * scale).astype(BF16)
    mb_ref[0, :, 2 * W_BRANCH:] = _dot(h, w_ref[:, C_MB + 2 * W_BRANCH:C_GATE]).astype(BF16)

    for br in range(3):
        c0 = C_GATE + br * D_MODEL
        gl = _dot(h, w_ref[:, c0:c0 + D_MODEL])
        gate_ref[0, :, br * D_MODEL:(br + 1) * D_MODEL] = (1.0 / (1.0 + jnp.exp(-gl))).astype(BF16)


def _const_spec(shape):
    nd = len(shape)
    return pl.BlockSpec(shape, lambda *_: (0,) * nd, pipeline_mode=pl.Buffered(1))


def _inproj(x, shift, scale, g1, w_in_r, gq, wuq, gkv, wukv, cosa, sina, cosm, sinm):
    b, s, d = x.shape
    tm = TQ
    row = lambda w: pl.BlockSpec((1, tm, w), lambda i, j: (i, j, 0))
    vec = pl.BlockSpec((1, 1, d), lambda i, j: (i, 0, 0))
    return pl.pallas_call(
        _inproj_kernel,
        grid=(b, s // tm),
        in_specs=[row(d), vec, vec, _const_spec((1, d)), _const_spec(w_in_r.shape),
                  _const_spec(gq.shape), _const_spec(wuq.shape), _const_spec(gkv.shape),
                  _const_spec(wukv.shape), row(LANE), row(LANE), row(LANE), row(LANE)],
        out_specs=[row(W_MLA_OUT), row(3 * W_BRANCH), row(3 * W_BRANCH), row(3 * D_MODEL)],
        out_shape=[jax.ShapeDtypeStruct((b, s, W_MLA_OUT), BF16),
                   jax.ShapeDtypeStruct((b, s, 3 * W_BRANCH), BF16),
                   jax.ShapeDtypeStruct((b, s, 3 * W_BRANCH), BF16),
                   jax.ShapeDtypeStruct((b, s, 3 * D_MODEL), BF16)],
        compiler_params=pltpu.CompilerParams(
            dimension_semantics=("arbitrary", "arbitrary"), vmem_limit_bytes=VMEM_LIMIT),
        name="inproj",
    )(x, shift, scale, g1, w_in_r, gq, wuq, gkv, wukv, cosa, sina, cosm, sinm)


def _transpose_values(v_ref, vt_ref, nkv):
    for j in range(nkv):
        for c in range(W_BRANCH // LANE):
            blk = v_ref[0, j * TQ:(j + 1) * TQ, c * LANE:(c + 1) * LANE]
            vt_ref[j, c * LANE:(c + 1) * LANE, :] = blk.astype(F32).T.astype(BF16)


def _store_heads(o_ref, acc_ref, l_ref):
    for c in range(W_BRANCH // LANE):
        parts = []
        for h in (2 * c, 2 * c + 1):
            part = acc_ref[h * HEAD:(h + 1) * HEAD, :]
            if l_ref is not None:
                part = part / l_ref[h:h + 1, :]
            parts.append(part)
        o_ref[0, :, c * LANE:(c + 1) * LANE] = jnp.concatenate(parts, axis=0).T.astype(BF16)


def _head_query(q_ref, h):
    c = h // 2
    q = q_ref[0, :, c * LANE:(c + 1) * LANE]
    lane = lax.broadcasted_iota(jnp.int32, (1, LANE), 1)
    lo = (h % 2) * HEAD
    return jnp.where((lane >= lo) & (lane < lo + HEAD), q, jnp.zeros_like(q))


def _causal(strict):
    key = lax.broadcasted_iota(jnp.int32, (TQ, TQ), 0)
    qry = lax.broadcasted_iota(jnp.int32, (TQ, TQ), 1)
    return key < qry if strict else key <= qry


def _softmax_block(h, s, vt, m_ref, l_ref, acc_ref, first):
    rows = slice(h * HEAD, (h + 1) * HEAD)
    m_blk = jnp.max(s, axis=0, keepdims=True)
    if first:
        p = jnp.exp(s - m_blk)
        m_ref[h:h + 1, :] = m_blk
        l_ref[h:h + 1, :] = jnp.sum(p, axis=0, keepdims=True)
        acc_ref[rows, :] = _dot(vt, p.astype(BF16))
    else:
        m_old = m_ref[h:h + 1, :]
        m_new = jnp.maximum(m_old, m_blk)
        alpha = jnp.exp(m_old - m_new)
        p = jnp.exp(s - m_new)
        m_ref[h:h + 1, :] = m_new
        l_ref[h:h + 1, :] = alpha * l_ref[h:h + 1, :] + jnp.sum(p, axis=0, keepdims=True)
        acc_ref[rows, :] = alpha * acc_ref[rows, :] + _dot(vt, p.astype(BF16))


def _mla_kernel(q_ref, k_ref, v_ref, o_ref, vt_ref, m_ref, l_ref, acc_ref, *, nkv):
    qi = pl.program_id(1)

    @pl.when(qi == 0)
    def _():
        _transpose_values(v_ref, vt_ref, nkv)

    def block(j, first):
        row0 = pl.multiple_of(j * TQ, TQ)
        scores = [_dot_nt(k_ref[0, pl.ds(row0, TQ), h * LANE:(h + 1) * LANE],
                          q_ref[0, :, h * LANE:(h + 1) * LANE]) for h in range(N_HEADS)]
        for h in range(N_HEADS):
            s = jnp.where(_causal(False), scores[h], NEG_INF) if first else scores[h]
            _softmax_block(h, s, vt_ref[j, h * MLA_V:(h + 1) * MLA_V, :], m_ref, l_ref, acc_ref,
                           first)

    def body(j, carry):
        block(j, False)
        return carry

    block(qi, True)
    lax.fori_loop(0, qi, body, 0)
    _store_heads(o_ref, acc_ref, l_ref)


def _sb_kernel(q_ref, k_ref, v_ref, o_ref, vt_ref, run_ref, acc_ref, *, nkv):
    qi = pl.program_id(1)

    @pl.when(qi == 0)
    def _():
        _transpose_values(v_ref, vt_ref, nkv)

    def block(j, first):
        row0 = pl.multiple_of(j * TQ, TQ)
        suffix = jnp.where(_causal(False), 1.0, 0.0).astype(BF16)
        zs = [_dot_nt(k_ref[0, pl.ds(row0, TQ), (h // 2) * LANE:(h // 2 + 1) * LANE],
                      _head_query(q_ref, h)) for h in range(N_HEADS)]
        for h in range(N_HEADS):
            rows = slice(h * HEAD, (h + 1) * HEAD)
            z = zs[h]
            log_beta = jnp.minimum(z, 0.0) - jnp.log(1.0 + jnp.exp(-jnp.abs(z)))
            log_keep = log_beta - z
            if first:
                log_keep = jnp.where(_causal(True), log_keep, 0.0)
            hi = log_keep.astype(BF16)
            lo = (log_keep - hi.astype(F32)).astype(BF16)
            incl = _dot(suffix, hi) + _dot(suffix, lo)
            if first:
                w = jnp.where(_causal(True), jnp.exp(log_beta + (incl - log_keep)), 0.0)
                run_ref[h:h + 1, :] = incl[0:1, :]
                acc_ref[rows, :] = _dot(vt_ref[j, rows, :], w.astype(BF16))
            else:
                run = run_ref[h:h + 1, :]
                w = jnp.exp(log_beta + (incl - log_keep + run))
                run_ref[h:h + 1, :] = run + incl[0:1, :]
                acc_ref[rows, :] = acc_ref[rows, :] + _dot(vt_ref[j, rows, :], w.astype(BF16))

    def body(i, carry):
        block(qi - 1 - i, False)
        return carry

    block(qi, True)
    lax.fori_loop(0, qi, body, 0)
    _store_heads(o_ref, acc_ref, None)


def _moba_kernel(q_ref, k_ref, v_ref, o_ref, vt_ref, m_ref, l_ref, acc_ref, km_ref, sel_ref, *,
                 nkv, nbp):
    qi = pl.program_id(1)
    lane = lax.broadcasted_iota(jnp.int32, (1, W_BRANCH), 1)

    @pl.when(qi == 0)
    def _():
        _transpose_values(v_ref, vt_ref, nkv)
        km_ref[...] = jnp.zeros_like(km_ref)
        for j in range(nkv):
            mean = jnp.mean(k_ref[0, j * TQ:(j + 1) * TQ, :].astype(F32), axis=0, keepdims=True)
            for h in range(N_HEADS):
                in_head = (lane >= h * HEAD) & (lane < (h + 1) * HEAD)
                km_ref[h * nbp + j:h * nbp + j + 1, :] = jnp.where(in_head, mean, 0.0)

    gates = lax.dot_general(km_ref[...], q_ref[0].astype(F32), (((1,), (1,)), ((), ())),
                            precision=lax.Precision.HIGHEST, preferred_element_type=F32)
    blk = lax.broadcasted_iota(jnp.int32, (nbp, TQ), 0)
    for h in range(N_HEADS):
        gate = jnp.where(blk < qi, gates[h * nbp:(h + 1) * nbp, :], NEG_INF)
        rank = jnp.zeros((nbp, TQ), jnp.int32)
        for j in range(nkv):
            gj = gate[j:j + 1, :]
            ahead = (gj > gate) | ((gj == gate) & (blk > j))
            rank = rank + jnp.where(ahead, 1, 0)
        sel = (rank < MOBA_TOPK) & (rank < qi)
        sel_ref[h * nbp:(h + 1) * nbp, :] = jnp.where(sel, 1.0, 0.0)

    def block(j, first):
        row0 = pl.multiple_of(j * TQ, TQ)
        scores = [_dot_nt(k_ref[0, pl.ds(row0, TQ), (h // 2) * LANE:(h // 2 + 1) * LANE],
                          _head_query(q_ref, h)) for h in range(N_HEADS)]
        for h in range(N_HEADS):
            if first:
                s = jnp.where(_causal(False), scores[h], NEG_INF)
            else:
                s = jnp.where(sel_ref[pl.ds(h * nbp + j, 1), :] > 0.5, scores[h], NEG_INF)
            _softmax_block(h, s, vt_ref[j, h * HEAD:(h + 1) * HEAD, :], m_ref, l_ref, acc_ref, first)

    def body(j, carry):
        block(j, False)
        return carry

    block(qi, True)
    lax.fori_loop(0, qi, body, 0)
    _store_heads(o_ref, acc_ref, l_ref)


def _attention(kind, src, qk_width, v_col):
    b, s, _ = src.shape
    nkv = s // TQ
    vt = pltpu.VMEM((nkv, W_BRANCH, TQ), BF16)
    stat = pltpu.VMEM((N_HEADS, TQ), F32)
    acc = pltpu.VMEM((W_BRANCH, TQ), F32)
    if kind == "mla":
        body = functools.partial(_mla_kernel, nkv=nkv)
        scratch = [vt, stat, stat, acc]
    elif kind == "sb":
        body = functools.partial(_sb_kernel, nkv=nkv)
        scratch = [vt, stat, acc]
    else:
        nbp = -(-nkv // 8) * 8
        body = functools.partial(_moba_kernel, nkv=nkv, nbp=nbp)
        scratch = [vt, stat, stat, acc, pltpu.VMEM((N_HEADS * nbp, W_BRANCH), F32),
                   pltpu.VMEM((N_HEADS * nbp, TQ), F32)]
    return pl.pallas_call(
        body,
        grid=(b, nkv),
        in_specs=[
            pl.BlockSpec((1, TQ, qk_width), lambda i, t: (i, t, 0)),
            pl.BlockSpec((1, s, qk_width), lambda i, t: (i, 0, 1)),
            pl.BlockSpec((1, s, W_BRANCH), lambda i, t: (i, 0, v_col)),
        ],
        out_specs=pl.BlockSpec((1, TQ, W_BRANCH), lambda i, t: (i, t, 0)),
        out_shape=jax.ShapeDtypeStruct((b, s, W_BRANCH), BF16),
        scratch_shapes=scratch,
        compiler_params=pltpu.CompilerParams(
            dimension_semantics=("arbitrary", "arbitrary"), vmem_limit_bytes=VMEM_LIMIT),
        name="attn_" + kind,
    )(src, src, src)


def _merge_kernel(x_ref, gate1_ref, oa_ref, ob_ref, oc_ref, g_ref, wa_ref, wb_ref, wc_ref, wo_ref,
                  o_ref):
    merged = g_ref[0, :, :D_MODEL].astype(F32) * _dot(oa_ref[0], wa_ref[...])
    merged = merged + g_ref[0, :, D_MODEL:2 * D_MODEL].astype(F32) * _dot(ob_ref[0], wb_ref[...])
    merged = merged + g_ref[0, :, 2 * D_MODEL:].astype(F32) * _dot(oc_ref[0], wc_ref[...])
    o_ref[0] = x_ref[0] + gate1_ref[0] * _dot(merged.astype(BF16), wo_ref[...])


def _merge(x, gate1, oa, ob, oc, gates, wa, wb, wc, wo):
    b, s, d = x.shape
    tm = 512
    row = lambda w: pl.BlockSpec((1, tm, w), lambda i, j: (i, j, 0))
    vec = pl.BlockSpec((1, 1, d), lambda i, j: (i, 0, 0))
    return pl.pallas_call(
        _merge_kernel,
        grid=(b, s // tm),
        in_specs=[row(d), vec, row(W_BRANCH), row(W_BRANCH), row(W_BRANCH), row(3 * d),
                  _const_spec(wa.shape), _const_spec(wb.shape), _const_spec(wc.shape),
                  _const_spec(wo.shape)],
        out_specs=row(d),
        out_shape=jax.ShapeDtypeStruct((b, s, d), F32),
        compiler_params=pltpu.CompilerParams(
            dimension_semantics=("arbitrary", "arbitrary"), vmem_limit_bytes=VMEM_LIMIT),
        name="merge",
    )(x, gate1, oa, ob, oc, gates, wa, wb, wc, wo)


def _ffn_kernel(x_ref, shift_ref, scale_ref, gate_ref, g2_ref, w1_ref, w2_ref, gf_ref, o_ref, *,
                final_norm):
    x = x_ref[0]
    h = (_rms(x, g2_ref[...]) * (1.0 + scale_ref[0]) + shift_ref[0]).astype(BF16)
    acc = jnp.zeros_like(x)
    chunk = D_MODEL
    for c in range(D_FF // chunk):
        u = jnp.maximum(_dot(h, w1_ref[:, c * chunk:(c + 1) * chunk]), 0.0)
        acc = acc + _dot((u * u).astype(BF16), w2_ref[c * chunk:(c + 1) * chunk, :])
    y = x + gate_ref[0] * acc
    if final_norm:
        y = _rms(y, gf_ref[...])
    o_ref[0] = y


def _ffn(x, shift, scale, gate, g2, w1, w2, gf, final_norm):
    b, s, d = x.shape
    tm = 512
    row = pl.BlockSpec((1, tm, d), lambda i, j: (i, j, 0))
    vec = pl.BlockSpec((1, 1, d), lambda i, j: (i, 0, 0))
    return pl.pallas_call(
        functools.partial(_ffn_kernel, final_norm=final_norm),
        grid=(b, s // tm),
        in_specs=[row, vec, vec, vec, _const_spec((1, d)), _const_spec(w1.shape),
                  _const_spec(w2.shape), _const_spec((1, d))],
        out_specs=row,
        out_shape=jax.ShapeDtypeStruct((b, s, d), F32),
        compiler_params=pltpu.CompilerParams(
            dimension_semantics=("arbitrary", "arbitrary"), vmem_limit_bytes=VMEM_LIMIT),
        name="ffn",
    )(x, shift, scale, gate, g2, w1, w2, gf)


def _rope_tables(positions):
    pos = positions.astype(F32)[..., None]

    def angles(rot_dim):
        inv = ROPE_THETA ** (-jnp.arange(0, rot_dim, 2, dtype=F32) / rot_dim)
        ang = pos * inv
        return jnp.cos(ang), jnp.sin(ang)

    b, s = positions.shape
    ca, sa = angles(MLA_ROPE)
    one = lambda n: jnp.ones((b, s, n), F32)
    zero = lambda n: jnp.zeros((b, s, n), F32)
    cosa = jnp.concatenate([one(MLA_NOPE), ca, ca, one(LANE - MLA_QK)], axis=-1)
    sina = jnp.concatenate([zero(MLA_NOPE), -sa, sa, zero(LANE - MLA_QK)], axis=-1)
    cm, sm = angles(MOBA_ROT)
    cos_h = jnp.concatenate([cm, cm, one(HEAD - MOBA_ROT)], axis=-1)
    sin_h = jnp.concatenate([-sm, sm, zero(HEAD - MOBA_ROT)], axis=-1)
    cosm = jnp.concatenate([cos_h, cos_h], axis=-1)
    sinm = jnp.concatenate([sin_h, sin_h], axis=-1)
    return cosa, sina, cosm, sinm


def _layer_weights(w_in, w_uq, w_ukv):
    d = w_in.shape[0]
    o_kpe = MLA_Q_RANK + MLA_KV_RANK
    o_sb = o_kpe + MLA_ROPE
    kpe = jnp.concatenate([jnp.zeros((d, MLA_NOPE), F32), w_in[:, o_kpe:o_sb],
                           jnp.zeros((d, LANE - MLA_QK), F32)], axis=1)
    w_in_r = jnp.concatenate([w_in[:, :o_kpe], kpe, w_in[:, o_sb:]], axis=1).astype(BF16)
    wuq = jnp.pad(w_uq.reshape(MLA_Q_RANK, N_HEADS, MLA_QK),
                  ((0, 0), (0, 0), (0, LANE - MLA_QK))).reshape(MLA_Q_RANK, N_HEADS * LANE)
    ukv = w_ukv.reshape(MLA_KV_RANK, N_HEADS, MLA_NOPE + MLA_V)
    wk = jnp.pad(ukv[:, :, :MLA_NOPE], ((0, 0), (0, 0), (0, LANE - MLA_NOPE)))
    wukv = jnp.concatenate([wk.reshape(MLA_KV_RANK, N_HEADS * LANE),
                            ukv[:, :, MLA_NOPE:].reshape(MLA_KV_RANK, N_HEADS * MLA_V)], axis=1)
    return w_in_r, wuq.astype(BF16), wukv.astype(BF16)


def kernel(x, c, positions, w_ada, b_ada, norm1_g, norm2_g, w_in, q_norm_g, w_uq, kv_norm_g, w_ukv,
           w_o_mla, w_o_sb, w_o_moba, w_out, w_ff1, w_ff2, final_norm_g):
    b, s, d = x.shape
    depth = w_ada.shape[0]
    assert d == D_MODEL and s % 512 == 0 and w_in.shape[-1] == C_END - (LANE - MLA_ROPE)
    cosa, sina, cosm, sinm = _rope_tables(positions)
    mod = _adaln_mod(c, w_ada, b_ada)
    gf = final_norm_g.reshape(1, d)
    for l in range(depth):
        shift1, scale1, gate1, shift2, scale2, gate2 = [
            m.reshape(b, 1, d) for m in jnp.split(mod[l], 6, axis=-1)]
        w_in_r, wuq, wukv = _layer_weights(w_in[l], w_uq[l], w_ukv[l])
        mla, sb, mb, gates = _inproj(
            x, shift1, scale1, norm1_g[l].reshape(1, d), w_in_r, q_norm_g[l].reshape(1, -1), wuq,
            kv_norm_g[l].reshape(1, -1), wukv, cosa, sina, cosm, sinm)
        o_mla = _attention("mla", mla, N_HEADS * LANE, 4)
        o_sb = _attention("sb", sb, W_BRANCH, 2)
        o_mb = _attention("moba", mb, W_BRANCH, 2)
        x = _merge(x, gate1, o_mla, o_sb, o_mb, gates, w_o_mla[l].astype(BF16),
                   w_o_sb[l].astype(BF16), w_o_moba[l].astype(BF16), w_out[l].astype(BF16))
        x = _ffn(x, shift2, scale2, gate2, norm2_g[l].reshape(1, d), w_ff1[l].astype(BF16),
                 w_ff2[l].astype(BF16), gf, l == depth - 1)
    return x
```

```python
import functools
import math

import jax
import jax.numpy as jnp
from jax import lax
from jax.experimental import pallas as pl
from jax.experimental.pallas import tpu as pltpu

F32 = jnp.float32
BF16 = jnp.bfloat16

D_MODEL = 1024
N_HEADS = 8
MLA_NOPE = 64
MLA_ROPE = 32
MLA_V = 64
MLA_QK = MLA_NOPE + MLA_ROPE
MLA_Q_RANK = 768
MLA_KV_RANK = 256
HEAD = 64
W_BRANCH = N_HEADS * HEAD
MOBA_BLOCK = 256
MOBA_TOPK = 3
MOBA_ROT = HEAD // 4
ROPE_THETA = 500000.0
D_FF = 4 * D_MODEL
EPS = 1e-6
NEG_INF = -1e30
SB_DEAD = -104.0
LOG2E = 1.4426950408889634

LANE = 128
TQ = 256
VMEM_LIMIT = 56 * 1024 * 1024

C_QLAT = 0
C_CKV = C_QLAT + MLA_Q_RANK
C_KPE = C_CKV + MLA_KV_RANK
C_SB = C_KPE + LANE
C_MB = C_SB + 3 * W_BRANCH
C_GATE = C_MB + 3 * W_BRANCH
C_END = C_GATE + 3 * D_MODEL
W_MLA_OUT = 2 * N_HEADS * LANE + N_HEADS * MLA_V


def _rms(xf, g):
    return xf * lax.rsqrt(jnp.mean(xf * xf, axis=-1, keepdims=True) + EPS) * g


def _dot(a, b):
    return jnp.dot(a, b, preferred_element_type=F32)


def _dot_nt(a, b):
    return lax.dot_general(a, b, (((1,), (1,)), ((), ())), preferred_element_type=F32)


def _rope_lanes(xb, cos, sin, half, first_half):
    rot = jnp.where(first_half, pltpu.roll(xb, LANE - half, 1), pltpu.roll(xb, half, 1))
    return xb * cos + rot * sin


def _mod_kernel(c_ref, w_ref, b_ref, o_ref):
    c = c_ref[...]
    c_act = c * (1.0 / (1.0 + jnp.exp(-c)))
    o_ref[0] = _dot(c_act.astype(BF16), w_ref[0].astype(BF16)) + b_ref[0]


def _adaln_mod(c, w_ada, b_ada):
    depth, d, n = w_ada.shape
    b = c.shape[0]
    tn = 1536
    return pl.pallas_call(
        _mod_kernel,
        grid=(depth, n // tn),
        in_specs=[
            pl.BlockSpec((b, d), lambda l, j: (0, 0)),
            pl.BlockSpec((1, d, tn), lambda l, j: (l, 0, j)),
            pl.BlockSpec((1, 1, tn), lambda l, j: (l, 0, j)),
        ],
        out_specs=pl.BlockSpec((1, b, tn), lambda l, j: (l, 0, j)),
        out_shape=jax.ShapeDtypeStruct((depth, b, n), F32),
        compiler_params=pltpu.CompilerParams(
            dimension_semantics=("arbitrary", "arbitrary"), vmem_limit_bytes=VMEM_LIMIT),
        name="adaln_mod",
    )(c, w_ada, b_ada.reshape(depth, 1, n))


def _inproj_kernel(x_ref, shift_ref, scale_ref, g1_ref, w_ref, gq_ref, wuq_ref, gkv_ref, wukv_ref,
                   cosa_ref, sina_ref, cosm_ref, sinm_ref,
                   mla_ref, sb_ref, mb_ref, gate_ref):
    x = x_ref[0]
    h = _rms(x, g1_ref[...]) * (1.0 + scale_ref[0]) + shift_ref[0]
    h = h.astype(BF16)
    lane = lax.broadcasted_iota(jnp.int32, (1, LANE), 1)

    cosa, sina = cosa_ref[0], sina_ref[0]
    mla_first = lane < MLA_NOPE + MLA_ROPE // 2
    mla_scale = LOG2E / math.sqrt(MLA_QK)
    qlat = _dot(h, w_ref[:, C_QLAT:C_CKV])
    q = _dot(_rms(qlat, gq_ref[...]).astype(BF16), wuq_ref[...])
    for hd in range(N_HEADS):
        qb = _rope_lanes(q[:, hd * LANE:(hd + 1) * LANE], cosa, sina, MLA_ROPE // 2, mla_first)
        mla_ref[0, :, hd * LANE:(hd + 1) * LANE] = (qb * mla_scale).astype(BF16)

    kpe = _rope_lanes(_dot(h, w_ref[:, C_KPE:C_SB]), cosa, sina, MLA_ROPE // 2, mla_first)
    ckv = _dot(h, w_ref[:, C_CKV:C_KPE])
    kv = _dot(_rms(ckv, gkv_ref[...]).astype(BF16), wukv_ref[...])
    k_off = N_HEADS * LANE
    for hd in range(N_HEADS):
        kb = kv[:, hd * LANE:(hd + 1) * LANE] + kpe
        mla_ref[0, :, k_off + hd * LANE:k_off + (hd + 1) * LANE] = kb.astype(BF16)
    mla_ref[0, :, 2 * k_off:] = kv[:, k_off:].astype(BF16)

    sb_scale = 1.0 / math.sqrt(HEAD)
    sb_ref[0, :, :W_BRANCH] = (_dot(h, w_ref[:, C_SB:C_SB + W_BRANCH]) * sb_scale).astype(BF16)
    sb_ref[0, :, W_BRANCH:] = _dot(h, w_ref[:, C_SB + W_BRANCH:C_MB]).astype(BF16)

    cosm, sinm = cosm_ref[0], sinm_ref[0]
    mb_first = (lane % HEAD) < MOBA_ROT // 2
    for part, scale in ((0, sb_scale * LOG2E), (1, 1.0)):
        c0 = C_MB + part * W_BRANCH
        acc = _dot(h, w_ref[:, c0:c0 + W_BRANCH])
        for cb in range(W_BRANCH // LANE):
            blk = _rope_lanes(acc[:, cb * LANE:(cb + 1) * LANE], cosm, sinm, MOBA_ROT // 2, mb_first)
            mb_ref[0, :, part * W_BRANCH + cb * LANE:part * W_BRANCH + (cb + 1) * LANE] = (
                blk * scale).astype(BF16)
    mb_ref[0, :, 2 * W_BRANCH:] = _dot(h, w_ref[:, C_MB + 2 * W_BRANCH:C_GATE]).astype(BF16)

    for br in range(3):
        c0 = C_GATE + br * D_MODEL
        gl = _dot(h, w_ref[:, c0:c0 + D_MODEL])
        gate_ref[0, :, br * D_MODEL:(br + 1) * D_MODEL] = (1.0 / (1.0 + jnp.exp(-gl))).astype(BF16)


def _const_spec(shape):
    nd = len(shape)
    return pl.BlockSpec(shape, lambda *_: (0,) * nd, pipeline_mode=pl.Buffered(1))


def _inproj(x, shift, scale, g1, w_in_r, gq, wuq, gkv, wukv, cosa, sina, cosm, sinm):
    b, s, d = x.shape
    tm = TQ
    row = lambda w: pl.BlockSpec((1, tm, w), lambda i, j: (i, j, 0))
    vec = pl.BlockSpec((1, 1, d), lambda i, j: (i, 0, 0))
    return pl.pallas_call(
        _inproj_kernel,
        grid=(b, s // tm),
        in_specs=[row(d), vec, vec, _const_spec((1, d)), _const_spec(w_in_r.shape),
                  _const_spec(gq.shape), _const_spec(wuq.shape), _const_spec(gkv.shape),
                  _const_spec(wukv.shape), row(LANE), row(LANE), row(LANE), row(LANE)],
        out_specs=[row(W_MLA_OUT), row(3 * W_BRANCH), row(3 * W_BRANCH), row(3 * D_MODEL)],
        out_shape=[jax.ShapeDtypeStruct((b, s, W_MLA_OUT), BF16),
                   jax.ShapeDtypeStruct((b, s, 3 * W_BRANCH), BF16),
                   jax.ShapeDtypeStruct((b, s, 3 * W_BRANCH), BF16),
                   jax.ShapeDtypeStruct((b, s, 3 * D_MODEL), BF16)],
        compiler_params=pltpu.CompilerParams(
            dimension_semantics=("arbitrary", "arbitrary"), vmem_limit_bytes=VMEM_LIMIT),
        name="inproj",
    )(x, shift, scale, g1, w_in_r, gq, wuq, gkv, wukv, cosa, sina, cosm, sinm)


def _transpose_values(v_ref, vt_ref, nkv):
    for j in range(nkv):
        for c in range(W_BRANCH // LANE):
            blk = v_ref[0, j * TQ:(j + 1) * TQ, c * LANE:(c + 1) * LANE]
            vt_ref[j, c * LANE:(c + 1) * LANE, :] = blk.astype(F32).T.astype(BF16)


def _store_heads(o_ref, acc_ref, l_ref):
    for c in range(W_BRANCH // LANE):
        parts = []
        for h in (2 * c, 2 * c + 1):
            part = acc_ref[h * HEAD:(h + 1) * HEAD, :]
            if l_ref is not None:
                part = part / l_ref[h:h + 1, :]
            parts.append(part)
        o_ref[0, :, c * LANE:(c + 1) * LANE] = jnp.concatenate(parts, axis=0).T.astype(BF16)


def _head_query(q_ref, h):
    c = h // 2
    q = q_ref[0, :, c * LANE:(c + 1) * LANE]
    lane = lax.broadcasted_iota(jnp.int32, (1, LANE), 1)
    lo = (h % 2) * HEAD
    return jnp.where((lane >= lo) & (lane < lo + HEAD), q, jnp.zeros_like(q))


def _causal(strict):
    key = lax.broadcasted_iota(jnp.int32, (TQ, TQ), 0)
    qry = lax.broadcasted_iota(jnp.int32, (TQ, TQ), 1)
    return key < qry if strict else key <= qry


def _softmax_block(h, s, vt, m_ref, l_ref, acc_ref, first):
    rows = slice(h * HEAD, (h + 1) * HEAD)
    m_blk = jnp.max(s, axis=0, keepdims=True)
    if first:
        p = jnp.exp2(s - m_blk)
        m_ref[h:h + 1, :] = m_blk
        l_ref[h:h + 1, :] = jnp.sum(p, axis=0, keepdims=True)
        acc_ref[rows, :] = _dot(vt, p.astype(BF16))
    else:
        m_old = m_ref[h:h + 1, :]
        m_new = jnp.maximum(m_old, m_blk)
        alpha = jnp.exp2(m_old - m_new)
        p = jnp.exp2(s - m_new)
        m_ref[h:h + 1, :] = m_new
        l_ref[h:h + 1, :] = alpha * l_ref[h:h + 1, :] + jnp.sum(p, axis=0, keepdims=True)
        acc_ref[rows, :] = alpha * acc_ref[rows, :] + _dot(vt, p.astype(BF16))


def _mla_kernel(q_ref, k_ref, v_ref, o_ref, vt_ref, m_ref, l_ref, acc_ref, *, nkv):
    qi = pl.program_id(1)

    @pl.when(qi == 0)
    def _():
        _transpose_values(v_ref, vt_ref, nkv)

    def block(j, first):
        row0 = pl.multiple_of(j * TQ, TQ)
        scores = [_dot_nt(k_ref[0, pl.ds(row0, TQ), h * LANE:(h + 1) * LANE],
                          q_ref[0, :, h * LANE:(h + 1) * LANE]) for h in range(N_HEADS)]
        for h in range(N_HEADS):
            s = jnp.where(_causal(False), scores[h], NEG_INF) if first else scores[h]
            _softmax_block(h, s, vt_ref[j, h * MLA_V:(h + 1) * MLA_V, :], m_ref, l_ref, acc_ref,
                           first)

    def body(j, carry):
        block(j, False)
        return carry

    block(qi, True)
    lax.fori_loop(0, qi, body, 0)
    _store_heads(o_ref, acc_ref, l_ref)


def _sb_kernel(q_ref, k_ref, v_ref, o_ref, vt_ref, run_ref, acc_ref, *, nkv):
    qi = pl.program_id(1)

    @pl.when(qi == 0)
    def _():
        _transpose_values(v_ref, vt_ref, nkv)

    def block(j, first):
        row0 = pl.multiple_of(j * TQ, TQ)
        suffix = jnp.where(_causal(False), 1.0, 0.0).astype(BF16)
        zs = [_dot_nt(k_ref[0, pl.ds(row0, TQ), (h // 2) * LANE:(h // 2 + 1) * LANE],
                      _head_query(q_ref, h)) for h in range(N_HEADS)]
        for h in range(N_HEADS):
            rows = slice(h * HEAD, (h + 1) * HEAD)
            z = zs[h]
            log_beta = jnp.minimum(z, 0.0) - jnp.log(1.0 + jnp.exp(-jnp.abs(z)))
            log_keep = log_beta - z
            if first:
                log_keep = jnp.where(_causal(True), log_keep, 0.0)
            hi = log_keep.astype(BF16)
            lo = (log_keep - hi.astype(F32)).astype(BF16)
            incl = _dot(suffix, hi) + _dot(suffix, lo)
            if first:
                w = jnp.where(_causal(True), jnp.exp(log_beta + (incl - log_keep)), 0.0)
                run_ref[h:h + 1, :] = incl[0:1, :]
                acc_ref[rows, :] = _dot(vt_ref[j, rows, :], w.astype(BF16))
            else:
                run = run_ref[h:h + 1, :]
                w = jnp.exp(log_beta + (incl - log_keep + run))
                run_ref[h:h + 1, :] = run + incl[0:1, :]
                acc_ref[rows, :] = acc_ref[rows, :] + _dot(vt_ref[j, rows, :], w.astype(BF16))

    def live(carry):
        i, top = carry
        return (i < qi) & (top > SB_DEAD)

    def body(carry):
        i, _ = carry
        block(qi - 1 - i, False)
        return i + 1, jnp.max(run_ref[...])

    block(qi, True)
    lax.while_loop(live, body, (jnp.int32(0), jnp.max(run_ref[...])))
    _store_heads(o_ref, acc_ref, None)


def _moba_kernel(q_ref, k_ref, v_ref, o_ref, vt_ref, m_ref, l_ref, acc_ref, km_ref, sel_ref, *,
                 nkv, nbp):
    qi = pl.program_id(1)
    lane = lax.broadcasted_iota(jnp.int32, (1, W_BRANCH), 1)

    @pl.when(qi == 0)
    def _():
        _transpose_values(v_ref, vt_ref, nkv)
        km_ref[...] = jnp.zeros_like(km_ref)
        for j in range(nkv):
            mean = jnp.mean(k_ref[0, j * TQ:(j + 1) * TQ, :].astype(F32), axis=0, keepdims=True)
            for h in range(N_HEADS):
                in_head = (lane >= h * HEAD) & (lane < (h + 1) * HEAD)
                km_ref[h * nbp + j:h * nbp + j + 1, :] = jnp.where(in_head, mean, 0.0)

    gates = lax.dot_general(km_ref[...], q_ref[0].astype(F32), (((1,), (1,)), ((), ())),
                            precision=lax.Precision.HIGHEST, preferred_element_type=F32)
    blk = lax.broadcasted_iota(jnp.int32, (nbp, TQ), 0)
    for h in range(N_HEADS):
        gate = jnp.where(blk < qi, gates[h * nbp:(h + 1) * nbp, :], NEG_INF)
        rank = jnp.zeros((nbp, TQ), jnp.int32)
        for j in range(nkv):
            gj = gate[j:j + 1, :]
            ahead = (gj > gate) | ((gj == gate) & (blk > j))
            rank = rank + jnp.where(ahead, 1, 0)
        sel = (rank < MOBA_TOPK) & (rank < qi)
        sel_ref[h * nbp:(h + 1) * nbp, :] = jnp.where(sel, 1.0, 0.0)

    def block(j, first):
        row0 = pl.multiple_of(j * TQ, TQ)
        scores = [_dot_nt(k_ref[0, pl.ds(row0, TQ), (h // 2) * LANE:(h // 2 + 1) * LANE],
                          _head_query(q_ref, h)) for h in range(N_HEADS)]
        for h in range(N_HEADS):
            if first:
                s = jnp.where(_causal(False), scores[h], NEG_INF)
            else:
                s = jnp.where(sel_ref[pl.ds(h * nbp + j, 1), :] > 0.5, scores[h], NEG_INF)
            _softmax_block(h, s, vt_ref[j, h * HEAD:(h + 1) * HEAD, :], m_ref, l_ref, acc_ref, first)

    def body(j, carry):
        block(j, False)
        return carry

    block(qi, True)
    lax.fori_loop(0, qi, body, 0)
    _store_heads(o_ref, acc_ref, l_ref)


def _attention(kind, src, qk_width, v_col):
    b, s, _ = src.shape
    nkv = s // TQ
    vt = pltpu.VMEM((nkv, W_BRANCH, TQ), BF16)
    stat = pltpu.VMEM((N_HEADS, TQ), F32)
    acc = pltpu.VMEM((W_BRANCH, TQ), F32)
    if kind == "mla":
        body = functools.partial(_mla_kernel, nkv=nkv)
        scratch = [vt, stat, stat, acc]
    elif kind == "sb":
        body = functools.partial(_sb_kernel, nkv=nkv)
        scratch = [vt, stat, acc]
    else:
        nbp = -(-nkv // 8) * 8
        body = functools.partial(_moba_kernel, nkv=nkv, nbp=nbp)
        scratch = [vt, stat, stat, acc, pltpu.VMEM((N_HEADS * nbp, W_BRANCH), F32),
                   pltpu.VMEM((N_HEADS * nbp, TQ), F32)]
    return pl.pallas_call(
        body,
        grid=(b, nkv),
        in_specs=[
            pl.BlockSpec((1, TQ, qk_width), lambda i, t: (i, t, 0)),
            pl.BlockSpec((1, s, qk_width), lambda i, t: (i, 0, 1)),
            pl.BlockSpec((1, s, W_BRANCH), lambda i, t: (i, 0, v_col)),
        ],
        out_specs=pl.BlockSpec((1, TQ, W_BRANCH), lambda i, t: (i, t, 0)),
        out_shape=jax.ShapeDtypeStruct((b, s, W_BRANCH), BF16),
        scratch_shapes=scratch,
        compiler_params=pltpu.CompilerParams(
            dimension_semantics=("arbitrary", "arbitrary"), vmem_limit_bytes=VMEM_LIMIT),
        name="attn_" + kind,
    )(src, src, src)


def _merge_kernel(x_ref, gate1_ref, oa_ref, ob_ref, oc_ref, g_ref, wa_ref, wb_ref, wc_ref, wo_ref,
                  o_ref):
    merged = g_ref[0, :, :D_MODEL].astype(F32) * _dot(oa_ref[0], wa_ref[...])
    merged = merged + g_ref[0, :, D_MODEL:2 * D_MODEL].astype(F32) * _dot(ob_ref[0], wb_ref[...])
    merged = merged + g_ref[0, :, 2 * D_MODEL:].astype(F32) * _dot(oc_ref[0], wc_ref[...])
    o_ref[0] = x_ref[0] + gate1_ref[0] * _dot(merged.astype(BF16), wo_ref[...])


def _merge(x, gate1, oa, ob, oc, gates, wa, wb, wc, wo):
    b, s, d = x.shape
    tm = 512
    row = lambda w: pl.BlockSpec((1, tm, w), lambda i, j: (i, j, 0))
    vec = pl.BlockSpec((1, 1, d), lambda i, j: (i, 0, 0))
    return pl.pallas_call(
        _merge_kernel,
        grid=(b, s // tm),
        in_specs=[row(d), vec, row(W_BRANCH), row(W_BRANCH), row(W_BRANCH), row(3 * d),
                  _const_spec(wa.shape), _const_spec(wb.shape), _const_spec(wc.shape),
                  _const_spec(wo.shape)],
        out_specs=row(d),
        out_shape=jax.ShapeDtypeStruct((b, s, d), F32),
        compiler_params=pltpu.CompilerParams(
            dimension_semantics=("arbitrary", "arbitrary"), vmem_limit_bytes=VMEM_LIMIT),
        name="merge",
    )(x, gate1, oa, ob, oc, gates, wa, wb, wc, wo)


def _ffn_kernel(x_ref, shift_ref, scale_ref, gate_ref, g2_ref, w1_ref, w2_ref, gf_ref, o_ref, *,
                final_norm):
    x = x_ref[0]
    h = (_rms(x, g2_ref[...]) * (1.0 + scale_ref[0]) + shift_ref[0]).astype(BF16)
    acc = jnp.zeros_like(x)
    chunk = D_MODEL
    for c in range(D_FF // chunk):
        u = jnp.maximum(_dot(h, w1_ref[:, c * chunk:(c + 1) * chunk]), 0.0)
        acc = acc + _dot((u * u).astype(BF16), w2_ref[c * chunk:(c + 1) * chunk, :])
    y = x + gate_ref[0] * acc
    if final_norm:
        y = _rms(y, gf_ref[...])
    o_ref[0] = y


def _ffn(x, shift, scale, gate, g2, w1, w2, gf, final_norm):
    b, s, d = x.shape
    tm = 512
    row = pl.BlockSpec((1, tm, d), lambda i, j: (i, j, 0))
    vec = pl.BlockSpec((1, 1, d), lambda i, j: (i, 0, 0))
    return pl.pallas_call(
        functools.partial(_ffn_kernel, final_norm=final_norm),
        grid=(b, s // tm),
        in_specs=[row, vec, vec, vec, _const_spec((1, d)), _const_spec(w1.shape),
                  _const_spec(w2.shape), _const_spec((1, d))],
        out_specs=row,
        out_shape=jax.ShapeDtypeStruct((b, s, d), F32),
        compiler_params=pltpu.CompilerParams(
            dimension_semantics=("arbitrary", "arbitrary"), vmem_limit_bytes=VMEM_LIMIT),
        name="ffn",
    )(x, shift, scale, gate, g2, w1, w2, gf)


def _rope_tables(positions):
    pos = positions.astype(F32)[..., None]

    def angles(rot_dim):
        inv = ROPE_THETA ** (-jnp.arange(0, rot_dim, 2, dtype=F32) / rot_dim)
        ang = pos * inv
        return jnp.cos(ang), jnp.sin(ang)

    b, s = positions.shape
    ca, sa = angles(MLA_ROPE)
    one = lambda n: jnp.ones((b, s, n), F32)
    zero = lambda n: jnp.zeros((b, s, n), F32)
    cosa = jnp.concatenate([one(MLA_NOPE), ca, ca, one(LANE - MLA_QK)], axis=-1)
    sina = jnp.concatenate([zero(MLA_NOPE), -sa, sa, zero(LANE - MLA_QK)], axis=-1)
    cm, sm = angles(MOBA_ROT)
    cos_h = jnp.concatenate([cm, cm, one(HEAD - MOBA_ROT)], axis=-1)
    sin_h = jnp.concatenate([-sm, sm, zero(HEAD - MOBA_ROT)], axis=-1)
    cosm = jnp.concatenate([cos_h, cos_h], axis=-1)
    sinm = jnp.concatenate([sin_h, sin_h], axis=-1)
    return cosa, sina, cosm, sinm


def _layer_weights(w_in, w_uq, w_ukv):
    d = w_in.shape[0]
    o_kpe = MLA_Q_RANK + MLA_KV_RANK
    o_sb = o_kpe + MLA_ROPE
    kpe = jnp.concatenate([jnp.zeros((d, MLA_NOPE), F32), w_in[:, o_kpe:o_sb],
                           jnp.zeros((d, LANE - MLA_QK), F32)], axis=1)
    w_in_r = jnp.concatenate([w_in[:, :o_kpe], kpe, w_in[:, o_sb:]], axis=1).astype(BF16)
    wuq = jnp.pad(w_uq.reshape(MLA_Q_RANK, N_HEADS, MLA_QK),
                  ((0, 0), (0, 0), (0, LANE - MLA_QK))).reshape(MLA_Q_RANK, N_HEADS * LANE)
    ukv = w_ukv.reshape(MLA_KV_RANK, N_HEADS, MLA_NOPE + MLA_V)
    wk = jnp.pad(ukv[:, :, :MLA_NOPE], ((0, 0), (0, 0), (0, LANE - MLA_NOPE)))
    wukv = jnp.concatenate([wk.reshape(MLA_KV_RANK, N_HEADS * LANE),
                            ukv[:, :, MLA_NOPE:].reshape(MLA_KV_RANK, N_HEADS * MLA_V)], axis=1)
    return w_in_r, wuq.astype(BF16), wukv.astype(BF16)


def kernel(x, c, positions, w_ada, b_ada, norm1_g, norm2_g, w_in, q_norm_g, w_uq, kv_norm_g, w_ukv,
           w_o_mla, w_o_sb, w_o_moba, w_out, w_ff1, w_ff2, final_norm_g):
    b, s, d = x.shape
    depth = w_ada.shape[0]
    assert d == D_MODEL and s % 512 == 0 and w_in.shape[-1] == C_END - (LANE - MLA_ROPE)
    cosa, sina, cosm, sinm = _rope_tables(positions)
    mod = _adaln_mod(c, w_ada, b_ada)
    gf = final_norm_g.reshape(1, d)
    for l in range(depth):
        shift1, scale1, gate1, shift2, scale2, gate2 = [
            m.reshape(b, 1, d) for m in jnp.split(mod[l], 6, axis=-1)]
        w_in_r, wuq, wukv = _layer_weights(w_in[l], w_uq[l], w_ukv[l])
        mla, sb, mb, gates = _inproj(
            x, shift1, scale1, norm1_g[l].reshape(1, d), w_in_r, q_norm_g[l].reshape(1, -1), wuq,
            kv_norm_g[l].reshape(1, -1), wukv, cosa, sina, cosm, sinm)
        o_mla = _attention("mla", mla, N_HEADS * LANE, 4)
        o_sb = _attention("sb", sb, W_BRANCH, 2)
        o_mb = _attention("moba", mb, W_BRANCH, 2)
        x = _merge(x, gate1, o_mla, o_sb, o_mb, gates, w_o_mla[l].astype(BF16),
                   w_o_sb[l].astype(BF16), w_o_moba[l].astype(BF16), w_out[l].astype(BF16))
        x = _ffn(x, shift2, scale2, gate2, norm2_g[l].reshape(1, d), w_ff1[l].astype(BF16),
                 w_ff2[l].astype(BF16), gf, l == depth - 1)
    return x
```

```python
import functools
import math

import jax
import jax.numpy as jnp
from jax import lax
from jax.experimental import pallas as pl
from jax.experimental.pallas import tpu as pltpu

F32 = jnp.float32
BF16 = jnp.bfloat16

D_MODEL = 1024
N_HEADS = 8
MLA_NOPE = 64
MLA_ROPE = 32
MLA_V = 64
MLA_QK = MLA_NOPE + MLA_ROPE
MLA_Q_RANK = 768
MLA_KV_RANK = 256
HEAD = 64
W_BRANCH = N_HEADS * HEAD
MOBA_BLOCK = 256
MOBA_TOPK = 3
MOBA_ROT = HEAD // 4
ROPE_THETA = 500000.0
D_FF = 4 * D_MODEL
EPS = 1e-6
NEG_INF = -1e30
SB_DEAD = -104.0
LOG2E = 1.4426950408889634

LANE = 128
TQ = 256
VMEM_LIMIT = 56 * 1024 * 1024

C_QLAT = 0
C_CKV = C_QLAT + MLA_Q_RANK
C_KPE = C_CKV + MLA_KV_RANK
C_SB = C_KPE + LANE
C_MB = C_SB + 3 * W_BRANCH
C_GATE = C_MB + 3 * W_BRANCH
C_END = C_GATE + 3 * D_MODEL
W_MLA_OUT = 2 * N_HEADS * LANE + N_HEADS * MLA_V


def _rms(xf, g):
    return xf * lax.rsqrt(jnp.mean(xf * xf, axis=-1, keepdims=True) + EPS) * g


def _dot(a, b):
    return jnp.dot(a, b, preferred_element_type=F32)


def _dot_nt(a, b):
    return lax.dot_general(a, b, (((1,), (1,)), ((), ())), preferred_element_type=F32)


def _rope_lanes(xb, cos, sin, half, first_half):
    rot = jnp.where(first_half, pltpu.roll(xb, LANE - half, 1), pltpu.roll(xb, half, 1))
    return xb * cos + rot * sin


def _mod_kernel(c_ref, w_ref, b_ref, o_ref):
    c = c_ref[...]
    c_act = c * (1.0 / (1.0 + jnp.exp(-c)))
    o_ref[0] = _dot(c_act.astype(BF16), w_ref[0].astype(BF16)) + b_ref[0]


def _adaln_mod(c, w_ada, b_ada):
    depth, d, n = w_ada.shape
    b = c.shape[0]
    tn = 1536
    return pl.pallas_call(
        _mod_kernel,
        grid=(depth, n // tn),
        in_specs=[
            pl.BlockSpec((b, d), lambda l, j: (0, 0)),
            pl.BlockSpec((1, d, tn), lambda l, j: (l, 0, j)),
            pl.BlockSpec((1, 1, tn), lambda l, j: (l, 0, j)),
        ],
        out_specs=pl.BlockSpec((1, b, tn), lambda l, j: (l, 0, j)),
        out_shape=jax.ShapeDtypeStruct((depth, b, n), F32),
        compiler_params=pltpu.CompilerParams(
            dimension_semantics=("arbitrary", "arbitrary"), vmem_limit_bytes=VMEM_LIMIT),
        name="adaln_mod",
    )(c, w_ada, b_ada.reshape(depth, 1, n))


def _inproj_kernel(x_ref, shift_ref, scale_ref, g1_ref, w_ref, gq_ref, wuq_ref, gkv_ref, wukv_ref,
                   cosa_ref, sina_ref, cosm_ref, sinm_ref,
                   mla_ref, sb_ref, mb_ref, gate_ref):
    x = x_ref[0]
    h = _rms(x, g1_ref[...]) * (1.0 + scale_ref[0]) + shift_ref[0]
    h = h.astype(BF16)
    lane = lax.broadcasted_iota(jnp.int32, (1, LANE), 1)

    cosa, sina = cosa_ref[0], sina_ref[0]
    mla_first = lane < MLA_NOPE + MLA_ROPE // 2
    mla_scale = LOG2E / math.sqrt(MLA_QK)
    qlat = _dot(h, w_ref[:, C_QLAT:C_CKV])
    q = _dot(_rms(qlat, gq_ref[...]).astype(BF16), wuq_ref[...])
    for hd in range(N_HEADS):
        qb = _rope_lanes(q[:, hd * LANE:(hd + 1) * LANE], cosa, sina, MLA_ROPE // 2, mla_first)
        mla_ref[0, :, hd * LANE:(hd + 1) * LANE] = (qb * mla_scale).astype(BF16)

    kpe = _rope_lanes(_dot(h, w_ref[:, C_KPE:C_SB]), cosa, sina, MLA_ROPE // 2, mla_first)
    ckv = _dot(h, w_ref[:, C_CKV:C_KPE])
    kv = _dot(_rms(ckv, gkv_ref[...]).astype(BF16), wukv_ref[...])
    k_off = N_HEADS * LANE
    for hd in range(N_HEADS):
        kb = kv[:, hd * LANE:(hd + 1) * LANE] + kpe
        mla_ref[0, :, k_off + hd * LANE:k_off + (hd + 1) * LANE] = kb.astype(BF16)
    mla_ref[0, :, 2 * k_off:] = kv[:, k_off:].astype(BF16)

    sb_scale = 1.0 / math.sqrt(HEAD)
    sb_ref[0, :, :W_BRANCH] = (_dot(h, w_ref[:, C_SB:C_SB + W_BRANCH]) * sb_scale).astype(BF16)
    sb_ref[0, :, W_BRANCH:] = _dot(h, w_ref[:, C_SB + W_BRANCH:C_MB]).astype(BF16)

    cosm, sinm = cosm_ref[0], sinm_ref[0]
    mb_first = (lane % HEAD) < MOBA_ROT // 2
    for part, scale in ((0, sb_scale * LOG2E), (1, 1.0)):
        c0 = C_MB + part * W_BRANCH
        acc = _dot(h, w_ref[:, c0:c0 + W_BRANCH])
        for cb in range(W_BRANCH // LANE):
            blk = _rope_lanes(acc[:, cb * LANE:(cb + 1) * LANE], cosm, sinm, MOBA_ROT // 2, mb_first)
            mb_ref[0, :, part * W_BRANCH + cb * LANE:part * W_BRANCH + (cb + 1) * LANE] = (
                blk * scale).astype(BF16)
    mb_ref[0, :, 2 * W_BRANCH:] = _dot(h, w_ref[:, C_MB + 2 * W_BRANCH:C_GATE]).astype(BF16)

    for br in range(3):
        c0 = C_GATE + br * D_MODEL
        gl = _dot(h, w_ref[:, c0:c0 + D_MODEL])
        gate_ref[0, :, br * D_MODEL:(br + 1) * D_MODEL] = (1.0 / (1.0 + jnp.exp(-gl))).astype(BF16)


def _const_spec(shape):
    nd = len(shape)
    return pl.BlockSpec(shape, lambda *_: (0,) * nd, pipeline_mode=pl.Buffered(1))


def _inproj(x, shift, scale, g1, w_in_r, gq, wuq, gkv, wukv, cosa, sina, cosm, sinm):
    b, s, d = x.shape
    tm = TQ
    row = lambda w: pl.BlockSpec((1, tm, w), lambda i, j: (i, j, 0))
    vec = pl.BlockSpec((1, 1, d), lambda i, j: (i, 0, 0))
    return pl.pallas_call(
        _inproj_kernel,
        grid=(b, s // tm),
        in_specs=[row(d), vec, vec, _const_spec((1, d)), _const_spec(w_in_r.shape),
                  _const_spec(gq.shape), _const_spec(wuq.shape), _const_spec(gkv.shape),
                  _const_spec(wukv.shape), row(LANE), row(LANE), row(LANE), row(LANE)],
        out_specs=[row(W_MLA_OUT), row(3 * W_BRANCH), row(3 * W_BRANCH), row(3 * D_MODEL)],
        out_shape=[jax.ShapeDtypeStruct((b, s, W_MLA_OUT), BF16),
                   jax.ShapeDtypeStruct((b, s, 3 * W_BRANCH), BF16),
                   jax.ShapeDtypeStruct((b, s, 3 * W_BRANCH), BF16),
                   jax.ShapeDtypeStruct((b, s, 3 * D_MODEL), BF16)],
        compiler_params=pltpu.CompilerParams(
            dimension_semantics=("arbitrary", "arbitrary"), vmem_limit_bytes=VMEM_LIMIT),
        name="inproj",
    )(x, shift, scale, g1, w_in_r, gq, wuq, gkv, wukv, cosa, sina, cosm, sinm)


def _transpose_values(v_ref, vt_ref, nkv):
    for j in range(nkv):
        for c in range(W_BRANCH // LANE):
            blk = v_ref[0, j * TQ:(j + 1) * TQ, c * LANE:(c + 1) * LANE]
            vt_ref[j, c * LANE:(c + 1) * LANE, :] = blk.astype(F32).T.astype(BF16)


def _store_heads(o_ref, acc_ref, l_ref):
    for c in range(W_BRANCH // LANE):
        parts = []
        for h in (2 * c, 2 * c + 1):
            part = acc_ref[h * HEAD:(h + 1) * HEAD, :]
            if l_ref is not None:
                part = part / l_ref[h:h + 1, :]
            parts.append(part)
        o_ref[0, :, c * LANE:(c + 1) * LANE] = jnp.concatenate(parts, axis=0).T.astype(BF16)


def _stack_pair_queries(q_ref, qbd_ref):
    pairs, _, w = qbd_ref.shape
    lane = lax.broadcasted_iota(jnp.int32, (1, w), 1)
    for p in range(pairs):
        q = q_ref[0, :, p * w:(p + 1) * w]
        qbd_ref[p, 0:TQ, :] = jnp.where(lane < w // 2, q, jnp.zeros_like(q))
        qbd_ref[p, TQ:2 * TQ, :] = jnp.where(lane >= w // 2, q, jnp.zeros_like(q))


def _pair_scores(k_ref, qbd_ref, row0, p):
    w = qbd_ref.shape[2]
    return _dot_nt(k_ref[0, pl.ds(row0, TQ), p * w:(p + 1) * w], qbd_ref[p])


def _pair_values(vt_pair, weights):
    out = _dot(vt_pair, jnp.concatenate(weights, axis=1))
    return [out[a * HEAD:(a + 1) * HEAD, a * TQ:(a + 1) * TQ] for a in range(2)]


def _causal(strict):
    key = lax.broadcasted_iota(jnp.int32, (TQ, TQ), 0)
    qry = lax.broadcasted_iota(jnp.int32, (TQ, TQ), 1)
    return key < qry if strict else key <= qry


def _softmax_pair(p, scores, vt_pair, m_ref, l_ref, acc_ref, first):
    probs, alphas = [], []
    for a, s in enumerate(scores):
        h = 2 * p + a
        m_new = jnp.max(s, axis=0, keepdims=True)
        if not first:
            m_old = m_ref[h:h + 1, :]
            m_new = jnp.maximum(m_old, m_new)
            alphas.append(jnp.exp2(m_old - m_new))
        pr = jnp.exp2(s - m_new)
        l_blk = jnp.sum(pr, axis=0, keepdims=True)
        m_ref[h:h + 1, :] = m_new
        l_ref[h:h + 1, :] = l_blk if first else alphas[a] * l_ref[h:h + 1, :] + l_blk
        probs.append(pr.astype(BF16))
    for a, o in enumerate(_pair_values(vt_pair, probs)):
        rows = slice((2 * p + a) * HEAD, (2 * p + a + 1) * HEAD)
        acc_ref[rows, :] = o if first else alphas[a] * acc_ref[rows, :] + o


def _mla_kernel(q_ref, k_ref, v_ref, o_ref, vt_ref, qbd_ref, m_ref, l_ref, acc_ref, *, nkv):
    qi = pl.program_id(1)

    @pl.when(qi == 0)
    def _():
        _transpose_values(v_ref, vt_ref, nkv)

    _stack_pair_queries(q_ref, qbd_ref)

    def block(j, first):
        row0 = pl.multiple_of(j * TQ, TQ)
        scores = [_pair_scores(k_ref, qbd_ref, row0, p) for p in range(N_HEADS // 2)]
        for p in range(N_HEADS // 2):
            halves = [scores[p][:, a * TQ:(a + 1) * TQ] for a in range(2)]
            if first:
                halves = [jnp.where(_causal(False), s, NEG_INF) for s in halves]
            _softmax_pair(p, halves, vt_ref[j, p * LANE:(p + 1) * LANE, :], m_ref, l_ref, acc_ref,
                          first)

    def body(j, carry):
        block(j, False)
        return carry

    block(qi, True)
    lax.fori_loop(0, qi, body, 0)
    _store_heads(o_ref, acc_ref, l_ref)


def _sb_kernel(q_ref, k_ref, v_ref, o_ref, vt_ref, qbd_ref, run_ref, acc_ref, *, nkv):
    qi = pl.program_id(1)

    @pl.when(qi == 0)
    def _():
        _transpose_values(v_ref, vt_ref, nkv)

    _stack_pair_queries(q_ref, qbd_ref)

    def block(j, first):
        row0 = pl.multiple_of(j * TQ, TQ)
        suffix = jnp.where(_causal(False), 1.0, 0.0).astype(BF16)
        zs = [_pair_scores(k_ref, qbd_ref, row0, p) for p in range(N_HEADS // 2)]
        for p in range(N_HEADS // 2):
            weights = []
            for a in range(2):
                h = 2 * p + a
                z = zs[p][:, a * TQ:(a + 1) * TQ]
                log_beta = jnp.minimum(z, 0.0) - jnp.log(1.0 + jnp.exp(-jnp.abs(z)))
                log_keep = log_beta - z
                if first:
                    log_keep = jnp.where(_causal(True), log_keep, 0.0)
                hi = log_keep.astype(BF16)
                lo = (log_keep - hi.astype(F32)).astype(BF16)
                both = _dot(suffix, jnp.concatenate([hi, lo], axis=1))
                incl = both[:, :TQ] + both[:, TQ:]
                if first:
                    w = jnp.where(_causal(True), jnp.exp(log_beta + (incl - log_keep)), 0.0)
                    run_ref[h:h + 1, :] = incl[0:1, :]
                else:
                    run = run_ref[h:h + 1, :]
                    w = jnp.exp(log_beta + (incl - log_keep + run))
                    run_ref[h:h + 1, :] = run + incl[0:1, :]
                weights.append(w.astype(BF16))
            for a, o in enumerate(_pair_values(vt_ref[j, p * LANE:(p + 1) * LANE, :], weights)):
                rows = slice((2 * p + a) * HEAD, (2 * p + a + 1) * HEAD)
                acc_ref[rows, :] = o if first else acc_ref[rows, :] + o

    def live(carry):
        i, top = carry
        return (i < qi) & (top > SB_DEAD)

    def body(carry):
        i, _ = carry
        block(qi - 1 - i, False)
        return i + 1, jnp.max(run_ref[...])

    block(qi, True)
    lax.while_loop(live, body, (jnp.int32(0), jnp.max(run_ref[...])))
    _store_heads(o_ref, acc_ref, None)


def _moba_kernel(q_ref, k_ref, v_ref, o_ref, vt_ref, qbd_ref, m_ref, l_ref, acc_ref, km_ref,
                 sel_ref, *, nkv, nbp):
    qi = pl.program_id(1)
    lane = lax.broadcasted_iota(jnp.int32, (1, W_BRANCH), 1)

    @pl.when(qi == 0)
    def _():
        _transpose_values(v_ref, vt_ref, nkv)
        km_ref[...] = jnp.zeros_like(km_ref)
        for j in range(nkv):
            mean = jnp.mean(k_ref[0, j * TQ:(j + 1) * TQ, :].astype(F32), axis=0, keepdims=True)
            for h in range(N_HEADS):
                in_head = (lane >= h * HEAD) & (lane < (h + 1) * HEAD)
                km_ref[h * nbp + j:h * nbp + j + 1, :] = jnp.where(in_head, mean, 0.0)

    _stack_pair_queries(q_ref, qbd_ref)

    gates = lax.dot_general(km_ref[...], q_ref[0].astype(F32), (((1,), (1,)), ((), ())),
                            precision=lax.Precision.HIGHEST, preferred_element_type=F32)
    blk = lax.broadcasted_iota(jnp.int32, (nbp, TQ), 0)
    for h in range(N_HEADS):
        gate = jnp.where(blk < qi, gates[h * nbp:(h + 1) * nbp, :], NEG_INF)
        rank = jnp.zeros((nbp, TQ), jnp.int32)
        for j in range(nkv):
            gj = gate[j:j + 1, :]
            ahead = (gj > gate) | ((gj == gate) & (blk > j))
            rank = rank + jnp.where(ahead, 1, 0)
        sel = (rank < MOBA_TOPK) & (rank < qi)
        sel_ref[h * nbp:(h + 1) * nbp, :] = jnp.where(sel, 1.0, 0.0)

    def block(j, first):
        row0 = pl.multiple_of(j * TQ, TQ)
        scores = [_pair_scores(k_ref, qbd_ref, row0, p) for p in range(N_HEADS // 2)]
        for p in range(N_HEADS // 2):
            halves = []
            for a in range(2):
                s = scores[p][:, a * TQ:(a + 1) * TQ]
                if first:
                    halves.append(jnp.where(_causal(False), s, NEG_INF))
                else:
                    chosen = sel_ref[pl.ds((2 * p + a) * nbp + j, 1), :] > 0.5
                    halves.append(jnp.where(chosen, s, NEG_INF))
            _softmax_pair(p, halves, vt_ref[j, p * LANE:(p + 1) * LANE, :], m_ref, l_ref, acc_ref,
                          first)

    def body(j, carry):
        block(j, False)
        return carry

    block(qi, True)
    lax.fori_loop(0, qi, body, 0)
    _store_heads(o_ref, acc_ref, l_ref)


def _attention(kind, src, qk_width, v_col):
    b, s, _ = src.shape
    nkv = s // TQ
    vt = pltpu.VMEM((nkv, W_BRANCH, TQ), BF16)
    stat = pltpu.VMEM((N_HEADS, TQ), F32)
    acc = pltpu.VMEM((W_BRANCH, TQ), F32)
    qbd = pltpu.VMEM((N_HEADS // 2, 2 * TQ, 2 * qk_width // N_HEADS), BF16)
    if kind == "mla":
        body = functools.partial(_mla_kernel, nkv=nkv)
        scratch = [vt, qbd, stat, stat, acc]
    elif kind == "sb":
        body = functools.partial(_sb_kernel, nkv=nkv)
        scratch = [vt, qbd, stat, acc]
    else:
        nbp = -(-nkv // 8) * 8
        body = functools.partial(_moba_kernel, nkv=nkv, nbp=nbp)
        scratch = [vt, qbd, stat, stat, acc, pltpu.VMEM((N_HEADS * nbp, W_BRANCH), F32),
                   pltpu.VMEM((N_HEADS * nbp, TQ), F32)]
    return pl.pallas_call(
        body,
        grid=(b, nkv),
        in_specs=[
            pl.BlockSpec((1, TQ, qk_width), lambda i, t: (i, t, 0)),
            pl.BlockSpec((1, s, qk_width), lambda i, t: (i, 0, 1)),
            pl.BlockSpec((1, s, W_BRANCH), lambda i, t: (i, 0, v_col)),
        ],
        out_specs=pl.BlockSpec((1, TQ, W_BRANCH), lambda i, t: (i, t, 0)),
        out_shape=jax.ShapeDtypeStruct((b, s, W_BRANCH), BF16),
        scratch_shapes=scratch,
        compiler_params=pltpu.CompilerParams(
            dimension_semantics=("arbitrary", "arbitrary"), vmem_limit_bytes=VMEM_LIMIT),
        name="attn_" + kind,
    )(src, src, src)


def _merge_kernel(x_ref, gate1_ref, oa_ref, ob_ref, oc_ref, g_ref, wa_ref, wb_ref, wc_ref, wo_ref,
                  o_ref):
    merged = g_ref[0, :, :D_MODEL].astype(F32) * _dot(oa_ref[0], wa_ref[...])
    merged = merged + g_ref[0, :, D_MODEL:2 * D_MODEL].astype(F32) * _dot(ob_ref[0], wb_ref[...])
    merged = merged + g_ref[0, :, 2 * D_MODEL:].astype(F32) * _dot(oc_ref[0], wc_ref[...])
    o_ref[0] = x_ref[0] + gate1_ref[0] * _dot(merged.astype(BF16), wo_ref[...])


def _merge(x, gate1, oa, ob, oc, gates, wa, wb, wc, wo):
    b, s, d = x.shape
    tm = 512
    row = lambda w: pl.BlockSpec((1, tm, w), lambda i, j: (i, j, 0))
    vec = pl.BlockSpec((1, 1, d), lambda i, j: (i, 0, 0))
    return pl.pallas_call(
        _merge_kernel,
        grid=(b, s // tm),
        in_specs=[row(d), vec, row(W_BRANCH), row(W_BRANCH), row(W_BRANCH), row(3 * d),
                  _const_spec(wa.shape), _const_spec(wb.shape), _const_spec(wc.shape),
                  _const_spec(wo.shape)],
        out_specs=row(d),
        out_shape=jax.ShapeDtypeStruct((b, s, d), F32),
        compiler_params=pltpu.CompilerParams(
            dimension_semantics=("arbitrary", "arbitrary"), vmem_limit_bytes=VMEM_LIMIT),
        name="merge",
    )(x, gate1, oa, ob, oc, gates, wa, wb, wc, wo)


def _ffn_kernel(x_ref, shift_ref, scale_ref, gate_ref, g2_ref, w1_ref, w2_ref, gf_ref, o_ref, *,
                final_norm):
    x = x_ref[0]
    h = (_rms(x, g2_ref[...]) * (1.0 + scale_ref[0]) + shift_ref[0]).astype(BF16)
    acc = jnp.zeros_like(x)
    chunk = D_MODEL
    for c in range(D_FF // chunk):
        u = jnp.maximum(_dot(h, w1_ref[:, c * chunk:(c + 1) * chunk]), 0.0)
        acc = acc + _dot((u * u).astype(BF16), w2_ref[c * chunk:(c + 1) * chunk, :])
    y = x + gate_ref[0] * acc
    if final_norm:
        y = _rms(y, gf_ref[...])
    o_ref[0] = y


def _ffn(x, shift, scale, gate, g2, w1, w2, gf, final_norm):
    b, s, d = x.shape
    tm = 512
    row = pl.BlockSpec((1, tm, d), lambda i, j: (i, j, 0))
    vec = pl.BlockSpec((1, 1, d), lambda i, j: (i, 0, 0))
    return pl.pallas_call(
        functools.partial(_ffn_kernel, final_norm=final_norm),
        grid=(b, s // tm),
        in_specs=[row, vec, vec, vec, _const_spec((1, d)), _const_spec(w1.shape),
                  _const_spec(w2.shape), _const_spec((1, d))],
        out_specs=row,
        out_shape=jax.ShapeDtypeStruct((b, s, d), F32),
        compiler_params=pltpu.CompilerParams(
            dimension_semantics=("arbitrary", "arbitrary"), vmem_limit_bytes=VMEM_LIMIT),
        name="ffn",
    )(x, shift, scale, gate, g2, w1, w2, gf)


def _rope_tables(positions):
    pos = positions.astype(F32)[..., None]

    def angles(rot_dim):
        inv = ROPE_THETA ** (-jnp.arange(0, rot_dim, 2, dtype=F32) / rot_dim)
        ang = pos * inv
        return jnp.cos(ang), jnp.sin(ang)

    b, s = positions.shape
    ca, sa = angles(MLA_ROPE)
    one = lambda n: jnp.ones((b, s, n), F32)
    zero = lambda n: jnp.zeros((b, s, n), F32)
    cosa = jnp.concatenate([one(MLA_NOPE), ca, ca, one(LANE - MLA_QK)], axis=-1)
    sina = jnp.concatenate([zero(MLA_NOPE), -sa, sa, zero(LANE - MLA_QK)], axis=-1)
    cm, sm = angles(MOBA_ROT)
    cos_h = jnp.concatenate([cm, cm, one(HEAD - MOBA_ROT)], axis=-1)
    sin_h = jnp.concatenate([-sm, sm, zero(HEAD - MOBA_ROT)], axis=-1)
    cosm = jnp.concatenate([cos_h, cos_h], axis=-1)
    sinm = jnp.concatenate([sin_h, sin_h], axis=-1)
    return cosa, sina, cosm, sinm


def _layer_weights(w_in, w_uq, w_ukv):
    d = w_in.shape[0]
    o_kpe = MLA_Q_RANK + MLA_KV_RANK
    o_sb = o_kpe + MLA_ROPE
    kpe = jnp.concatenate([jnp.zeros((d, MLA_NOPE), F32), w_in[:, o_kpe:o_sb],
                           jnp.zeros((d, LANE - MLA_QK), F32)], axis=1)
    w_in_r = jnp.concatenate([w_in[:, :o_kpe], kpe, w_in[:, o_sb:]], axis=1).astype(BF16)
    wuq = jnp.pad(w_uq.reshape(MLA_Q_RANK, N_HEADS, MLA_QK),
                  ((0, 0), (0, 0), (0, LANE - MLA_QK))).reshape(MLA_Q_RANK, N_HEADS * LANE)
    ukv = w_ukv.reshape(MLA_KV_RANK, N_HEADS, MLA_NOPE + MLA_V)
    wk = jnp.pad(ukv[:, :, :MLA_NOPE], ((0, 0), (0, 0), (0, LANE - MLA_NOPE)))
    wukv = jnp.concatenate([wk.reshape(MLA_KV_RANK, N_HEADS * LANE),
                            ukv[:, :, MLA_NOPE:].reshape(MLA_KV_RANK, N_HEADS * MLA_V)], axis=1)
    return w_in_r, wuq.astype(BF16), wukv.astype(BF16)


def kernel(x, c, positions, w_ada, b_ada, norm1_g, norm2_g, w_in, q_norm_g, w_uq, kv_norm_g, w_ukv,
           w_o_mla, w_o_sb, w_o_moba, w_out, w_ff1, w_ff2, final_norm_g):
    b, s, d = x.shape
    depth = w_ada.shape[0]
    assert d == D_MODEL and s % 512 == 0 and w_in.shape[-1] == C_END - (LANE - MLA_ROPE)
    cosa, sina, cosm, sinm = _rope_tables(positions)
    mod = _adaln_mod(c, w_ada, b_ada)
    gf = final_norm_g.reshape(1, d)
    for l in range(depth):
        shift1, scale1, gate1, shift2, scale2, gate2 = [
            m.reshape(b, 1, d) for m in jnp.split(mod[l], 6, axis=-1)]
        w_in_r, wuq, wukv = _layer_weights(w_in[l], w_uq[l], w_ukv[l])
        mla, sb, mb, gates = _inproj(
            x, shift1, scale1, norm1_g[l].reshape(1, d), w_in_r, q_norm_g[l].reshape(1, -1), wuq,
            kv_norm_g[l].reshape(1, -1), wukv, cosa, sina, cosm, sinm)
        o_mla = _attention("mla", mla, N_HEADS * LANE, 4)
        o_sb = _attention("sb", sb, W_BRANCH, 2)
        o_mb = _attention("moba", mb, W_BRANCH, 2)
        x = _merge(x, gate1, o_mla, o_sb, o_mb, gates, w_o_mla[l].astype(BF16),
                   w_o_sb[l].astype(BF16), w_o_moba[l].astype(BF16), w_out[l].astype(BF16))
        x = _ffn(x, shift2, scale2, gate2, norm2_g[l].reshape(1, d), w_ff1[l].astype(BF16),
                 w_ff2[l].astype(BF16), gf, l == depth - 1)
    return x
```

```python
import functools
import math

import jax
import jax.numpy as jnp
from jax import lax
from jax.experimental import pallas as pl
from jax.experimental.pallas import tpu as pltpu

F32 = jnp.float32
BF16 = jnp.bfloat16

D_MODEL = 1024
N_HEADS = 8
MLA_NOPE = 64
MLA_ROPE = 32
MLA_V = 64
MLA_QK = MLA_NOPE + MLA_ROPE
MLA_Q_RANK = 768
MLA_KV_RANK = 256
HEAD = 64
W_BRANCH = N_HEADS * HEAD
MOBA_BLOCK = 256
MOBA_TOPK = 3
MOBA_ROT = HEAD // 4
ROPE_THETA = 500000.0
D_FF = 4 * D_MODEL
EPS = 1e-6
NEG_INF = -1e30
SB_DEAD = -104.0
LOG2E = 1.4426950408889634

LANE = 128
TQ = 256
VMEM_LIMIT = 56 * 1024 * 1024

C_QLAT = 0
C_CKV = C_QLAT + MLA_Q_RANK
C_KPE = C_CKV + MLA_KV_RANK
C_SB = C_KPE + LANE
C_MB = C_SB + 3 * W_BRANCH
C_GATE = C_MB + 3 * W_BRANCH
C_END = C_GATE + 3 * D_MODEL
W_MLA_OUT = 2 * N_HEADS * LANE + N_HEADS * MLA_V


def _rms(xf, g):
    return xf * lax.rsqrt(jnp.mean(xf * xf, axis=-1, keepdims=True) + EPS) * g


def _dot(a, b):
    return jnp.dot(a, b, preferred_element_type=F32)


def _dot_nt(a, b):
    return lax.dot_general(a, b, (((1,), (1,)), ((), ())), preferred_element_type=F32)


def _rope_lanes(xb, cos, sin, half, first_half):
    rot = jnp.where(first_half, pltpu.roll(xb, LANE - half, 1), pltpu.roll(xb, half, 1))
    return xb * cos + rot * sin


def _mod_kernel(c_ref, w_ref, b_ref, o_ref):
    c = c_ref[...]
    c_act = c * (1.0 / (1.0 + jnp.exp(-c)))
    o_ref[0] = _dot(c_act.astype(BF16), w_ref[0].astype(BF16)) + b_ref[0]


def _adaln_mod(c, w_ada, b_ada):
    depth, d, n = w_ada.shape
    b = c.shape[0]
    tn = 1536
    return pl.pallas_call(
        _mod_kernel,
        grid=(depth, n // tn),
        in_specs=[
            pl.BlockSpec((b, d), lambda l, j: (0, 0)),
            pl.BlockSpec((1, d, tn), lambda l, j: (l, 0, j)),
            pl.BlockSpec((1, 1, tn), lambda l, j: (l, 0, j)),
        ],
        out_specs=pl.BlockSpec((1, b, tn), lambda l, j: (l, 0, j)),
        out_shape=jax.ShapeDtypeStruct((depth, b, n), F32),
        compiler_params=pltpu.CompilerParams(
            dimension_semantics=("arbitrary", "arbitrary"), vmem_limit_bytes=VMEM_LIMIT),
        name="adaln_mod",
    )(c, w_ada, b_ada.reshape(depth, 1, n))


def _inproj_kernel(x_ref, shift_ref, scale_ref, g1_ref, w_ref, gq_ref, wuq_ref, gkv_ref, wukv_ref,
                   cosa_ref, sina_ref, cosm_ref, sinm_ref,
                   mla_ref, sb_ref, mb_ref, gate_ref):
    x = x_ref[0]
    h = _rms(x, g1_ref[...]) * (1.0 + scale_ref[0]) + shift_ref[0]
    h = h.astype(BF16)
    lane = lax.broadcasted_iota(jnp.int32, (1, LANE), 1)

    cosa, sina = cosa_ref[0], sina_ref[0]
    mla_first = lane < MLA_NOPE + MLA_ROPE // 2
    mla_scale = LOG2E / math.sqrt(MLA_QK)
    qlat = _dot(h, w_ref[:, C_QLAT:C_CKV])
    q = _dot(_rms(qlat, gq_ref[...]).astype(BF16), wuq_ref[...])
    for hd in range(N_HEADS):
        qb = _rope_lanes(q[:, hd * LANE:(hd + 1) * LANE], cosa, sina, MLA_ROPE // 2, mla_first)
        mla_ref[0, :, hd * LANE:(hd + 1) * LANE] = (qb * mla_scale).astype(BF16)

    kpe = _rope_lanes(_dot(h, w_ref[:, C_KPE:C_SB]), cosa, sina, MLA_ROPE // 2, mla_first)
    ckv = _dot(h, w_ref[:, C_CKV:C_KPE])
    kv = _dot(_rms(ckv, gkv_ref[...]).astype(BF16), wukv_ref[...])
    k_off = N_HEADS * LANE
    for hd in range(N_HEADS):
        kb = kv[:, hd * LANE:(hd + 1) * LANE] + kpe
        mla_ref[0, :, k_off + hd * LANE:k_off + (hd + 1) * LANE] = kb.astype(BF16)
    mla_ref[0, :, 2 * k_off:] = kv[:, k_off:].astype(BF16)

    sb_scale = 1.0 / math.sqrt(HEAD)
    sb_ref[0, :, :W_BRANCH] = (_dot(h, w_ref[:, C_SB:C_SB + W_BRANCH]) * sb_scale).astype(BF16)
    sb_ref[0, :, W_BRANCH:] = _dot(h, w_ref[:, C_SB + W_BRANCH:C_MB]).astype(BF16)

    cosm, sinm = cosm_ref[0], sinm_ref[0]
    mb_first = (lane % HEAD) < MOBA_ROT // 2
    for part, scale in ((0, sb_scale * LOG2E), (1, 1.0)):
        c0 = C_MB + part * W_BRANCH
        acc = _dot(h, w_ref[:, c0:c0 + W_BRANCH])
        for cb in range(W_BRANCH // LANE):
            blk = _rope_lanes(acc[:, cb * LANE:(cb + 1) * LANE], cosm, sinm, MOBA_ROT // 2, mb_first)
            mb_ref[0, :, part * W_BRANCH + cb * LANE:part * W_BRANCH + (cb + 1) * LANE] = (
                blk * scale).astype(BF16)
    mb_ref[0, :, 2 * W_BRANCH:] = _dot(h, w_ref[:, C_MB + 2 * W_BRANCH:C_GATE]).astype(BF16)

    for br in range(3):
        c0 = C_GATE + br * D_MODEL
        gl = _dot(h, w_ref[:, c0:c0 + D_MODEL])
        gate_ref[0, :, br * D_MODEL:(br + 1) * D_MODEL] = (1.0 / (1.0 + jnp.exp(-gl))).astype(BF16)


def _const_spec(shape):
    nd = len(shape)
    return pl.BlockSpec(shape, lambda *_: (0,) * nd, pipeline_mode=pl.Buffered(1))


def _inproj(x, shift, scale, g1, w_in_r, gq, wuq, gkv, wukv, cosa, sina, cosm, sinm):
    b, s, d = x.shape
    tm = TQ
    row = lambda w: pl.BlockSpec((1, tm, w), lambda i, j: (i, j, 0))
    vec = pl.BlockSpec((1, 1, d), lambda i, j: (i, 0, 0))
    return pl.pallas_call(
        _inproj_kernel,
        grid=(b, s // tm),
        in_specs=[row(d), vec, vec, _const_spec((1, d)), _const_spec(w_in_r.shape),
                  _const_spec(gq.shape), _const_spec(wuq.shape), _const_spec(gkv.shape),
                  _const_spec(wukv.shape), row(LANE), row(LANE), row(LANE), row(LANE)],
        out_specs=[row(W_MLA_OUT), row(3 * W_BRANCH), row(3 * W_BRANCH), row(3 * D_MODEL)],
        out_shape=[jax.ShapeDtypeStruct((b, s, W_MLA_OUT), BF16),
                   jax.ShapeDtypeStruct((b, s, 3 * W_BRANCH), BF16),
                   jax.ShapeDtypeStruct((b, s, 3 * W_BRANCH), BF16),
                   jax.ShapeDtypeStruct((b, s, 3 * D_MODEL), BF16)],
        compiler_params=pltpu.CompilerParams(
            dimension_semantics=("arbitrary", "arbitrary"), vmem_limit_bytes=VMEM_LIMIT),
        name="inproj",
    )(x, shift, scale, g1, w_in_r, gq, wuq, gkv, wukv, cosa, sina, cosm, sinm)


def _transpose_values(v_ref, vt_ref, nkv):
    for j in range(nkv):
        for c in range(W_BRANCH // LANE):
            blk = v_ref[0, j * TQ:(j + 1) * TQ, c * LANE:(c + 1) * LANE]
            vt_ref[j, c * LANE:(c + 1) * LANE, :] = blk.astype(F32).T.astype(BF16)


def _store_heads(o_ref, acc_ref, l_ref):
    for c in range(W_BRANCH // LANE):
        parts = []
        for h in (2 * c, 2 * c + 1):
            part = acc_ref[h * HEAD:(h + 1) * HEAD, :]
            if l_ref is not None:
                part = part / l_ref[h:h + 1, :]
            parts.append(part)
        o_ref[0, :, c * LANE:(c + 1) * LANE] = jnp.concatenate(parts, axis=0).T.astype(BF16)


def _stack_pair_queries(q_ref, qbd_ref):
    pairs, _, w = qbd_ref.shape
    lane = lax.broadcasted_iota(jnp.int32, (1, w), 1)
    for p in range(pairs):
        q = q_ref[0, :, p * w:(p + 1) * w]
        qbd_ref[p, 0:TQ, :] = jnp.where(lane < w // 2, q, jnp.zeros_like(q))
        qbd_ref[p, TQ:2 * TQ, :] = jnp.where(lane >= w // 2, q, jnp.zeros_like(q))


def _pair_scores(k_ref, qbd_ref, row0, p):
    w = qbd_ref.shape[2]
    return _dot_nt(k_ref[0, pl.ds(row0, TQ), p * w:(p + 1) * w], qbd_ref[p])


def _pair_values(vt_pair, weights):
    return [_dot(vt_pair[a * HEAD:(a + 1) * HEAD, :], weights[a]) for a in range(2)]


def _causal(strict):
    key = lax.broadcasted_iota(jnp.int32, (TQ, TQ), 0)
    qry = lax.broadcasted_iota(jnp.int32, (TQ, TQ), 1)
    return key < qry if strict else key <= qry


def _stage_scores(k_ref, qbd_ref, s_ref, mx_ref, slot, j, mask):
    row0 = pl.multiple_of(j * TQ, TQ)
    for p in range(N_HEADS // 2):
        pair = _pair_scores(k_ref, qbd_ref, row0, p)
        for a in range(2):
            h = 2 * p + a
            s = mask(h, pair[:, a * TQ:(a + 1) * TQ])
            s_ref[slot, p, :, a * TQ:(a + 1) * TQ] = s
            mx_ref[slot, h:h + 1, :] = jnp.max(s, axis=0, keepdims=True)


def _softmax_update(s_ref, mx_ref, slot, vt_blk, m_ref, l_ref, acc_ref, first):
    for p in range(N_HEADS // 2):
        probs, alphas = [], []
        for a in range(2):
            h = 2 * p + a
            m_new = mx_ref[slot, h:h + 1, :]
            if not first:
                m_old = m_ref[h:h + 1, :]
                m_new = jnp.maximum(m_old, m_new)
                alphas.append(jnp.exp2(m_old - m_new))
            pr = jnp.exp2(s_ref[slot, p, :, a * TQ:(a + 1) * TQ] - m_new)
            l_blk = jnp.sum(pr, axis=0, keepdims=True)
            m_ref[h:h + 1, :] = m_new
            l_ref[h:h + 1, :] = l_blk if first else alphas[a] * l_ref[h:h + 1, :] + l_blk
            probs.append(pr.astype(BF16))
        for a, o in enumerate(_pair_values(vt_blk[p * LANE:(p + 1) * LANE, :], probs)):
            rows = slice((2 * p + a) * HEAD, (2 * p + a + 1) * HEAD)
            acc_ref[rows, :] = o if first else alphas[a] * acc_ref[rows, :] + o


def _pipelined_sweep(qi, nkv, stage, update):
    stage(0, qi, True)
    stage(1, 0, False)
    update(0, qi, True)

    def two_blocks(i, carry):
        j = 2 * i
        stage(0, j + 1, False)
        update(1, j, False)
        stage(1, jnp.minimum(j + 2, nkv - 1), False)
        update(0, j + 1, False)
        return carry

    lax.fori_loop(0, lax.shift_right_logical(qi, 1), two_blocks, 0)

    @pl.when((qi & 1) == 1)
    def _():
        update(1, qi - 1, False)


def _mla_kernel(q_ref, k_ref, v_ref, o_ref, vt_ref, qbd_ref, s_ref, mx_ref, m_ref, l_ref, acc_ref,
                *, nkv):
    qi = pl.program_id(1)

    @pl.when(qi == 0)
    def _():
        _transpose_values(v_ref, vt_ref, nkv)

    _stack_pair_queries(q_ref, qbd_ref)

    def stage(slot, j, diag):
        mask = (lambda h, s: jnp.where(_causal(False), s, NEG_INF)) if diag else (lambda h, s: s)
        _stage_scores(k_ref, qbd_ref, s_ref, mx_ref, slot, j, mask)

    def update(slot, j, first):
        _softmax_update(s_ref, mx_ref, slot, vt_ref.at[j], m_ref, l_ref, acc_ref, first)

    _pipelined_sweep(qi, nkv, stage, update)
    _store_heads(o_ref, acc_ref, l_ref)


def _sb_kernel(q_ref, k_ref, v_ref, o_ref, vt_ref, qbd_ref, run_ref, acc_ref, *, nkv):
    qi = pl.program_id(1)

    @pl.when(qi == 0)
    def _():
        _transpose_values(v_ref, vt_ref, nkv)

    _stack_pair_queries(q_ref, qbd_ref)

    def block(j, first):
        row0 = pl.multiple_of(j * TQ, TQ)
        suffix = jnp.where(_causal(False), 1.0, 0.0).astype(BF16)
        zs = [_pair_scores(k_ref, qbd_ref, row0, p) for p in range(N_HEADS // 2)]
        for p in range(N_HEADS // 2):
            weights = []
            for a in range(2):
                h = 2 * p + a
                z = zs[p][:, a * TQ:(a + 1) * TQ]
                log_beta = jnp.minimum(z, 0.0) - jnp.log(1.0 + jnp.exp(-jnp.abs(z)))
                log_keep = log_beta - z
                if first:
                    log_keep = jnp.where(_causal(True), log_keep, 0.0)
                hi = log_keep.astype(BF16)
                lo = (log_keep - hi.astype(F32)).astype(BF16)
                both = _dot(suffix, jnp.concatenate([hi, lo], axis=1))
                incl = both[:, :TQ] + both[:, TQ:]
                if first:
                    w = jnp.where(_causal(True), jnp.exp(log_beta + (incl - log_keep)), 0.0)
                    run_ref[h:h + 1, :] = incl[0:1, :]
                else:
                    run = run_ref[h:h + 1, :]
                    w = jnp.exp(log_beta + (incl - log_keep + run))
                    run_ref[h:h + 1, :] = run + incl[0:1, :]
                weights.append(w.astype(BF16))
            for a, o in enumerate(_pair_values(vt_ref[j, p * LANE:(p + 1) * LANE, :], weights)):
                rows = slice((2 * p + a) * HEAD, (2 * p + a + 1) * HEAD)
                acc_ref[rows, :] = o if first else acc_ref[rows, :] + o

    def live(carry):
        i, top = carry
        return (i < qi) & (top > SB_DEAD)

    def body(carry):
        i, _ = carry
        block(qi - 1 - i, False)
        return i + 1, jnp.max(run_ref[...])

    block(qi, True)
    lax.while_loop(live, body, (jnp.int32(0), jnp.max(run_ref[...])))
    _store_heads(o_ref, acc_ref, None)


def _moba_kernel(q_ref, k_ref, v_ref, o_ref, vt_ref, qbd_ref, s_ref, mx_ref, m_ref, l_ref, acc_ref,
                 km_ref, sel_ref, *, nkv, nbp):
    qi = pl.program_id(1)
    lane = lax.broadcasted_iota(jnp.int32, (1, W_BRANCH), 1)

    @pl.when(qi == 0)
    def _():
        _transpose_values(v_ref, vt_ref, nkv)
        km_ref[...] = jnp.zeros_like(km_ref)
        for j in range(nkv):
            mean = jnp.mean(k_ref[0, j * TQ:(j + 1) * TQ, :].astype(F32), axis=0, keepdims=True)
            for h in range(N_HEADS):
                in_head = (lane >= h * HEAD) & (lane < (h + 1) * HEAD)
                km_ref[h * nbp + j:h * nbp + j + 1, :] = jnp.where(in_head, mean, 0.0)

    _stack_pair_queries(q_ref, qbd_ref)

    gates = lax.dot_general(km_ref[...], q_ref[0].astype(F32), (((1,), (1,)), ((), ())),
                            precision=lax.Precision.HIGHEST, preferred_element_type=F32)
    blk = lax.broadcasted_iota(jnp.int32, (nbp, TQ), 0)
    for h in range(N_HEADS):
        gate = jnp.where(blk < qi, gates[h * nbp:(h + 1) * nbp, :], NEG_INF)
        rank = jnp.zeros((nbp, TQ), jnp.int32)
        for j in range(nkv):
            gj = gate[j:j + 1, :]
            ahead = (gj > gate) | ((gj == gate) & (blk > j))
            rank = rank + jnp.where(ahead, 1, 0)
        sel = (rank < MOBA_TOPK) & (rank < qi)
        sel_ref[h * nbp:(h + 1) * nbp, :] = jnp.where(sel, 1.0, 0.0)

    def stage(slot, j, diag):
        if diag:
            mask = lambda h, s: jnp.where(_causal(False), s, NEG_INF)
        else:
            mask = lambda h, s: jnp.where(sel_ref[pl.ds(h * nbp + j, 1), :] > 0.5, s, NEG_INF)
        _stage_scores(k_ref, qbd_ref, s_ref, mx_ref, slot, j, mask)

    def update(slot, j, first):
        _softmax_update(s_ref, mx_ref, slot, vt_ref.at[j], m_ref, l_ref, acc_ref, first)

    _pipelined_sweep(qi, nkv, stage, update)
    _store_heads(o_ref, acc_ref, l_ref)


def _attention(kind, src, qk_width, v_col):
    b, s, _ = src.shape
    nkv = s // TQ
    vt = pltpu.VMEM((nkv, W_BRANCH, TQ), BF16)
    stat = pltpu.VMEM((N_HEADS, TQ), F32)
    acc = pltpu.VMEM((W_BRANCH, TQ), F32)
    qbd = pltpu.VMEM((N_HEADS // 2, 2 * TQ, 2 * qk_width // N_HEADS), BF16)
    staged = [pltpu.VMEM((2, N_HEADS // 2, TQ, 2 * TQ), F32), pltpu.VMEM((2, N_HEADS, TQ), F32)]
    if kind == "mla":
        body = functools.partial(_mla_kernel, nkv=nkv)
        scratch = [vt, qbd] + staged + [stat, stat, acc]
    elif kind == "sb":
        body = functools.partial(_sb_kernel, nkv=nkv)
        scratch = [vt, qbd, stat, acc]
    else:
        nbp = -(-nkv // 8) * 8
        body = functools.partial(_moba_kernel, nkv=nkv, nbp=nbp)
        scratch = [vt, qbd] + staged + [stat, stat, acc, pltpu.VMEM((N_HEADS * nbp, W_BRANCH), F32),
                   pltpu.VMEM((N_HEADS * nbp, TQ), F32)]
    return pl.pallas_call(
        body,
        grid=(b, nkv),
        in_specs=[
            pl.BlockSpec((1, TQ, qk_width), lambda i, t: (i, t, 0)),
            pl.BlockSpec((1, s, qk_width), lambda i, t: (i, 0, 1)),
            pl.BlockSpec((1, s, W_BRANCH), lambda i, t: (i, 0, v_col)),
        ],
        out_specs=pl.BlockSpec((1, TQ, W_BRANCH), lambda i, t: (i, t, 0)),
        out_shape=jax.ShapeDtypeStruct((b, s, W_BRANCH), BF16),
        scratch_shapes=scratch,
        compiler_params=pltpu.CompilerParams(
            dimension_semantics=("arbitrary", "arbitrary"), vmem_limit_bytes=VMEM_LIMIT),
        name="attn_" + kind,
    )(src, src, src)


def _merge_kernel(x_ref, gate1_ref, oa_ref, ob_ref, oc_ref, g_ref, wa_ref, wb_ref, wc_ref, wo_ref,
                  o_ref):
    merged = g_ref[0, :, :D_MODEL].astype(F32) * _dot(oa_ref[0], wa_ref[...])
    merged = merged + g_ref[0, :, D_MODEL:2 * D_MODEL].astype(F32) * _dot(ob_ref[0], wb_ref[...])
    merged = merged + g_ref[0, :, 2 * D_MODEL:].astype(F32) * _dot(oc_ref[0], wc_ref[...])
    o_ref[0] = x_ref[0] + gate1_ref[0] * _dot(merged.astype(BF16), wo_ref[...])


def _merge(x, gate1, oa, ob, oc, gates, wa, wb, wc, wo):
    b, s, d = x.shape
    tm = 512
    row = lambda w: pl.BlockSpec((1, tm, w), lambda i, j: (i, j, 0))
    vec = pl.BlockSpec((1, 1, d), lambda i, j: (i, 0, 0))
    return pl.pallas_call(
        _merge_kernel,
        grid=(b, s // tm),
        in_specs=[row(d), vec, row(W_BRANCH), row(W_BRANCH), row(W_BRANCH), row(3 * d),
                  _const_spec(wa.shape), _const_spec(wb.shape), _const_spec(wc.shape),
                  _const_spec(wo.shape)],
        out_specs=row(d),
        out_shape=jax.ShapeDtypeStruct((b, s, d), F32),
        compiler_params=pltpu.CompilerParams(
            dimension_semantics=("arbitrary", "arbitrary"), vmem_limit_bytes=VMEM_LIMIT),
        name="merge",
    )(x, gate1, oa, ob, oc, gates, wa, wb, wc, wo)


def _ffn_kernel(x_ref, shift_ref, scale_ref, gate_ref, g2_ref, w1_ref, w2_ref, gf_ref, o_ref, *,
                final_norm):
    x = x_ref[0]
    h = (_rms(x, g2_ref[...]) * (1.0 + scale_ref[0]) + shift_ref[0]).astype(BF16)
    acc = jnp.zeros_like(x)
    chunk = D_MODEL
    for c in range(D_FF // chunk):
        u = jnp.maximum(_dot(h, w1_ref[:, c * chunk:(c + 1) * chunk]), 0.0)
        acc = acc + _dot((u * u).astype(BF16), w2_ref[c * chunk:(c + 1) * chunk, :])
    y = x + gate_ref[0] * acc
    if final_norm:
        y = _rms(y, gf_ref[...])
    o_ref[0] = y


def _ffn(x, shift, scale, gate, g2, w1, w2, gf, final_norm):
    b, s, d = x.shape
    tm = 512
    row = pl.BlockSpec((1, tm, d), lambda i, j: (i, j, 0))
    vec = pl.BlockSpec((1, 1, d), lambda i, j: (i, 0, 0))
    return pl.pallas_call(
        functools.partial(_ffn_kernel, final_norm=final_norm),
        grid=(b, s // tm),
        in_specs=[row, vec, vec, vec, _const_spec((1, d)), _const_spec(w1.shape),
                  _const_spec(w2.shape), _const_spec((1, d))],
        out_specs=row,
        out_shape=jax.ShapeDtypeStruct((b, s, d), F32),
        compiler_params=pltpu.CompilerParams(
            dimension_semantics=("arbitrary", "arbitrary"), vmem_limit_bytes=VMEM_LIMIT),
        name="ffn",
    )(x, shift, scale, gate, g2, w1, w2, gf)


def _rope_tables(positions):
    pos = positions.astype(F32)[..., None]

    def angles(rot_dim):
        inv = ROPE_THETA ** (-jnp.arange(0, rot_dim, 2, dtype=F32) / rot_dim)
        ang = pos * inv
        return jnp.cos(ang), jnp.sin(ang)

    b, s = positions.shape
    ca, sa = angles(MLA_ROPE)
    one = lambda n: jnp.ones((b, s, n), F32)
    zero = lambda n: jnp.zeros((b, s, n), F32)
    cosa = jnp.concatenate([one(MLA_NOPE), ca, ca, one(LANE - MLA_QK)], axis=-1)
    sina = jnp.concatenate([zero(MLA_NOPE), -sa, sa, zero(LANE - MLA_QK)], axis=-1)
    cm, sm = angles(MOBA_ROT)
    cos_h = jnp.concatenate([cm, cm, one(HEAD - MOBA_ROT)], axis=-1)
    sin_h = jnp.concatenate([-sm, sm, zero(HEAD - MOBA_ROT)], axis=-1)
    cosm = jnp.concatenate([cos_h, cos_h], axis=-1)
    sinm = jnp.concatenate([sin_h, sin_h], axis=-1)
    return cosa, sina, cosm, sinm


def _layer_weights(w_in, w_uq, w_ukv):
    d = w_in.shape[0]
    o_kpe = MLA_Q_RANK + MLA_KV_RANK
    o_sb = o_kpe + MLA_ROPE
    kpe = jnp.concatenate([jnp.zeros((d, MLA_NOPE), F32), w_in[:, o_kpe:o_sb],
                           jnp.zeros((d, LANE - MLA_QK), F32)], axis=1)
    w_in_r = jnp.concatenate([w_in[:, :o_kpe], kpe, w_in[:, o_sb:]], axis=1).astype(BF16)
    wuq = jnp.pad(w_uq.reshape(MLA_Q_RANK, N_HEADS, MLA_QK),
                  ((0, 0), (0, 0), (0, LANE - MLA_QK))).reshape(MLA_Q_RANK, N_HEADS * LANE)
    ukv = w_ukv.reshape(MLA_KV_RANK, N_HEADS, MLA_NOPE + MLA_V)
    wk = jnp.pad(ukv[:, :, :MLA_NOPE], ((0, 0), (0, 0), (0, LANE - MLA_NOPE)))
    wukv = jnp.concatenate([wk.reshape(MLA_KV_RANK, N_HEADS * LANE),
                            ukv[:, :, MLA_NOPE:].reshape(MLA_KV_RANK, N_HEADS * MLA_V)], axis=1)
    return w_in_r, wuq.astype(BF16), wukv.astype(BF16)


def kernel(x, c, positions, w_ada, b_ada, norm1_g, norm2_g, w_in, q_norm_g, w_uq, kv_norm_g, w_ukv,
           w_o_mla, w_o_sb, w_o_moba, w_out, w_ff1, w_ff2, final_norm_g):
    b, s, d = x.shape
    depth = w_ada.shape[0]
    assert d == D_MODEL and s % 512 == 0 and w_in.shape[-1] == C_END - (LANE - MLA_ROPE)
    cosa, sina, cosm, sinm = _rope_tables(positions)
    mod = _adaln_mod(c, w_ada, b_ada)
    gf = final_norm_g.reshape(1, d)
    for l in range(depth):
        shift1, scale1, gate1, shift2, scale2, gate2 = [
            m.reshape(b, 1, d) for m in jnp.split(mod[l], 6, axis=-1)]
        w_in_r, wuq, wukv = _layer_weights(w_in[l], w_uq[l], w_ukv[l])
        mla, sb, mb, gates = _inproj(
            x, shift1, scale1, norm1_g[l].reshape(1, d), w_in_r, q_norm_g[l].reshape(1, -1), wuq,
            kv_norm_g[l].reshape(1, -1), wukv, cosa, sina, cosm, sinm)
        o_mla = _attention("mla", mla, N_HEADS * LANE, 4)
        o_sb = _attention("sb", sb, W_BRANCH, 2)
        o_mb = _attention("moba", mb, W_BRANCH, 2)
        x = _merge(x, gate1, o_mla, o_sb, o_mb, gates, w_o_mla[l].astype(BF16),
                   w_o_sb[l].astype(BF16), w_o_moba[l].astype(BF16), w_out[l].astype(BF16))
        x = _ffn(x, shift2, scale2, gate2, norm2_g[l].reshape(1, d), w_ff1[l].astype(BF16),
                 w_ff2[l].astype(BF16), gf, l == depth - 1)
    return x
```

```python
import functools
import math

import jax
import jax.numpy as jnp
from jax import lax
from jax.experimental import pallas as pl
from jax.experimental.pallas import tpu as pltpu

F32 = jnp.float32
BF16 = jnp.bfloat16

D_MODEL = 1024
N_HEADS = 8
MLA_NOPE = 64
MLA_ROPE = 32
MLA_V = 64
MLA_QK = MLA_NOPE + MLA_ROPE
MLA_Q_RANK = 768
MLA_KV_RANK = 256
HEAD = 64
W_BRANCH = N_HEADS * HEAD
MOBA_BLOCK = 256
MOBA_TOPK = 3
MOBA_ROT = HEAD // 4
ROPE_THETA = 500000.0
D_FF = 4 * D_MODEL
EPS = 1e-6
NEG_INF = -1e30
SB_DEAD = -104.0
LOG2E = 1.4426950408889634

LANE = 128
TQ = 256
VMEM_LIMIT = 56 * 1024 * 1024

C_KPE = MLA_Q_RANK + MLA_KV_RANK
C_REST = C_KPE + MLA_ROPE
R_MB = 3 * W_BRANCH
R_GATE = R_MB + 3 * W_BRANCH
D_IN = C_REST + R_GATE + 3 * D_MODEL
W_MLA_OUT = 2 * N_HEADS * LANE + N_HEADS * MLA_V


def _rms(xf, g):
    return xf * lax.rsqrt(jnp.mean(xf * xf, axis=-1, keepdims=True) + EPS) * g


def _dot(a, b):
    return jnp.dot(a, b, preferred_element_type=F32)


def _dot_nt(a, b):
    return lax.dot_general(a, b, (((1,), (1,)), ((), ())), preferred_element_type=F32)


def _rope_lanes(xb, cos, sin, half, first_half):
    rot = jnp.where(first_half, pltpu.roll(xb, LANE - half, 1), pltpu.roll(xb, half, 1))
    return xb * cos + rot * sin


def _mod_kernel(c_ref, w_ref, b_ref, o_ref):
    c = c_ref[...]
    c_act = c * (1.0 / (1.0 + jnp.exp(-c)))
    o_ref[0] = _dot(c_act.astype(BF16), w_ref[0].astype(BF16)) + b_ref[0]


def _adaln_mod(c, w_ada, b_ada):
    depth, d, n = w_ada.shape
    b = c.shape[0]
    tn = 1536
    return pl.pallas_call(
        _mod_kernel,
        grid=(depth, n // tn),
        in_specs=[
            pl.BlockSpec((b, d), lambda l, j: (0, 0)),
            pl.BlockSpec((1, d, tn), lambda l, j: (l, 0, j)),
            pl.BlockSpec((1, 1, tn), lambda l, j: (l, 0, j)),
        ],
        out_specs=pl.BlockSpec((1, b, tn), lambda l, j: (l, 0, j)),
        out_shape=jax.ShapeDtypeStruct((depth, b, n), F32),
        compiler_params=pltpu.CompilerParams(
            dimension_semantics=("arbitrary", "arbitrary"), vmem_limit_bytes=VMEM_LIMIT),
        name="adaln_mod",
    )(c, w_ada, b_ada.reshape(depth, 1, n))


def _inproj_kernel(x_ref, shift_ref, scale_ref, g1_ref, wlat_ref, wkpe_ref, w_ref, gq_ref, wuq_ref,
                   gkv_ref, wukv_ref, cosa_ref, sina_ref, cosm_ref, sinm_ref,
                   mla_ref, sb_ref, mb_ref, gate_ref):
    x = x_ref[0]
    h = _rms(x, g1_ref[...]) * (1.0 + scale_ref[0]) + shift_ref[0]
    h = h.astype(BF16)
    lane = lax.broadcasted_iota(jnp.int32, (1, LANE), 1)

    cosa, sina = cosa_ref[0], sina_ref[0]
    mla_first = lane < MLA_NOPE + MLA_ROPE // 2
    mla_scale = LOG2E / math.sqrt(MLA_QK)
    qlat = _dot(h, wlat_ref[:, :MLA_Q_RANK])
    q = _dot(_rms(qlat, gq_ref[...]).astype(BF16), wuq_ref[...])
    for hd in range(N_HEADS):
        qb = _rope_lanes(q[:, hd * LANE:(hd + 1) * LANE], cosa, sina, MLA_ROPE // 2, mla_first)
        mla_ref[0, :, hd * LANE:(hd + 1) * LANE] = (qb * mla_scale).astype(BF16)

    kpe = _rope_lanes(_dot(h, wkpe_ref[...]), cosa, sina, MLA_ROPE // 2, mla_first)
    ckv = _dot(h, wlat_ref[:, MLA_Q_RANK:])
    kv = _dot(_rms(ckv, gkv_ref[...]).astype(BF16), wukv_ref[...])
    k_off = N_HEADS * LANE
    for hd in range(N_HEADS):
        kb = kv[:, hd * LANE:(hd + 1) * LANE] + kpe
        mla_ref[0, :, k_off + hd * LANE:k_off + (hd + 1) * LANE] = kb.astype(BF16)
    mla_ref[0, :, 2 * k_off:] = kv[:, k_off:].astype(BF16)

    sb_scale = 1.0 / math.sqrt(HEAD)
    sb_ref[0, :, :W_BRANCH] = (_dot(h, w_ref[:, :W_BRANCH]) * sb_scale).astype(BF16)
    sb_ref[0, :, W_BRANCH:] = _dot(h, w_ref[:, W_BRANCH:R_MB]).astype(BF16)

    cosm, sinm = cosm_ref[0], sinm_ref[0]
    mb_first = (lane % HEAD) < MOBA_ROT // 2
    for part, scale in ((0, sb_scale * LOG2E), (1, 1.0)):
        c0 = R_MB + part * W_BRANCH
        acc = _dot(h, w_ref[:, c0:c0 + W_BRANCH])
        for cb in range(W_BRANCH // LANE):
            blk = _rope_lanes(acc[:, cb * LANE:(cb + 1) * LANE], cosm, sinm, MOBA_ROT // 2, mb_first)
            mb_ref[0, :, part * W_BRANCH + cb * LANE:part * W_BRANCH + (cb + 1) * LANE] = (
                blk * scale).astype(BF16)
    mb_ref[0, :, 2 * W_BRANCH:] = _dot(h, w_ref[:, R_MB + 2 * W_BRANCH:R_GATE]).astype(BF16)

    for br in range(3):
        c0 = R_GATE + br * D_MODEL
        gl = _dot(h, w_ref[:, c0:c0 + D_MODEL])
        gate_ref[0, :, br * D_MODEL:(br + 1) * D_MODEL] = (1.0 / (1.0 + jnp.exp(-gl))).astype(BF16)


def _const_spec(shape):
    nd = len(shape)
    return pl.BlockSpec(shape, lambda *_: (0,) * nd, pipeline_mode=pl.Buffered(1))


def _inproj(x, shift, scale, g1, wlat, wkpe, wrest, gq, wuq, gkv, wukv, cosa, sina, cosm, sinm):
    b, s, d = x.shape
    tm = TQ
    row = lambda w: pl.BlockSpec((1, tm, w), lambda i, j: (i, j, 0))
    vec = pl.BlockSpec((1, 1, d), lambda i, j: (i, 0, 0))
    return pl.pallas_call(
        _inproj_kernel,
        grid=(b, s // tm),
        in_specs=[row(d), vec, vec, _const_spec((1, d)), _const_spec(wlat.shape),
                  _const_spec(wkpe.shape), _const_spec(wrest.shape), _const_spec(gq.shape),
                  _const_spec(wuq.shape), _const_spec(gkv.shape), _const_spec(wukv.shape),
                  row(LANE), row(LANE), row(LANE), row(LANE)],
        out_specs=[row(W_MLA_OUT), row(3 * W_BRANCH), row(3 * W_BRANCH), row(3 * D_MODEL)],
        out_shape=[jax.ShapeDtypeStruct((b, s, W_MLA_OUT), BF16),
                   jax.ShapeDtypeStruct((b, s, 3 * W_BRANCH), BF16),
                   jax.ShapeDtypeStruct((b, s, 3 * W_BRANCH), BF16),
                   jax.ShapeDtypeStruct((b, s, 3 * D_MODEL), BF16)],
        compiler_params=pltpu.CompilerParams(
            dimension_semantics=("arbitrary", "arbitrary"), vmem_limit_bytes=VMEM_LIMIT),
        name="inproj",
    )(x, shift, scale, g1, wlat, wkpe, wrest, gq, wuq, gkv, wukv, cosa, sina, cosm, sinm)


def _transpose_values(v_ref, vt_ref, nkv):
    for j in range(nkv):
        for c in range(W_BRANCH // LANE):
            blk = v_ref[0, j * TQ:(j + 1) * TQ, c * LANE:(c + 1) * LANE]
            vt_ref[j, c * LANE:(c + 1) * LANE, :] = blk.astype(F32).T.astype(BF16)


def _store_heads(o_ref, acc_ref, l_ref):
    for c in range(W_BRANCH // LANE):
        parts = []
        for h in (2 * c, 2 * c + 1):
            part = acc_ref[h * HEAD:(h + 1) * HEAD, :]
            if l_ref is not None:
                part = part / l_ref[h:h + 1, :]
            parts.append(part)
        o_ref[0, :, c * LANE:(c + 1) * LANE] = jnp.concatenate(parts, axis=0).T.astype(BF16)


def _stack_pair_queries(q_ref, qbd_ref):
    pairs, _, w = qbd_ref.shape
    lane = lax.broadcasted_iota(jnp.int32, (1, w), 1)
    for p in range(pairs):
        q = q_ref[0, :, p * w:(p + 1) * w]
        qbd_ref[p, 0:TQ, :] = jnp.where(lane < w // 2, q, jnp.zeros_like(q))
        qbd_ref[p, TQ:2 * TQ, :] = jnp.where(lane >= w // 2, q, jnp.zeros_like(q))


def _pair_scores(k_ref, qbd_ref, row0, p):
    w = qbd_ref.shape[2]
    return _dot_nt(k_ref[0, pl.ds(row0, TQ), p * w:(p + 1) * w], qbd_ref[p])


def _pair_values(vt_pair, weights):
    return [_dot(vt_pair[a * HEAD:(a + 1) * HEAD, :], weights[a]) for a in range(2)]


def _causal(strict):
    key = lax.broadcasted_iota(jnp.int32, (TQ, TQ), 0)
    qry = lax.broadcasted_iota(jnp.int32, (TQ, TQ), 1)
    return key < qry if strict else key <= qry


def _stage_scores(k_ref, qbd_ref, s_ref, mx_ref, slot, j, mask):
    row0 = pl.multiple_of(j * TQ, TQ)
    for p in range(N_HEADS // 2):
        pair = _pair_scores(k_ref, qbd_ref, row0, p)
        for a in range(2):
            h = 2 * p + a
            s = mask(h, pair[:, a * TQ:(a + 1) * TQ])
            s_ref[slot, p, :, a * TQ:(a + 1) * TQ] = s
            mx_ref[slot, h:h + 1, :] = jnp.max(s, axis=0, keepdims=True)


def _softmax_update(s_ref, mx_ref, slot, vt_blk, m_ref, l_ref, acc_ref, first):
    for p in range(N_HEADS // 2):
        probs, alphas = [], []
        for a in range(2):
            h = 2 * p + a
            m_new = mx_ref[slot, h:h + 1, :]
            if not first:
                m_old = m_ref[h:h + 1, :]
                m_new = jnp.maximum(m_old, m_new)
                alphas.append(jnp.exp2(m_old - m_new))
            pr = jnp.exp2(s_ref[slot, p, :, a * TQ:(a + 1) * TQ] - m_new)
            l_blk = jnp.sum(pr, axis=0, keepdims=True)
            m_ref[h:h + 1, :] = m_new
            l_ref[h:h + 1, :] = l_blk if first else alphas[a] * l_ref[h:h + 1, :] + l_blk
            probs.append(pr.astype(BF16))
        for a, o in enumerate(_pair_values(vt_blk[p * LANE:(p + 1) * LANE, :], probs)):
            rows = slice((2 * p + a) * HEAD, (2 * p + a + 1) * HEAD)
            acc_ref[rows, :] = o if first else alphas[a] * acc_ref[rows, :] + o


def _pipelined_sweep(qi, nkv, stage, update):
    stage(0, qi, True)
    stage(1, 0, False)
    update(0, qi, True)

    def two_blocks(i, carry):
        j = 2 * i
        stage(0, j + 1, False)
        update(1, j, False)
        stage(1, jnp.minimum(j + 2, nkv - 1), False)
        update(0, j + 1, False)
        return carry

    lax.fori_loop(0, lax.shift_right_logical(qi, 1), two_blocks, 0)

    @pl.when((qi & 1) == 1)
    def _():
        update(1, qi - 1, False)


def _mla_kernel(q_ref, k_ref, v_ref, o_ref, vt_ref, qbd_ref, s_ref, mx_ref, m_ref, l_ref, acc_ref,
                *, nkv):
    qi = pl.program_id(1)

    @pl.when(qi == 0)
    def _():
        _transpose_values(v_ref, vt_ref, nkv)

    _stack_pair_queries(q_ref, qbd_ref)

    def stage(slot, j, diag):
        mask = (lambda h, s: jnp.where(_causal(False), s, NEG_INF)) if diag else (lambda h, s: s)
        _stage_scores(k_ref, qbd_ref, s_ref, mx_ref, slot, j, mask)

    def update(slot, j, first):
        _softmax_update(s_ref, mx_ref, slot, vt_ref.at[j], m_ref, l_ref, acc_ref, first)

    _pipelined_sweep(qi, nkv, stage, update)
    _store_heads(o_ref, acc_ref, l_ref)


def _sb_kernel(q_ref, k_ref, v_ref, o_ref, vt_ref, qbd_ref, run_ref, acc_ref, *, nkv):
    qi = pl.program_id(1)

    @pl.when(qi == 0)
    def _():
        _transpose_values(v_ref, vt_ref, nkv)

    _stack_pair_queries(q_ref, qbd_ref)

    def block(j, first):
        row0 = pl.multiple_of(j * TQ, TQ)
        suffix = jnp.where(_causal(False), 1.0, 0.0).astype(BF16)
        zs = [_pair_scores(k_ref, qbd_ref, row0, p) for p in range(N_HEADS // 2)]
        zs = [zs[h // 2][:, (h % 2) * TQ:(h % 2 + 1) * TQ] for h in range(N_HEADS)]
        incls = []
        for h in range(N_HEADS):
            z = zs[h]
            neg_abs = lax.bitcast_convert_type(
                lax.bitcast_convert_type(z, jnp.uint32) | jnp.uint32(0x80000000), F32)
            drop = jnp.maximum(z, 0.0) + jnp.log(1.0 + jnp.exp(neg_abs))
            if first:
                drop = jnp.where(_causal(True), drop, 0.0)
            hi = lax.bitcast_convert_type(
                lax.bitcast_convert_type(drop, jnp.uint32) & jnp.uint32(0xFFFF0000), F32)
            both = _dot(suffix, jnp.concatenate([hi.astype(BF16), (drop - hi).astype(BF16)], axis=1))
            incls.append(both[:, :TQ] + both[:, TQ:])
        for h in range(N_HEADS):
            rows = slice(h * HEAD, (h + 1) * HEAD)
            if first:
                w = jnp.where(_causal(True), jnp.exp(zs[h] - incls[h]), 0.0)
                run_ref[h:h + 1, :] = -incls[h][0:1, :]
                acc_ref[rows, :] = _dot(vt_ref[j, rows, :], w.astype(BF16))
            else:
                run = run_ref[h:h + 1, :]
                w = jnp.exp((zs[h] + run) - incls[h])
                run_ref[h:h + 1, :] = run - incls[h][0:1, :]
                acc_ref[rows, :] = acc_ref[rows, :] + _dot(vt_ref[j, rows, :], w.astype(BF16))

    def live(carry):
        i, top = carry
        return (i < qi) & (top > SB_DEAD)

    def body(carry):
        i, _ = carry
        block(qi - 1 - i, False)
        return i + 1, jnp.max(run_ref[...])

    block(qi, True)
    lax.while_loop(live, body, (jnp.int32(0), jnp.max(run_ref[...])))
    _store_heads(o_ref, acc_ref, None)


def _moba_kernel(q_ref, k_ref, v_ref, o_ref, vt_ref, qbd_ref, s_ref, mx_ref, m_ref, l_ref, acc_ref,
                 km_ref, sel_ref, *, nkv, nbp):
    qi = pl.program_id(1)
    lane = lax.broadcasted_iota(jnp.int32, (1, W_BRANCH), 1)

    @pl.when(qi == 0)
    def _():
        _transpose_values(v_ref, vt_ref, nkv)
        km_ref[...] = jnp.zeros_like(km_ref)
        for j in range(nkv):
            mean = jnp.mean(k_ref[0, j * TQ:(j + 1) * TQ, :].astype(F32), axis=0, keepdims=True)
            for h in range(N_HEADS):
                in_head = (lane >= h * HEAD) & (lane < (h + 1) * HEAD)
                km_ref[h * nbp + j:h * nbp + j + 1, :] = jnp.where(in_head, mean, 0.0)

    _stack_pair_queries(q_ref, qbd_ref)

    gates = lax.dot_general(km_ref[...], q_ref[0].astype(F32), (((1,), (1,)), ((), ())),
                            precision=lax.Precision.HIGHEST, preferred_element_type=F32)
    blk = lax.broadcasted_iota(jnp.int32, (nbp, TQ), 0)
    for h in range(N_HEADS):
        gate = jnp.where(blk < qi, gates[h * nbp:(h + 1) * nbp, :], NEG_INF)
        rank = jnp.zeros((nbp, TQ), jnp.int32)
        for j in range(nkv):
            gj = gate[j:j + 1, :]
            ahead = (gj > gate) | ((gj == gate) & (blk > j))
            rank = rank + jnp.where(ahead, 1, 0)
        sel = (rank < MOBA_TOPK) & (rank < qi)
        sel_ref[h * nbp:(h + 1) * nbp, :] = jnp.where(sel, 1.0, 0.0)

    def stage(slot, j, diag):
        if diag:
            mask = lambda h, s: jnp.where(_causal(False), s, NEG_INF)
        else:
            mask = lambda h, s: jnp.where(sel_ref[pl.ds(h * nbp + j, 1), :] > 0.5, s, NEG_INF)
        _stage_scores(k_ref, qbd_ref, s_ref, mx_ref, slot, j, mask)

    def update(slot, j, first):
        _softmax_update(s_ref, mx_ref, slot, vt_ref.at[j], m_ref, l_ref, acc_ref, first)

    _pipelined_sweep(qi, nkv, stage, update)
    _store_heads(o_ref, acc_ref, l_ref)


def _attention(kind, src, qk_width, v_col):
    b, s, _ = src.shape
    nkv = s // TQ
    vt = pltpu.VMEM((nkv, W_BRANCH, TQ), BF16)
    stat = pltpu.VMEM((N_HEADS, TQ), F32)
    acc = pltpu.VMEM((W_BRANCH, TQ), F32)
    qbd = pltpu.VMEM((N_HEADS // 2, 2 * TQ, 2 * qk_width // N_HEADS), BF16)
    staged = [pltpu.VMEM((2, N_HEADS // 2, TQ, 2 * TQ), F32), pltpu.VMEM((2, N_HEADS, TQ), F32)]
    if kind == "mla":
        body = functools.partial(_mla_kernel, nkv=nkv)
        scratch = [vt, qbd] + staged + [stat, stat, acc]
    elif kind == "sb":
        body = functools.partial(_sb_kernel, nkv=nkv)
        scratch = [vt, qbd, stat, acc]
    else:
        nbp = -(-nkv // 8) * 8
        body = functools.partial(_moba_kernel, nkv=nkv, nbp=nbp)
        scratch = [vt, qbd] + staged + [stat, stat, acc, pltpu.VMEM((N_HEADS * nbp, W_BRANCH), F32),
                   pltpu.VMEM((N_HEADS * nbp, TQ), F32)]
    return pl.pallas_call(
        body,
        grid=(b, nkv),
        in_specs=[
            pl.BlockSpec((1, TQ, qk_width), lambda i, t: (i, t, 0)),
            pl.BlockSpec((1, s, qk_width), lambda i, t: (i, 0, 1)),
            pl.BlockSpec((1, s, W_BRANCH), lambda i, t: (i, 0, v_col)),
        ],
        out_specs=pl.BlockSpec((1, TQ, W_BRANCH), lambda i, t: (i, t, 0)),
        out_shape=jax.ShapeDtypeStruct((b, s, W_BRANCH), BF16),
        scratch_shapes=scratch,
        compiler_params=pltpu.CompilerParams(
            dimension_semantics=("arbitrary", "arbitrary"), vmem_limit_bytes=VMEM_LIMIT),
        name="attn_" + kind,
    )(src, src, src)


def _merge_kernel(x_ref, gate1_ref, oa_ref, ob_ref, oc_ref, g_ref, wa_ref, wb_ref, wc_ref, wo_ref,
                  o_ref):
    merged = g_ref[0, :, :D_MODEL].astype(F32) * _dot(oa_ref[0], wa_ref[...])
    merged = merged + g_ref[0, :, D_MODEL:2 * D_MODEL].astype(F32) * _dot(ob_ref[0], wb_ref[...])
    merged = merged + g_ref[0, :, 2 * D_MODEL:].astype(F32) * _dot(oc_ref[0], wc_ref[...])
    o_ref[0] = x_ref[0] + gate1_ref[0] * _dot(merged.astype(BF16), wo_ref[...])


def _merge(x, gate1, oa, ob, oc, gates, wa, wb, wc, wo):
    b, s, d = x.shape
    tm = 512
    row = lambda w: pl.BlockSpec((1, tm, w), lambda i, j: (i, j, 0))
    vec = pl.BlockSpec((1, 1, d), lambda i, j: (i, 0, 0))
    return pl.pallas_call(
        _merge_kernel,
        grid=(b, s // tm),
        in_specs=[row(d), vec, row(W_BRANCH), row(W_BRANCH), row(W_BRANCH), row(3 * d),
                  _const_spec(wa.shape), _const_spec(wb.shape), _const_spec(wc.shape),
                  _const_spec(wo.shape)],
        out_specs=row(d),
        out_shape=jax.ShapeDtypeStruct((b, s, d), F32),
        compiler_params=pltpu.CompilerParams(
            dimension_semantics=("arbitrary", "arbitrary"), vmem_limit_bytes=VMEM_LIMIT),
        name="merge",
    )(x, gate1, oa, ob, oc, gates, wa, wb, wc, wo)


def _ffn_kernel(x_ref, shift_ref, scale_ref, gate_ref, g2_ref, w1_ref, w2_ref, gf_ref, o_ref, *,
                final_norm):
    x = x_ref[0]
    h = (_rms(x, g2_ref[...]) * (1.0 + scale_ref[0]) + shift_ref[0]).astype(BF16)
    acc = jnp.zeros_like(x)
    chunk = D_MODEL
    for c in range(D_FF // chunk):
        u = jnp.maximum(_dot(h, w1_ref[:, c * chunk:(c + 1) * chunk]), 0.0)
        acc = acc + _dot((u * u).astype(BF16), w2_ref[c * chunk:(c + 1) * chunk, :])
    y = x + gate_ref[0] * acc
    if final_norm:
        y = _rms(y, gf_ref[...])
    o_ref[0] = y


def _ffn(x, shift, scale, gate, g2, w1, w2, gf, final_norm):
    b, s, d = x.shape
    tm = 512
    row = pl.BlockSpec((1, tm, d), lambda i, j: (i, j, 0))
    vec = pl.BlockSpec((1, 1, d), lambda i, j: (i, 0, 0))
    return pl.pallas_call(
        functools.partial(_ffn_kernel, final_norm=final_norm),
        grid=(b, s // tm),
        in_specs=[row, vec, vec, vec, _const_spec((1, d)), _const_spec(w1.shape),
                  _const_spec(w2.shape), _const_spec((1, d))],
        out_specs=row,
        out_shape=jax.ShapeDtypeStruct((b, s, d), F32),
        compiler_params=pltpu.CompilerParams(
            dimension_semantics=("arbitrary", "arbitrary"), vmem_limit_bytes=VMEM_LIMIT),
        name="ffn",
    )(x, shift, scale, gate, g2, w1, w2, gf)


def _rope_tables(positions):
    pos = positions.astype(F32)[..., None]

    def angles(rot_dim):
        inv = ROPE_THETA ** (-jnp.arange(0, rot_dim, 2, dtype=F32) / rot_dim)
        ang = pos * inv
        return jnp.cos(ang), jnp.sin(ang)

    b, s = positions.shape
    ca, sa = angles(MLA_ROPE)
    one = lambda n: jnp.ones((b, s, n), F32)
    zero = lambda n: jnp.zeros((b, s, n), F32)
    cosa = jnp.concatenate([one(MLA_NOPE), ca, ca, one(LANE - MLA_QK)], axis=-1)
    sina = jnp.concatenate([zero(MLA_NOPE), -sa, sa, zero(LANE - MLA_QK)], axis=-1)
    cm, sm = angles(MOBA_ROT)
    cos_h = jnp.concatenate([cm, cm, one(HEAD - MOBA_ROT)], axis=-1)
    sin_h = jnp.concatenate([-sm, sm, zero(HEAD - MOBA_ROT)], axis=-1)
    cosm = jnp.concatenate([cos_h, cos_h], axis=-1)
    sinm = jnp.concatenate([sin_h, sin_h], axis=-1)
    return cosa, sina, cosm, sinm


def _layer_weights(w_in, w_uq, w_ukv):
    wlat = w_in[:, :C_KPE].astype(BF16)
    wkpe = jnp.pad(w_in[:, C_KPE:C_REST], ((0, 0), (MLA_NOPE, LANE - MLA_QK))).astype(BF16)
    wrest = w_in[:, C_REST:].astype(BF16)
    wuq = jnp.pad(w_uq.reshape(MLA_Q_RANK, N_HEADS, MLA_QK),
                  ((0, 0), (0, 0), (0, LANE - MLA_QK))).reshape(MLA_Q_RANK, N_HEADS * LANE)
    ukv = w_ukv.reshape(MLA_KV_RANK, N_HEADS, MLA_NOPE + MLA_V)
    wk = jnp.pad(ukv[:, :, :MLA_NOPE], ((0, 0), (0, 0), (0, LANE - MLA_NOPE)))
    wukv = jnp.concatenate([wk.reshape(MLA_KV_RANK, N_HEADS * LANE),
                            ukv[:, :, MLA_NOPE:].reshape(MLA_KV_RANK, N_HEADS * MLA_V)], axis=1)
    return wlat, wkpe, wrest, wuq.astype(BF16), wukv.astype(BF16)


def kernel(x, c, positions, w_ada, b_ada, norm1_g, norm2_g, w_in, q_norm_g, w_uq, kv_norm_g, w_ukv,
           w_o_mla, w_o_sb, w_o_moba, w_out, w_ff1, w_ff2, final_norm_g):
    b, s, d = x.shape
    depth = w_ada.shape[0]
    assert d == D_MODEL and s % 512 == 0 and w_in.shape[-1] == D_IN
    cosa, sina, cosm, sinm = _rope_tables(positions)
    mod = _adaln_mod(c, w_ada, b_ada)
    gf = final_norm_g.reshape(1, d)
    for l in range(depth):
        shift1, scale1, gate1, shift2, scale2, gate2 = [
            m.reshape(b, 1, d) for m in jnp.split(mod[l], 6, axis=-1)]
        wlat, wkpe, wrest, wuq, wukv = _layer_weights(w_in[l], w_uq[l], w_ukv[l])
        mla, sb, mb, gates = _inproj(
            x, shift1, scale1, norm1_g[l].reshape(1, d), wlat, wkpe, wrest,
            q_norm_g[l].reshape(1, -1), wuq, kv_norm_g[l].reshape(1, -1), wukv, cosa, sina, cosm, sinm)
        o_mla = _attention("mla", mla, N_HEADS * LANE, 4)
        o_sb = _attention("sb", sb, W_BRANCH, 2)
        o_mb = _attention("moba", mb, W_BRANCH, 2)
        x = _merge(x, gate1, o_mla, o_sb, o_mb, gates, w_o_mla[l].astype(BF16),
                   w_o_sb[l].astype(BF16), w_o_moba[l].astype(BF16), w_out[l].astype(BF16))
        x = _ffn(x, shift2, scale2, gate2, norm2_g[l].reshape(1, d), w_ff1[l].astype(BF16),
                 w_ff2[l].astype(BF16), gf, l == depth - 1)
    return x
```

```python
import functools
import math

import jax
import jax.numpy as jnp
from jax import lax
from jax.experimental import pallas as pl
from jax.experimental.pallas import tpu as pltpu

F32 = jnp.float32
BF16 = jnp.bfloat16

D_MODEL = 1024
N_HEADS = 8
MLA_NOPE = 64
MLA_ROPE = 32
MLA_V = 64
MLA_QK = MLA_NOPE + MLA_ROPE
MLA_Q_RANK = 768
MLA_KV_RANK = 256
HEAD = 64
W_BRANCH = N_HEADS * HEAD
MOBA_BLOCK = 256
MOBA_TOPK = 3
MOBA_ROT = HEAD // 4
ROPE_THETA = 500000.0
D_FF = 4 * D_MODEL
EPS = 1e-6
NEG_INF = -1e30
SB_DEAD = -104.0
LOG2E = 1.4426950408889634

LANE = 128
TQ = 256
VMEM_LIMIT = 56 * 1024 * 1024

C_KPE = MLA_Q_RANK + MLA_KV_RANK
C_REST = C_KPE + MLA_ROPE
R_MB = 3 * W_BRANCH
R_GATE = R_MB + 3 * W_BRANCH
D_IN = C_REST + R_GATE + 3 * D_MODEL
W_MLA_OUT = 2 * N_HEADS * LANE + N_HEADS * MLA_V


def _rms(xf, g):
    return xf * lax.rsqrt(jnp.mean(xf * xf, axis=-1, keepdims=True) + EPS) * g


def _dot(a, b):
    return jnp.dot(a, b, preferred_element_type=F32)


def _dot_nt(a, b):
    return lax.dot_general(a, b, (((1,), (1,)), ((), ())), preferred_element_type=F32)


def _rope_lanes(xb, cos, sin, half, first_half):
    rot = jnp.where(first_half, pltpu.roll(xb, LANE - half, 1), pltpu.roll(xb, half, 1))
    return xb * cos + rot * sin


def _mod_kernel(c_ref, w_ref, b_ref, o_ref):
    c = c_ref[...]
    c_act = c * (1.0 / (1.0 + jnp.exp(-c)))
    o_ref[0] = _dot(c_act.astype(BF16), w_ref[0].astype(BF16)) + b_ref[0]


def _expand_rope(t, lane):
    def put(table, lo, width, src):
        moved = pltpu.roll(t, (lo - src) % LANE, 1)
        return jnp.where((lane >= lo) & (lane < lo + width), moved, table)

    h1, h2 = MLA_ROPE // 2, MOBA_ROT // 2
    one, zero = jnp.ones_like(t), jnp.zeros_like(t)
    x1 = MLA_NOPE
    cosa = put(put(one, x1, h1, 0), x1 + h1, h1, 0)
    sina = put(zero, x1 + h1, h1, h1) - put(zero, x1, h1, h1)
    cosm, sinp, sinn = one, zero, zero
    for head in range(LANE // HEAD):
        cosm = put(put(cosm, head * HEAD, h2, 2 * h1), head * HEAD + h2, h2, 2 * h1)
        sinn = put(sinn, head * HEAD, h2, 2 * h1 + h2)
        sinp = put(sinp, head * HEAD + h2, h2, 2 * h1 + h2)
    return cosa, sina, cosm, sinp - sinn


def _adaln_mod(c, w_ada, b_ada):
    depth, d, n = w_ada.shape
    b = c.shape[0]
    tn = 1536
    return pl.pallas_call(
        _mod_kernel,
        grid=(depth, n // tn),
        in_specs=[
            pl.BlockSpec((b, d), lambda l, j: (0, 0)),
            pl.BlockSpec((1, d, tn), lambda l, j: (l, 0, j)),
            pl.BlockSpec((1, 1, tn), lambda l, j: (l, 0, j)),
        ],
        out_specs=pl.BlockSpec((1, b, tn), lambda l, j: (l, 0, j)),
        out_shape=jax.ShapeDtypeStruct((depth, b, n), F32),
        compiler_params=pltpu.CompilerParams(
            dimension_semantics=("arbitrary", "arbitrary"), vmem_limit_bytes=VMEM_LIMIT),
        name="adaln_mod",
    )(c, w_ada, b_ada.reshape(depth, 1, n))


def _inproj_kernel(x_ref, shift_ref, scale_ref, g1_ref, wlat_ref, wkpe_ref, w_ref, gq_ref, wuq_ref,
                   gkv_ref, wukv_ref, rope_ref, mla_ref, sb_ref, mb_ref, gate_ref):
    x = x_ref[0]
    h = _rms(x, g1_ref[...]) * (1.0 + scale_ref[0]) + shift_ref[0]
    h = h.astype(BF16)
    lane = lax.broadcasted_iota(jnp.int32, (1, LANE), 1)

    cosa, sina, cosm, sinm = _expand_rope(rope_ref[0], lane)
    mla_first = lane < MLA_NOPE + MLA_ROPE // 2
    mb_first = (lane % HEAD) < MOBA_ROT // 2
    mla_scale = LOG2E / math.sqrt(MLA_QK)
    sb_scale = 1.0 / math.sqrt(HEAD)

    qlat = _dot(h, wlat_ref[:, :MLA_Q_RANK])
    ckv = _dot(h, wlat_ref[:, MLA_Q_RANK:])
    kpe = _dot(h, wkpe_ref[...])

    sb_ref[0, :, :W_BRANCH] = (_dot(h, w_ref[:, :W_BRANCH]) * sb_scale).astype(BF16)
    sb_ref[0, :, W_BRANCH:] = _dot(h, w_ref[:, W_BRANCH:R_MB]).astype(BF16)

    q = _dot(_rms(qlat, gq_ref[...]).astype(BF16), wuq_ref[...])
    kv = _dot(_rms(ckv, gkv_ref[...]).astype(BF16), wukv_ref[...])

    for part, scale in ((0, sb_scale * LOG2E), (1, 1.0)):
        c0 = R_MB + part * W_BRANCH
        acc = _dot(h, w_ref[:, c0:c0 + W_BRANCH])
        for cb in range(W_BRANCH // LANE):
            blk = _rope_lanes(acc[:, cb * LANE:(cb + 1) * LANE], cosm, sinm, MOBA_ROT // 2, mb_first)
            mb_ref[0, :, part * W_BRANCH + cb * LANE:part * W_BRANCH + (cb + 1) * LANE] = (
                blk * scale).astype(BF16)
    mb_ref[0, :, 2 * W_BRANCH:] = _dot(h, w_ref[:, R_MB + 2 * W_BRANCH:R_GATE]).astype(BF16)

    kpe = _rope_lanes(kpe, cosa, sina, MLA_ROPE // 2, mla_first)
    k_off = N_HEADS * LANE
    for hd in range(N_HEADS):
        qb = _rope_lanes(q[:, hd * LANE:(hd + 1) * LANE], cosa, sina, MLA_ROPE // 2, mla_first)
        mla_ref[0, :, hd * LANE:(hd + 1) * LANE] = (qb * mla_scale).astype(BF16)
        kb = kv[:, hd * LANE:(hd + 1) * LANE] + kpe
        mla_ref[0, :, k_off + hd * LANE:k_off + (hd + 1) * LANE] = kb.astype(BF16)
    mla_ref[0, :, 2 * k_off:] = kv[:, k_off:].astype(BF16)

    for br in range(3):
        c0 = R_GATE + br * D_MODEL
        gl = _dot(h, w_ref[:, c0:c0 + D_MODEL])
        gate_ref[0, :, br * D_MODEL:(br + 1) * D_MODEL] = (1.0 / (1.0 + jnp.exp(-gl))).astype(BF16)


def _const_spec(shape):
    nd = len(shape)
    return pl.BlockSpec(shape, lambda *_: (0,) * nd, pipeline_mode=pl.Buffered(1))


def _inproj(x, shift, scale, g1, wlat, wkpe, wrest, gq, wuq, gkv, wukv, rope):
    b, s, d = x.shape
    tm = TQ
    row = lambda w: pl.BlockSpec((1, tm, w), lambda i, j: (i, j, 0))
    vec = pl.BlockSpec((1, 1, d), lambda i, j: (i, 0, 0))
    return pl.pallas_call(
        _inproj_kernel,
        grid=(b, s // tm),
        in_specs=[row(d), vec, vec, _const_spec((1, d)), _const_spec(wlat.shape),
                  _const_spec(wkpe.shape), _const_spec(wrest.shape), _const_spec(gq.shape),
                  _const_spec(wuq.shape), _const_spec(gkv.shape), _const_spec(wukv.shape),
                  row(LANE)],
        out_specs=[row(W_MLA_OUT), row(3 * W_BRANCH), row(3 * W_BRANCH), row(3 * D_MODEL)],
        out_shape=[jax.ShapeDtypeStruct((b, s, W_MLA_OUT), BF16),
                   jax.ShapeDtypeStruct((b, s, 3 * W_BRANCH), BF16),
                   jax.ShapeDtypeStruct((b, s, 3 * W_BRANCH), BF16),
                   jax.ShapeDtypeStruct((b, s, 3 * D_MODEL), BF16)],
        compiler_params=pltpu.CompilerParams(
            dimension_semantics=("arbitrary", "arbitrary"), vmem_limit_bytes=VMEM_LIMIT),
        name="inproj",
    )(x, shift, scale, g1, wlat, wkpe, wrest, gq, wuq, gkv, wukv, rope)


def _transpose_values(v_ref, vt_ref, nkv):
    for j in range(nkv):
        for c in range(W_BRANCH // LANE):
            blk = v_ref[0, j * TQ:(j + 1) * TQ, c * LANE:(c + 1) * LANE]
            vt_ref[j, c * LANE:(c + 1) * LANE, :] = blk.astype(F32).T.astype(BF16)


def _store_heads(o_ref, acc_ref, l_ref):
    for c in range(W_BRANCH // LANE):
        parts = []
        for h in (2 * c, 2 * c + 1):
            part = acc_ref[h * HEAD:(h + 1) * HEAD, :]
            if l_ref is not None:
                part = part / l_ref[h:h + 1, :]
            parts.append(part)
        o_ref[0, :, c * LANE:(c + 1) * LANE] = jnp.concatenate(parts, axis=0).T.astype(BF16)


def _stack_pair_queries(q_ref, qbd_ref):
    pairs, _, w = qbd_ref.shape
    lane = lax.broadcasted_iota(jnp.int32, (1, w), 1)
    for p in range(pairs):
        q = q_ref[0, :, p * w:(p + 1) * w]
        qbd_ref[p, 0:TQ, :] = jnp.where(lane < w // 2, q, jnp.zeros_like(q))
        qbd_ref[p, TQ:2 * TQ, :] = jnp.where(lane >= w // 2, q, jnp.zeros_like(q))


def _pair_scores(k_ref, qbd_ref, row0, p):
    w = qbd_ref.shape[2]
    return _dot_nt(k_ref[0, pl.ds(row0, TQ), p * w:(p + 1) * w], qbd_ref[p])


def _pair_values(vt_pair, weights):
    return [_dot(vt_pair[a * HEAD:(a + 1) * HEAD, :], weights[a]) for a in range(2)]


def _causal(strict):
    key = lax.broadcasted_iota(jnp.int32, (TQ, TQ), 0)
    qry = lax.broadcasted_iota(jnp.int32, (TQ, TQ), 1)
    return key < qry if strict else key <= qry


def _stage_scores(k_ref, qbd_ref, s_ref, mx_ref, slot, j, mask):
    row0 = pl.multiple_of(j * TQ, TQ)
    for p in range(N_HEADS // 2):
        pair = _pair_scores(k_ref, qbd_ref, row0, p)
        for a in range(2):
            h = 2 * p + a
            s = mask(h, pair[:, a * TQ:(a + 1) * TQ])
            s_ref[slot, p, :, a * TQ:(a + 1) * TQ] = s
            mx_ref[slot, h:h + 1, :] = jnp.max(s, axis=0, keepdims=True)


def _softmax_update(s_ref, mx_ref, slot, vt_blk, m_ref, l_ref, acc_ref, first):
    for p in range(N_HEADS // 2):
        probs, alphas = [], []
        for a in range(2):
            h = 2 * p + a
            m_new = mx_ref[slot, h:h + 1, :]
            if not first:
                m_old = m_ref[h:h + 1, :]
                m_new = jnp.maximum(m_old, m_new)
                alphas.append(jnp.exp2(m_old - m_new))
            pr = jnp.exp2(s_ref[slot, p, :, a * TQ:(a + 1) * TQ] - m_new)
            l_blk = jnp.sum(pr, axis=0, keepdims=True)
            m_ref[h:h + 1, :] = m_new
            l_ref[h:h + 1, :] = l_blk if first else alphas[a] * l_ref[h:h + 1, :] + l_blk
            probs.append(pr.astype(BF16))
        for a, o in enumerate(_pair_values(vt_blk[p * LANE:(p + 1) * LANE, :], probs)):
            rows = slice((2 * p + a) * HEAD, (2 * p + a + 1) * HEAD)
            acc_ref[rows, :] = o if first else alphas[a] * acc_ref[rows, :] + o


def _pipelined_sweep(qi, nkv, stage, update):
    stage(0, qi, True)
    stage(1, 0, False)
    update(0, qi, True)

    def two_blocks(i, carry):
        j = 2 * i
        stage(0, j + 1, False)
        update(1, j, False)
        stage(1, jnp.minimum(j + 2, nkv - 1), False)
        update(0, j + 1, False)
        return carry

    lax.fori_loop(0, lax.shift_right_logical(qi, 1), two_blocks, 0)

    @pl.when((qi & 1) == 1)
    def _():
        update(1, qi - 1, False)


def _mla_kernel(q_ref, k_ref, v_ref, o_ref, vt_ref, qbd_ref, s_ref, mx_ref, m_ref, l_ref, acc_ref,
                *, nkv):
    qi = pl.program_id(1)

    @pl.when(qi == 0)
    def _():
        _transpose_values(v_ref, vt_ref, nkv)

    _stack_pair_queries(q_ref, qbd_ref)

    def stage(slot, j, diag):
        mask = (lambda h, s: jnp.where(_causal(False), s, NEG_INF)) if diag else (lambda h, s: s)
        _stage_scores(k_ref, qbd_ref, s_ref, mx_ref, slot, j, mask)

    def update(slot, j, first):
        _softmax_update(s_ref, mx_ref, slot, vt_ref.at[j], m_ref, l_ref, acc_ref, first)

    _pipelined_sweep(qi, nkv, stage, update)
    _store_heads(o_ref, acc_ref, l_ref)


def _sb_kernel(q_ref, k_ref, v_ref, o_ref, vt_ref, qbd_ref, run_ref, acc_ref, *, nkv):
    qi = pl.program_id(1)

    @pl.when(qi == 0)
    def _():
        _transpose_values(v_ref, vt_ref, nkv)

    _stack_pair_queries(q_ref, qbd_ref)

    def block(j, first):
        row0 = pl.multiple_of(j * TQ, TQ)
        suffix = jnp.where(_causal(False), 1.0, 0.0).astype(BF16)
        zs = [_pair_scores(k_ref, qbd_ref, row0, p) for p in range(N_HEADS // 2)]
        zs = [zs[h // 2][:, (h % 2) * TQ:(h % 2 + 1) * TQ] for h in range(N_HEADS)]
        incls = []
        for h in range(N_HEADS):
            z = zs[h]
            neg_abs = lax.bitcast_convert_type(
                lax.bitcast_convert_type(z, jnp.uint32) | jnp.uint32(0x80000000), F32)
            drop = jnp.maximum(z, 0.0) + jnp.log(1.0 + jnp.exp(neg_abs))
            if first:
                drop = jnp.where(_causal(True), drop, 0.0)
            hi = lax.bitcast_convert_type(
                lax.bitcast_convert_type(drop, jnp.uint32) & jnp.uint32(0xFFFF0000), F32)
            both = _dot(suffix, jnp.concatenate([hi.astype(BF16), (drop - hi).astype(BF16)], axis=1))
            incls.append(both[:, :TQ] + both[:, TQ:])
        for h in range(N_HEADS):
            rows = slice(h * HEAD, (h + 1) * HEAD)
            if first:
                w = jnp.where(_causal(True), jnp.exp(zs[h] - incls[h]), 0.0)
                run_ref[h:h + 1, :] = -incls[h][0:1, :]
                acc_ref[rows, :] = _dot(vt_ref[j, rows, :], w.astype(BF16))
            else:
                run = run_ref[h:h + 1, :]
                w = jnp.exp((zs[h] + run) - incls[h])
                run_ref[h:h + 1, :] = run - incls[h][0:1, :]
                acc_ref[rows, :] = acc_ref[rows, :] + _dot(vt_ref[j, rows, :], w.astype(BF16))

    def live(carry):
        i, top = carry
        return (i < qi) & (top > SB_DEAD)

    def body(carry):
        i, _ = carry
        block(qi - 1 - i, False)
        return i + 1, jnp.max(run_ref[...])

    block(qi, True)
    lax.while_loop(live, body, (jnp.int32(0), jnp.max(run_ref[...])))
    _store_heads(o_ref, acc_ref, None)


def _moba_kernel(q_ref, k_ref, v_ref, o_ref, vt_ref, qbd_ref, s_ref, mx_ref, m_ref, l_ref, acc_ref,
                 km_ref, sel_ref, *, nkv, nbp):
    qi = pl.program_id(1)
    lane = lax.broadcasted_iota(jnp.int32, (1, W_BRANCH), 1)

    @pl.when(qi == 0)
    def _():
        _transpose_values(v_ref, vt_ref, nkv)
        km_ref[...] = jnp.zeros_like(km_ref)
        for j in range(nkv):
            mean = jnp.mean(k_ref[0, j * TQ:(j + 1) * TQ, :].astype(F32), axis=0, keepdims=True)
            for h in range(N_HEADS):
                in_head = (lane >= h * HEAD) & (lane < (h + 1) * HEAD)
                km_ref[h * nbp + j:h * nbp + j + 1, :] = jnp.where(in_head, mean, 0.0)

    _stack_pair_queries(q_ref, qbd_ref)

    gates = lax.dot_general(km_ref[...], q_ref[0].astype(F32), (((1,), (1,)), ((), ())),
                            precision=lax.Precision.HIGHEST, preferred_element_type=F32)
    blk = lax.broadcasted_iota(jnp.int32, (nbp, TQ), 0)
    for h in range(N_HEADS):
        gate = jnp.where(blk < qi, gates[h * nbp:(h + 1) * nbp, :], NEG_INF)
        rank = jnp.zeros((nbp, TQ), jnp.int32)
        for j in range(nkv):
            gj = gate[j:j + 1, :]
            ahead = (gj > gate) | ((gj == gate) & (blk > j))
            rank = rank + jnp.where(ahead, 1, 0)
        sel = (rank < MOBA_TOPK) & (rank < qi)
        sel_ref[h * nbp:(h + 1) * nbp, :] = jnp.where(sel, 1.0, 0.0)

    def stage(slot, j, diag):
        if diag:
            mask = lambda h, s: jnp.where(_causal(False), s, NEG_INF)
        else:
            mask = lambda h, s: jnp.where(sel_ref[pl.ds(h * nbp + j, 1), :] > 0.5, s, NEG_INF)
        _stage_scores(k_ref, qbd_ref, s_ref, mx_ref, slot, j, mask)

    def update(slot, j, first):
        _softmax_update(s_ref, mx_ref, slot, vt_ref.at[j], m_ref, l_ref, acc_ref, first)

    _pipelined_sweep(qi, nkv, stage, update)
    _store_heads(o_ref, acc_ref, l_ref)


def _attention(kind, src, qk_width, v_col):
    b, s, _ = src.shape
    nkv = s // TQ
    vt = pltpu.VMEM((nkv, W_BRANCH, TQ), BF16)
    stat = pltpu.VMEM((N_HEADS, TQ), F32)
    acc = pltpu.VMEM((W_BRANCH, TQ), F32)
    qbd = pltpu.VMEM((N_HEADS // 2, 2 * TQ, 2 * qk_width // N_HEADS), BF16)
    staged = [pltpu.VMEM((2, N_HEADS // 2, TQ, 2 * TQ), F32), pltpu.VMEM((2, N_HEADS, TQ), F32)]
    if kind == "mla":
        body = functools.partial(_mla_kernel, nkv=nkv)
        scratch = [vt, qbd] + staged + [stat, stat, acc]
    elif kind == "sb":
        body = functools.partial(_sb_kernel, nkv=nkv)
        scratch = [vt, qbd, stat, acc]
    else:
        nbp = -(-nkv // 8) * 8
        body = functools.partial(_moba_kernel, nkv=nkv, nbp=nbp)
        scratch = [vt, qbd] + staged + [stat, stat, acc, pltpu.VMEM((N_HEADS * nbp, W_BRANCH), F32),
                   pltpu.VMEM((N_HEADS * nbp, TQ), F32)]
    return pl.pallas_call(
        body,
        grid=(b, nkv),
        in_specs=[
            pl.BlockSpec((1, TQ, qk_width), lambda i, t: (i, t, 0)),
            pl.BlockSpec((1, s, qk_width), lambda i, t: (i, 0, 1)),
            pl.BlockSpec((1, s, W_BRANCH), lambda i, t: (i, 0, v_col)),
        ],
        out_specs=pl.BlockSpec((1, TQ, W_BRANCH), lambda i, t: (i, t, 0)),
        out_shape=jax.ShapeDtypeStruct((b, s, W_BRANCH), BF16),
        scratch_shapes=scratch,
        compiler_params=pltpu.CompilerParams(
            dimension_semantics=("arbitrary", "arbitrary"), vmem_limit_bytes=VMEM_LIMIT),
        name="attn_" + kind,
    )(src, src, src)


def _merge_kernel(x_ref, gate1_ref, oa_ref, ob_ref, oc_ref, g_ref, wa_ref, wb_ref, wc_ref, wo_ref,
                  o_ref):
    merged = g_ref[0, :, :D_MODEL].astype(F32) * _dot(oa_ref[0], wa_ref[...])
    merged = merged + g_ref[0, :, D_MODEL:2 * D_MODEL].astype(F32) * _dot(ob_ref[0], wb_ref[...])
    merged = merged + g_ref[0, :, 2 * D_MODEL:].astype(F32) * _dot(oc_ref[0], wc_ref[...])
    o_ref[0] = x_ref[0] + gate1_ref[0] * _dot(merged.astype(BF16), wo_ref[...])


def _merge(x, gate1, oa, ob, oc, gates, wa, wb, wc, wo):
    b, s, d = x.shape
    tm = 512
    row = lambda w: pl.BlockSpec((1, tm, w), lambda i, j: (i, j, 0))
    vec = pl.BlockSpec((1, 1, d), lambda i, j: (i, 0, 0))
    return pl.pallas_call(
        _merge_kernel,
        grid=(b, s // tm),
        in_specs=[row(d), vec, row(W_BRANCH), row(W_BRANCH), row(W_BRANCH), row(3 * d),
                  _const_spec(wa.shape), _const_spec(wb.shape), _const_spec(wc.shape),
                  _const_spec(wo.shape)],
        out_specs=row(d),
        out_shape=jax.ShapeDtypeStruct((b, s, d), F32),
        compiler_params=pltpu.CompilerParams(
            dimension_semantics=("arbitrary", "arbitrary"), vmem_limit_bytes=VMEM_LIMIT),
        name="merge",
    )(x, gate1, oa, ob, oc, gates, wa, wb, wc, wo)


def _ffn_kernel(x_ref, shift_ref, scale_ref, gate_ref, g2_ref, w1_ref, w2_ref, gf_ref, o_ref, *,
                final_norm):
    x = x_ref[0]
    h = (_rms(x, g2_ref[...]) * (1.0 + scale_ref[0]) + shift_ref[0]).astype(BF16)
    acc = jnp.zeros_like(x)
    chunk = D_MODEL
    for c in range(D_FF // chunk):
        u = jnp.maximum(_dot(h, w1_ref[:, c * chunk:(c + 1) * chunk]), 0.0)
        acc = acc + _dot((u * u).astype(BF16), w2_ref[c * chunk:(c + 1) * chunk, :])
    y = x + gate_ref[0] * acc
    if final_norm:
        y = _rms(y, gf_ref[...])
    o_ref[0] = y


def _ffn(x, shift, scale, gate, g2, w1, w2, gf, final_norm):
    b, s, d = x.shape
    tm = 512
    row = pl.BlockSpec((1, tm, d), lambda i, j: (i, j, 0))
    vec = pl.BlockSpec((1, 1, d), lambda i, j: (i, 0, 0))
    return pl.pallas_call(
        functools.partial(_ffn_kernel, final_norm=final_norm),
        grid=(b, s // tm),
        in_specs=[row, vec, vec, vec, _const_spec((1, d)), _const_spec(w1.shape),
                  _const_spec(w2.shape), _const_spec((1, d))],
        out_specs=row,
        out_shape=jax.ShapeDtypeStruct((b, s, d), F32),
        compiler_params=pltpu.CompilerParams(
            dimension_semantics=("arbitrary", "arbitrary"), vmem_limit_bytes=VMEM_LIMIT),
        name="ffn",
    )(x, shift, scale, gate, g2, w1, w2, gf)


def _rope_table(positions):
    pos = positions.astype(F32)[..., None]
    parts = []
    for rot_dim in (MLA_ROPE, MOBA_ROT):
        inv = ROPE_THETA ** (-jnp.arange(0, rot_dim, 2, dtype=F32) / rot_dim)
        parts += [jnp.cos(pos * inv), jnp.sin(pos * inv)]
    used = MLA_ROPE + MOBA_ROT
    return jnp.pad(jnp.concatenate(parts, axis=-1), ((0, 0), (0, 0), (0, LANE - used)))


def _layer_weights(w_in, w_uq, w_ukv):
    wlat = w_in[:, :C_KPE].astype(BF16)
    wkpe = jnp.pad(w_in[:, C_KPE:C_REST], ((0, 0), (MLA_NOPE, LANE - MLA_QK))).astype(BF16)
    wrest = w_in[:, C_REST:].astype(BF16)
    wuq = jnp.pad(w_uq.reshape(MLA_Q_RANK, N_HEADS, MLA_QK),
                  ((0, 0), (0, 0), (0, LANE - MLA_QK))).reshape(MLA_Q_RANK, N_HEADS * LANE)
    ukv = w_ukv.reshape(MLA_KV_RANK, N_HEADS, MLA_NOPE + MLA_V)
    wk = jnp.pad(ukv[:, :, :MLA_NOPE], ((0, 0), (0, 0), (0, LANE - MLA_NOPE)))
    wukv = jnp.concatenate([wk.reshape(MLA_KV_RANK, N_HEADS * LANE),
                            ukv[:, :, MLA_NOPE:].reshape(MLA_KV_RANK, N_HEADS * MLA_V)], axis=1)
    return wlat, wkpe, wrest, wuq.astype(BF16), wukv.astype(BF16)


def kernel(x, c, positions, w_ada, b_ada, norm1_g, norm2_g, w_in, q_norm_g, w_uq, kv_norm_g, w_ukv,
           w_o_mla, w_o_sb, w_o_moba, w_out, w_ff1, w_ff2, final_norm_g):
    b, s, d = x.shape
    depth = w_ada.shape[0]
    assert d == D_MODEL and s % 512 == 0 and w_in.shape[-1] == D_IN
    rope = _rope_table(positions)
    mod = _adaln_mod(c, w_ada, b_ada)
    gf = final_norm_g.reshape(1, d)
    for l in range(depth):
        shift1, scale1, gate1, shift2, scale2, gate2 = [
            m.reshape(b, 1, d) for m in jnp.split(mod[l], 6, axis=-1)]
        wlat, wkpe, wrest, wuq, wukv = _layer_weights(w_in[l], w_uq[l], w_ukv[l])
        mla, sb, mb, gates = _inproj(
            x, shift1, scale1, norm1_g[l].reshape(1, d), wlat, wkpe, wrest,
            q_norm_g[l].reshape(1, -1), wuq, kv_norm_g[l].reshape(1, -1), wukv, rope)
        o_mla = _attention("mla", mla, N_HEADS * LANE, 4)
        o_sb = _attention("sb", sb, W_BRANCH, 2)
        o_mb = _attention("moba", mb, W_BRANCH, 2)
        x = _merge(x, gate1, o_mla, o_sb, o_mb, gates, w_o_mla[l].astype(BF16),
                   w_o_sb[l].astype(BF16), w_o_moba[l].astype(BF16), w_out[l].astype(BF16))
        x = _ffn(x, shift2, scale2, gate2, norm2_g[l].reshape(1, d), w_ff1[l].astype(BF16),
                 w_ff2[l].astype(BF16), gf, l == depth - 1)
    return x
```

```python
import functools
import math

import jax
import jax.numpy as jnp
from jax import lax
from jax.experimental import pallas as pl
from jax.experimental.pallas import tpu as pltpu

F32 = jnp.float32
BF16 = jnp.bfloat16

D_MODEL = 1024
N_HEADS = 8
MLA_NOPE = 64
MLA_ROPE = 32
MLA_V = 64
MLA_QK = MLA_NOPE + MLA_ROPE
MLA_Q_RANK = 768
MLA_KV_RANK = 256
HEAD = 64
W_BRANCH = N_HEADS * HEAD
MOBA_BLOCK = 256
MOBA_TOPK = 3
MOBA_ROT = HEAD // 4
ROPE_THETA = 500000.0
D_FF = 4 * D_MODEL
EPS = 1e-6
NEG_INF = -1e30
SB_DEAD = -104.0
LOG2E = 1.4426950408889634

LANE = 128
TQ = 256
BF16_ROWS = 16
HEAD_L = HEAD + BF16_ROWS
VMEM_LIMIT = 56 * 1024 * 1024

C_KPE = MLA_Q_RANK + MLA_KV_RANK
C_REST = C_KPE + MLA_ROPE
R_MB = 3 * W_BRANCH
R_GATE = R_MB + 3 * W_BRANCH
D_IN = C_REST + R_GATE + 3 * D_MODEL
W_MLA_OUT = 2 * N_HEADS * LANE + N_HEADS * MLA_V


def _rms(xf, g):
    return xf * lax.rsqrt(jnp.mean(xf * xf, axis=-1, keepdims=True) + EPS) * g


def _dot(a, b):
    return jnp.dot(a, b, preferred_element_type=F32)


def _rope_lanes(xb, cos, sin, half, first_half):
    rot = jnp.where(first_half, pltpu.roll(xb, LANE - half, 1), pltpu.roll(xb, half, 1))
    return xb * cos + rot * sin


def _mod_kernel(c_ref, w_ref, b_ref, o_ref):
    c = c_ref[...]
    c_act = c * (1.0 / (1.0 + jnp.exp(-c)))
    o_ref[0] = _dot(c_act.astype(BF16), w_ref[0].astype(BF16)) + b_ref[0]


def _expand_rope(t, lane):
    def put(table, lo, width, src):
        moved = pltpu.roll(t, (lo - src) % LANE, 1)
        return jnp.where((lane >= lo) & (lane < lo + width), moved, table)

    h1, h2 = MLA_ROPE // 2, MOBA_ROT // 2
    one, zero = jnp.ones_like(t), jnp.zeros_like(t)
    x1 = MLA_NOPE
    cosa = put(put(one, x1, h1, 0), x1 + h1, h1, 0)
    sina = put(zero, x1 + h1, h1, h1) - put(zero, x1, h1, h1)
    cosm, sinp, sinn = one, zero, zero
    for head in range(LANE // HEAD):
        cosm = put(put(cosm, head * HEAD, h2, 2 * h1), head * HEAD + h2, h2, 2 * h1)
        sinn = put(sinn, head * HEAD, h2, 2 * h1 + h2)
        sinp = put(sinp, head * HEAD + h2, h2, 2 * h1 + h2)
    return cosa, sina, cosm, sinp - sinn


def _adaln_mod(c, w_ada, b_ada):
    depth, d, n = w_ada.shape
    b = c.shape[0]
    tn = 1536
    return pl.pallas_call(
        _mod_kernel,
        grid=(depth, n // tn),
        in_specs=[
            pl.BlockSpec((b, d), lambda l, j: (0, 0)),
            pl.BlockSpec((1, d, tn), lambda l, j: (l, 0, j)),
            pl.BlockSpec((1, 1, tn), lambda l, j: (l, 0, j)),
        ],
        out_specs=pl.BlockSpec((1, b, tn), lambda l, j: (l, 0, j)),
        out_shape=jax.ShapeDtypeStruct((depth, b, n), F32),
        compiler_params=pltpu.CompilerParams(
            dimension_semantics=("arbitrary", "arbitrary"), vmem_limit_bytes=VMEM_LIMIT),
        name="adaln_mod",
    )(c, w_ada, b_ada.reshape(depth, 1, n))


def _inproj_kernel(x_ref, shift_ref, scale_ref, g1_ref, wlat_ref, wkpe_ref, w_ref, gq_ref, wuq_ref,
                   gkv_ref, wukv_ref, rope_ref, mla_ref, sb_ref, mb_ref, gate_ref):
    x = x_ref[0]
    h = _rms(x, g1_ref[...]) * (1.0 + scale_ref[0]) + shift_ref[0]
    h = h.astype(BF16)
    lane = lax.broadcasted_iota(jnp.int32, (1, LANE), 1)

    cosa, sina, cosm, sinm = _expand_rope(rope_ref[0], lane)
    mla_first = lane < MLA_NOPE + MLA_ROPE // 2
    mb_first = (lane % HEAD) < MOBA_ROT // 2
    mla_scale = LOG2E / math.sqrt(MLA_QK)
    sb_scale = 1.0 / math.sqrt(HEAD)

    qlat = _dot(h, wlat_ref[:, :MLA_Q_RANK])
    ckv = _dot(h, wlat_ref[:, MLA_Q_RANK:])
    kpe = _dot(h, wkpe_ref[...])

    sb_ref[0, :, :W_BRANCH] = (_dot(h, w_ref[:, :W_BRANCH]) * sb_scale).astype(BF16)
    sb_ref[0, :, W_BRANCH:] = _dot(h, w_ref[:, W_BRANCH:R_MB]).astype(BF16)

    q = _dot(_rms(qlat, gq_ref[...]).astype(BF16), wuq_ref[...])
    kv = _dot(_rms(ckv, gkv_ref[...]).astype(BF16), wukv_ref[...])

    for part, scale in ((0, sb_scale * LOG2E), (1, 1.0)):
        c0 = R_MB + part * W_BRANCH
        acc = _dot(h, w_ref[:, c0:c0 + W_BRANCH])
        for cb in range(W_BRANCH // LANE):
            blk = _rope_lanes(acc[:, cb * LANE:(cb + 1) * LANE], cosm, sinm, MOBA_ROT // 2, mb_first)
            mb_ref[0, :, part * W_BRANCH + cb * LANE:part * W_BRANCH + (cb + 1) * LANE] = (
                blk * scale).astype(BF16)
    mb_ref[0, :, 2 * W_BRANCH:] = _dot(h, w_ref[:, R_MB + 2 * W_BRANCH:R_GATE]).astype(BF16)

    kpe = _rope_lanes(kpe, cosa, sina, MLA_ROPE // 2, mla_first)
    k_off = N_HEADS * LANE
    for hd in range(N_HEADS):
        qb = _rope_lanes(q[:, hd * LANE:(hd + 1) * LANE], cosa, sina, MLA_ROPE // 2, mla_first)
        mla_ref[0, :, hd * LANE:(hd + 1) * LANE] = (qb * mla_scale).astype(BF16)
        kb = kv[:, hd * LANE:(hd + 1) * LANE] + kpe
        mla_ref[0, :, k_off + hd * LANE:k_off + (hd + 1) * LANE] = kb.astype(BF16)
    mla_ref[0, :, 2 * k_off:] = kv[:, k_off:].astype(BF16)

    for br in range(3):
        c0 = R_GATE + br * D_MODEL
        gl = _dot(h, w_ref[:, c0:c0 + D_MODEL])
        gate_ref[0, :, br * D_MODEL:(br + 1) * D_MODEL] = (1.0 / (1.0 + jnp.exp(-gl))).astype(BF16)


def _const_spec(shape):
    nd = len(shape)
    return pl.BlockSpec(shape, lambda *_: (0,) * nd, pipeline_mode=pl.Buffered(1))


def _inproj(x, shift, scale, g1, wlat, wkpe, wrest, gq, wuq, gkv, wukv, rope):
    b, s, d = x.shape
    tm = TQ
    row = lambda w: pl.BlockSpec((1, tm, w), lambda i, j: (i, j, 0))
    vec = pl.BlockSpec((1, 1, d), lambda i, j: (i, 0, 0))
    return pl.pallas_call(
        _inproj_kernel,
        grid=(b, s // tm),
        in_specs=[row(d), vec, vec, _const_spec((1, d)), _const_spec(wlat.shape),
                  _const_spec(wkpe.shape), _const_spec(wrest.shape), _const_spec(gq.shape),
                  _const_spec(wuq.shape), _const_spec(gkv.shape), _const_spec(wukv.shape),
                  row(LANE)],
        out_specs=[row(W_MLA_OUT), row(3 * W_BRANCH), row(3 * W_BRANCH), row(3 * D_MODEL)],
        out_shape=[jax.ShapeDtypeStruct((b, s, W_MLA_OUT), BF16),
                   jax.ShapeDtypeStruct((b, s, 3 * W_BRANCH), BF16),
                   jax.ShapeDtypeStruct((b, s, 3 * W_BRANCH), BF16),
                   jax.ShapeDtypeStruct((b, s, 3 * D_MODEL), BF16)],
        compiler_params=pltpu.CompilerParams(
            dimension_semantics=("arbitrary", "arbitrary"), vmem_limit_bytes=VMEM_LIMIT),
        name="inproj",
    )(x, shift, scale, g1, wlat, wkpe, wrest, gq, wuq, gkv, wukv, rope)


def _transpose_values(v_ref, vt_ref, nkv, stride):
    for j in range(nkv):
        for c in range(W_BRANCH // LANE):
            blk = v_ref[0, j * TQ:(j + 1) * TQ, c * LANE:(c + 1) * LANE].astype(F32).T.astype(BF16)
            for a in range(2):
                h = 2 * c + a
                vt_ref[j, h * stride:h * stride + HEAD, :] = blk[a * HEAD:(a + 1) * HEAD, :]
                if stride > HEAD:
                    vt_ref[j, h * stride + HEAD:(h + 1) * stride, :] = jnp.ones((stride - HEAD, TQ), BF16)


def _store_heads(o_ref, acc_ref, stride):
    for c in range(W_BRANCH // LANE):
        parts = []
        for h in (2 * c, 2 * c + 1):
            part = acc_ref[h * stride:h * stride + HEAD, :]
            if stride > HEAD:
                part = part / acc_ref[h * stride + HEAD:h * stride + HEAD + 1, :]
            parts.append(part)
        o_ref[0, :, c * LANE:(c + 1) * LANE] = jnp.concatenate(parts, axis=0).T.astype(BF16)


def _stack_pair_queries(q_ref, qbd_ref):
    pairs, w, _ = qbd_ref.shape
    zero = jnp.zeros((w // 2, TQ), BF16)
    for p in range(pairs):
        qt = q_ref[0, :, p * w:(p + 1) * w].astype(F32).T.astype(BF16)
        qbd_ref[p, :w // 2, :TQ] = qt[:w // 2, :]
        qbd_ref[p, :w // 2, TQ:] = zero
        qbd_ref[p, w // 2:, :TQ] = zero
        qbd_ref[p, w // 2:, TQ:] = qt[w // 2:, :]


def _pair_scores(k_ref, qbd_ref, row0, p):
    w = qbd_ref.shape[1]
    return _dot(k_ref[0, pl.ds(row0, TQ), p * w:(p + 1) * w], qbd_ref[p])


def _causal(strict):
    key = lax.broadcasted_iota(jnp.int32, (TQ, TQ), 0)
    qry = lax.broadcasted_iota(jnp.int32, (TQ, TQ), 1)
    return key < qry if strict else key <= qry


def _stage_pair(k_ref, qbd_ref, s_ref, mx_ref, j, p, mask):
    pair = _pair_scores(k_ref, qbd_ref, pl.multiple_of(j * TQ, TQ), p)
    for a in range(2):
        h = 2 * p + a
        s = mask(h, pair[:, a * TQ:(a + 1) * TQ])
        s_ref[p, :, a * TQ:(a + 1) * TQ] = s
        mx_ref[h:h + 1, :] = jnp.max(s, axis=0, keepdims=True)


def _update_pair(s_ref, mx_ref, vt_blk, m_ref, acc_ref, p, first):
    for a in range(2):
        h = 2 * p + a
        rows = slice(h * HEAD_L, (h + 1) * HEAD_L)
        m_new = mx_ref[h:h + 1, :]
        if not first:
            m_old = m_ref[h:h + 1, :]
            m_new = jnp.maximum(m_old, m_new)
            alpha = jnp.exp2(m_old - m_new)
        pr = jnp.exp2(s_ref[p, :, a * TQ:(a + 1) * TQ] - m_new)
        m_ref[h:h + 1, :] = m_new
        o = _dot(vt_blk[rows, :], pr.astype(BF16))
        acc_ref[rows, :] = o if first else alpha * acc_ref[rows, :] + o


def _pipelined_sweep(qi, nkv, stage, update):
    pairs = range(N_HEADS // 2)

    def step(nxt, cur):
        for p in pairs:
            stage(*nxt, False, p)
            update(*cur, p)

    for p in pairs:
        stage(0, qi, True, p)
    step((1, 0), (0, qi, True))

    def two_blocks(i, carry):
        j = 2 * i
        step((0, j + 1), (1, j, False))
        step((1, jnp.minimum(j + 2, nkv - 1)), (0, j + 1, False))
        return carry

    lax.fori_loop(0, lax.shift_right_logical(qi, 1), two_blocks, 0)

    @pl.when((qi & 1) == 1)
    def _():
        for p in pairs:
            update(1, qi - 1, False, p)


def _mla_kernel(q_ref, k_ref, v_ref, o_ref, vt_ref, qbd_ref, s_ref, mx_ref, m_ref, acc_ref, *, nkv):
    qi = pl.program_id(1)

    @pl.when(qi == 0)
    def _():
        _transpose_values(v_ref, vt_ref, nkv, HEAD_L)

    _stack_pair_queries(q_ref, qbd_ref)

    def stage(slot, j, diag, p):
        mask = (lambda h, s: jnp.where(_causal(False), s, NEG_INF)) if diag else (lambda h, s: s)
        _stage_pair(k_ref, qbd_ref, s_ref.at[slot], mx_ref.at[slot], j, p, mask)

    def update(slot, j, first, p):
        _update_pair(s_ref.at[slot], mx_ref.at[slot], vt_ref.at[j], m_ref, acc_ref, p, first)

    _pipelined_sweep(qi, nkv, stage, update)
    _store_heads(o_ref, acc_ref, HEAD_L)


def _sb_kernel(q_ref, k_ref, v_ref, o_ref, vt_ref, qbd_ref, run_ref, acc_ref, *, nkv):
    qi = pl.program_id(1)

    @pl.when(qi == 0)
    def _():
        _transpose_values(v_ref, vt_ref, nkv, HEAD)

    _stack_pair_queries(q_ref, qbd_ref)

    def block(j, first):
        row0 = pl.multiple_of(j * TQ, TQ)
        suffix = jnp.where(_causal(False), 1.0, 0.0).astype(BF16)
        zs = [_pair_scores(k_ref, qbd_ref, row0, p) for p in range(N_HEADS // 2)]
        zs = [zs[h // 2][:, (h % 2) * TQ:(h % 2 + 1) * TQ] for h in range(N_HEADS)]
        incls = []
        for h in range(N_HEADS):
            z = zs[h]
            neg_abs = lax.bitcast_convert_type(
                lax.bitcast_convert_type(z, jnp.uint32) | jnp.uint32(0x80000000), F32)
            drop = jnp.maximum(z, 0.0) + jnp.log(1.0 + jnp.exp(neg_abs))
            if first:
                drop = jnp.where(_causal(True), drop, 0.0)
            hi = lax.bitcast_convert_type(
                lax.bitcast_convert_type(drop, jnp.uint32) & jnp.uint32(0xFFFF0000), F32)
            both = _dot(suffix, jnp.concatenate([hi.astype(BF16), (drop - hi).astype(BF16)], axis=1))
            incls.append(both[:, :TQ] + both[:, TQ:])
        for h in range(N_HEADS):
            rows = slice(h * HEAD, (h + 1) * HEAD)
            if first:
                w = jnp.where(_causal(True), jnp.exp(zs[h] - incls[h]), 0.0)
                run_ref[h:h + 1, :] = -incls[h][0:1, :]
                acc_ref[rows, :] = _dot(vt_ref[j, rows, :], w.astype(BF16))
            else:
                run = run_ref[h:h + 1, :]
                w = jnp.exp((zs[h] + run) - incls[h])
                run_ref[h:h + 1, :] = run - incls[h][0:1, :]
                acc_ref[rows, :] = acc_ref[rows, :] + _dot(vt_ref[j, rows, :], w.astype(BF16))

    def live(carry):
        i, top = carry
        return (i < qi) & (top > SB_DEAD)

    def body(carry):
        i, _ = carry
        block(qi - 1 - i, False)
        return i + 1, jnp.max(run_ref[...])

    block(qi, True)
    lax.while_loop(live, body, (jnp.int32(0), jnp.max(run_ref[...])))
    _store_heads(o_ref, acc_ref, HEAD)


def _moba_kernel(q_ref, k_ref, v_ref, o_ref, vt_ref, qbd_ref, s_ref, mx_ref, m_ref, acc_ref, km_ref,
                 sel_ref, *, nkv, nbp):
    qi = pl.program_id(1)
    lane = lax.broadcasted_iota(jnp.int32, (1, W_BRANCH), 1)

    @pl.when(qi == 0)
    def _():
        _transpose_values(v_ref, vt_ref, nkv, HEAD_L)
        km_ref[...] = jnp.zeros_like(km_ref)
        for j in range(nkv):
            mean = jnp.mean(k_ref[0, j * TQ:(j + 1) * TQ, :].astype(F32), axis=0, keepdims=True)
            for h in range(N_HEADS):
                in_head = (lane >= h * HEAD) & (lane < (h + 1) * HEAD)
                km_ref[h * nbp + j:h * nbp + j + 1, :] = jnp.where(in_head, mean, 0.0)

    _stack_pair_queries(q_ref, qbd_ref)

    gates = lax.dot_general(km_ref[...], q_ref[0].astype(F32), (((1,), (1,)), ((), ())),
                            precision=lax.Precision.HIGHEST, preferred_element_type=F32)
    blk = lax.broadcasted_iota(jnp.int32, (nbp, TQ), 0)
    for h in range(N_HEADS):
        gate = jnp.where(blk < qi, gates[h * nbp:(h + 1) * nbp, :], NEG_INF)
        rank = jnp.zeros((nbp, TQ), jnp.int32)
        for j in range(nkv):
            gj = gate[j:j + 1, :]
            ahead = (gj > gate) | ((gj == gate) & (blk > j))
            rank = rank + jnp.where(ahead, 1, 0)
        sel = (rank < MOBA_TOPK) & (rank < qi)
        sel_ref[h * nbp:(h + 1) * nbp, :] = jnp.where(sel, 1.0, 0.0)

    def stage(slot, j, diag, p):
        if diag:
            mask = lambda h, s: jnp.where(_causal(False), s, NEG_INF)
        else:
            mask = lambda h, s: jnp.where(sel_ref[pl.ds(h * nbp + j, 1), :] > 0.5, s, NEG_INF)
        _stage_pair(k_ref, qbd_ref, s_ref.at[slot], mx_ref.at[slot], j, p, mask)

    def update(slot, j, first, p):
        _update_pair(s_ref.at[slot], mx_ref.at[slot], vt_ref.at[j], m_ref, acc_ref, p, first)

    _pipelined_sweep(qi, nkv, stage, update)
    _store_heads(o_ref, acc_ref, HEAD_L)


def _attention(kind, src, qk_width, v_col):
    b, s, _ = src.shape
    nkv = s // TQ
    rows = N_HEADS * (HEAD if kind == "sb" else HEAD_L)
    vt = pltpu.VMEM((nkv, rows, TQ), BF16)
    stat = pltpu.VMEM((N_HEADS, TQ), F32)
    acc = pltpu.VMEM((rows, TQ), F32)
    qbd = pltpu.VMEM((N_HEADS // 2, 2 * qk_width // N_HEADS, 2 * TQ), BF16)
    staged = [pltpu.VMEM((2, N_HEADS // 2, TQ, 2 * TQ), F32), pltpu.VMEM((2, N_HEADS, TQ), F32)]
    if kind == "mla":
        body = functools.partial(_mla_kernel, nkv=nkv)
        scratch = [vt, qbd] + staged + [stat, acc]
    elif kind == "sb":
        body = functools.partial(_sb_kernel, nkv=nkv)
        scratch = [vt, qbd, stat, acc]
    else:
        nbp = -(-nkv // 8) * 8
        body = functools.partial(_moba_kernel, nkv=nkv, nbp=nbp)
        scratch = [vt, qbd] + staged + [stat, acc, pltpu.VMEM((N_HEADS * nbp, W_BRANCH), F32),
                   pltpu.VMEM((N_HEADS * nbp, TQ), F32)]
    return pl.pallas_call(
        body,
        grid=(b, nkv),
        in_specs=[
            pl.BlockSpec((1, TQ, qk_width), lambda i, t: (i, t, 0)),
            pl.BlockSpec((1, s, qk_width), lambda i, t: (i, 0, 1)),
            pl.BlockSpec((1, s, W_BRANCH), lambda i, t: (i, 0, v_col)),
        ],
        out_specs=pl.BlockSpec((1, TQ, W_BRANCH), lambda i, t: (i, t, 0)),
        out_shape=jax.ShapeDtypeStruct((b, s, W_BRANCH), BF16),
        scratch_shapes=scratch,
        compiler_params=pltpu.CompilerParams(
            dimension_semantics=("arbitrary", "arbitrary"), vmem_limit_bytes=VMEM_LIMIT),
        name="attn_" + kind,
    )(src, src, src)


def _merge_kernel(x_ref, gate1_ref, oa_ref, ob_ref, oc_ref, g_ref, wa_ref, wb_ref, wc_ref, wo_ref,
                  o_ref):
    merged = g_ref[0, :, :D_MODEL].astype(F32) * _dot(oa_ref[0], wa_ref[...])
    merged = merged + g_ref[0, :, D_MODEL:2 * D_MODEL].astype(F32) * _dot(ob_ref[0], wb_ref[...])
    merged = merged + g_ref[0, :, 2 * D_MODEL:].astype(F32) * _dot(oc_ref[0], wc_ref[...])
    o_ref[0] = x_ref[0] + gate1_ref[0] * _dot(merged.astype(BF16), wo_ref[...])


def _merge(x, gate1, oa, ob, oc, gates, wa, wb, wc, wo):
    b, s, d = x.shape
    tm = 512
    row = lambda w: pl.BlockSpec((1, tm, w), lambda i, j: (i, j, 0))
    vec = pl.BlockSpec((1, 1, d), lambda i, j: (i, 0, 0))
    return pl.pallas_call(
        _merge_kernel,
        grid=(b, s // tm),
        in_specs=[row(d), vec, row(W_BRANCH), row(W_BRANCH), row(W_BRANCH), row(3 * d),
                  _const_spec(wa.shape), _const_spec(wb.shape), _const_spec(wc.shape),
                  _const_spec(wo.shape)],
        out_specs=row(d),
        out_shape=jax.ShapeDtypeStruct((b, s, d), F32),
        compiler_params=pltpu.CompilerParams(
            dimension_semantics=("arbitrary", "arbitrary"), vmem_limit_bytes=VMEM_LIMIT),
        name="merge",
    )(x, gate1, oa, ob, oc, gates, wa, wb, wc, wo)


def _ffn_kernel(x_ref, shift_ref, scale_ref, gate_ref, g2_ref, w1_ref, w2_ref, gf_ref, o_ref, *,
                final_norm):
    x = x_ref[0]
    h = (_rms(x, g2_ref[...]) * (1.0 + scale_ref[0]) + shift_ref[0]).astype(BF16)
    acc = jnp.zeros_like(x)
    chunk = D_MODEL
    for c in range(D_FF // chunk):
        u = jnp.maximum(_dot(h, w1_ref[:, c * chunk:(c + 1) * chunk]), 0.0)
        acc = acc + _dot((u * u).astype(BF16), w2_ref[c * chunk:(c + 1) * chunk, :])
    y = x + gate_ref[0] * acc
    if final_norm:
        y = _rms(y, gf_ref[...])
    o_ref[0] = y


def _ffn(x, shift, scale, gate, g2, w1, w2, gf, final_norm):
    b, s, d = x.shape
    tm = 512
    row = pl.BlockSpec((1, tm, d), lambda i, j: (i, j, 0))
    vec = pl.BlockSpec((1, 1, d), lambda i, j: (i, 0, 0))
    return pl.pallas_call(
        functools.partial(_ffn_kernel, final_norm=final_norm),
        grid=(b, s // tm),
        in_specs=[row, vec, vec, vec, _const_spec((1, d)), _const_spec(w1.shape),
                  _const_spec(w2.shape), _const_spec((1, d))],
        out_specs=row,
        out_shape=jax.ShapeDtypeStruct((b, s, d), F32),
        compiler_params=pltpu.CompilerParams(
            dimension_semantics=("arbitrary", "arbitrary"), vmem_limit_bytes=VMEM_LIMIT),
        name="ffn",
    )(x, shift, scale, gate, g2, w1, w2, gf)


def _rope_table(positions):
    pos = positions.astype(F32)[..., None]
    parts = []
    for rot_dim in (MLA_ROPE, MOBA_ROT):
        inv = ROPE_THETA ** (-jnp.arange(0, rot_dim, 2, dtype=F32) / rot_dim)
        parts += [jnp.cos(pos * inv), jnp.sin(pos * inv)]
    used = MLA_ROPE + MOBA_ROT
    return jnp.pad(jnp.concatenate(parts, axis=-1), ((0, 0), (0, 0), (0, LANE - used)))


def _layer_weights(w_in, w_uq, w_ukv):
    wlat = w_in[:, :C_KPE].astype(BF16)
    wkpe = jnp.pad(w_in[:, C_KPE:C_REST], ((0, 0), (MLA_NOPE, LANE - MLA_QK))).astype(BF16)
    wrest = w_in[:, C_REST:].astype(BF16)
    wuq = jnp.pad(w_uq.reshape(MLA_Q_RANK, N_HEADS, MLA_QK),
                  ((0, 0), (0, 0), (0, LANE - MLA_QK))).reshape(MLA_Q_RANK, N_HEADS * LANE)
    ukv = w_ukv.reshape(MLA_KV_RANK, N_HEADS, MLA_NOPE + MLA_V)
    wk = jnp.pad(ukv[:, :, :MLA_NOPE], ((0, 0), (0, 0), (0, LANE - MLA_NOPE)))
    wukv = jnp.concatenate([wk.reshape(MLA_KV_RANK, N_HEADS * LANE),
                            ukv[:, :, MLA_NOPE:].reshape(MLA_KV_RANK, N_HEADS * MLA_V)], axis=1)
    return wlat, wkpe, wrest, wuq.astype(BF16), wukv.astype(BF16)


def kernel(x, c, positions, w_ada, b_ada, norm1_g, norm2_g, w_in, q_norm_g, w_uq, kv_norm_g, w_ukv,
           w_o_mla, w_o_sb, w_o_moba, w_out, w_ff1, w_ff2, final_norm_g):
    b, s, d = x.shape
    depth = w_ada.shape[0]
    assert d == D_MODEL and s % 512 == 0 and w_in.shape[-1] == D_IN
    rope = _rope_table(positions)
    mod = _adaln_mod(c, w_ada, b_ada)
    gf = final_norm_g.reshape(1, d)
    for l in range(depth):
        shift1, scale1, gate1, shift2, scale2, gate2 = [
            m.reshape(b, 1, d) for m in jnp.split(mod[l], 6, axis=-1)]
        wlat, wkpe, wrest, wuq, wukv = _layer_weights(w_in[l], w_uq[l], w_ukv[l])
        mla, sb, mb, gates = _inproj(
            x, shift1, scale1, norm1_g[l].reshape(1, d), wlat, wkpe, wrest,
            q_norm_g[l].reshape(1, -1), wuq, kv_norm_g[l].reshape(1, -1), wukv, rope)
        o_mla = _attention("mla", mla, N_HEADS * LANE, 4)
        o_sb = _attention("sb", sb, W_BRANCH, 2)
        o_mb = _attention("moba", mb, W_BRANCH, 2)
        x = _merge(x, gate1, o_mla, o_sb, o_mb, gates, w_o_mla[l].astype(BF16),
                   w_o_sb[l].astype(BF16), w_o_moba[l].astype(BF16), w_out[l].astype(BF16))
        x = _ffn(x, shift2, scale2, gate2, norm2_g[l].reshape(1, d), w_ff1[l].astype(BF16),
                 w_ff2[l].astype(BF16), gf, l == depth - 1)
    return x
```

```python
import functools
import math

import jax
import jax.numpy as jnp
from jax import lax
from jax.experimental import pallas as pl
from jax.experimental.pallas import tpu as pltpu

F32 = jnp.float32
BF16 = jnp.bfloat16

D_MODEL = 1024
N_HEADS = 8
MLA_NOPE = 64
MLA_ROPE = 32
MLA_V = 64
MLA_QK = MLA_NOPE + MLA_ROPE
MLA_Q_RANK = 768
MLA_KV_RANK = 256
HEAD = 64
W_BRANCH = N_HEADS * HEAD
MOBA_BLOCK = 256
MOBA_TOPK = 3
MOBA_ROT = HEAD // 4
ROPE_THETA = 500000.0
D_FF = 4 * D_MODEL
EPS = 1e-6
NEG_INF = -1e30
SB_DEAD = -104.0
LOG2E = 1.4426950408889634

LANE = 128
TQ = 256
BF16_ROWS = 16
HEAD_L = HEAD + BF16_ROWS
VMEM_LIMIT = 56 * 1024 * 1024

C_KPE = MLA_Q_RANK + MLA_KV_RANK
C_REST = C_KPE + MLA_ROPE
R_MB = 3 * W_BRANCH
R_GATE = R_MB + 3 * W_BRANCH
D_IN = C_REST + R_GATE + 3 * D_MODEL
W_MLA_OUT = 2 * N_HEADS * LANE + N_HEADS * MLA_V


def _rms(xf, g):
    return xf * lax.rsqrt(jnp.mean(xf * xf, axis=-1, keepdims=True) + EPS) * g


def _dot(a, b):
    return jnp.dot(a, b, preferred_element_type=F32)


def _rope_lanes(xb, cos, sin, half, first_half):
    rot = jnp.where(first_half, pltpu.roll(xb, LANE - half, 1), pltpu.roll(xb, half, 1))
    return xb * cos + rot * sin


def _mod_kernel(c_ref, w_ref, b_ref, o_ref):
    c = c_ref[...]
    c_act = c * (1.0 / (1.0 + jnp.exp(-c)))
    o_ref[0] = _dot(c_act.astype(BF16), w_ref[0].astype(BF16)) + b_ref[0]


def _expand_rope(t, lane):
    def put(table, lo, width, src):
        moved = pltpu.roll(t, (lo - src) % LANE, 1)
        return jnp.where((lane >= lo) & (lane < lo + width), moved, table)

    h1, h2 = MLA_ROPE // 2, MOBA_ROT // 2
    one, zero = jnp.ones_like(t), jnp.zeros_like(t)
    x1 = MLA_NOPE
    cosa = put(put(one, x1, h1, 0), x1 + h1, h1, 0)
    sina = put(zero, x1 + h1, h1, h1) - put(zero, x1, h1, h1)
    cosm, sinp, sinn = one, zero, zero
    for head in range(LANE // HEAD):
        cosm = put(put(cosm, head * HEAD, h2, 2 * h1), head * HEAD + h2, h2, 2 * h1)
        sinn = put(sinn, head * HEAD, h2, 2 * h1 + h2)
        sinp = put(sinp, head * HEAD + h2, h2, 2 * h1 + h2)
    return cosa, sina, cosm, sinp - sinn


def _adaln_mod(c, w_ada, b_ada):
    depth, d, n = w_ada.shape
    b = c.shape[0]
    tn = 1536
    return pl.pallas_call(
        _mod_kernel,
        grid=(depth, n // tn),
        in_specs=[
            pl.BlockSpec((b, d), lambda l, j: (0, 0)),
            pl.BlockSpec((1, d, tn), lambda l, j: (l, 0, j)),
            pl.BlockSpec((1, 1, tn), lambda l, j: (l, 0, j)),
        ],
        out_specs=pl.BlockSpec((1, b, tn), lambda l, j: (l, 0, j)),
        out_shape=jax.ShapeDtypeStruct((depth, b, n), F32),
        compiler_params=pltpu.CompilerParams(
            dimension_semantics=("arbitrary", "arbitrary"), vmem_limit_bytes=VMEM_LIMIT),
        name="adaln_mod",
    )(c, w_ada, b_ada.reshape(depth, 1, n))


def _inproj_kernel(x_ref, shift_ref, scale_ref, g1_ref, wlat_ref, wkpe_ref, w_ref, gq_ref, wuq_ref,
                   gkv_ref, wukv_ref, rope_ref, mla_ref, sb_ref, mb_ref, gate_ref):
    x = x_ref[0]
    h = _rms(x, g1_ref[...]) * (1.0 + scale_ref[0]) + shift_ref[0]
    h = h.astype(BF16)
    lane = lax.broadcasted_iota(jnp.int32, (1, LANE), 1)

    cosa, sina, cosm, sinm = _expand_rope(rope_ref[0], lane)
    mla_first = lane < MLA_NOPE + MLA_ROPE // 2
    mb_first = (lane % HEAD) < MOBA_ROT // 2
    mla_scale = LOG2E / math.sqrt(MLA_QK)
    sb_scale = 1.0 / math.sqrt(HEAD)

    qlat = _dot(h, wlat_ref[:, :MLA_Q_RANK])
    ckv = _dot(h, wlat_ref[:, MLA_Q_RANK:])
    kpe = _dot(h, wkpe_ref[...])

    sb_ref[0, :, :W_BRANCH] = (_dot(h, w_ref[:, :W_BRANCH]) * sb_scale).astype(BF16)
    sb_ref[0, :, W_BRANCH:] = _dot(h, w_ref[:, W_BRANCH:R_MB]).astype(BF16)

    q = _dot(_rms(qlat, gq_ref[...]).astype(BF16), wuq_ref[...])
    kv = _dot(_rms(ckv, gkv_ref[...]).astype(BF16), wukv_ref[...])

    for part, scale in ((0, sb_scale * LOG2E), (1, 1.0)):
        c0 = R_MB + part * W_BRANCH
        acc = _dot(h, w_ref[:, c0:c0 + W_BRANCH])
        for cb in range(W_BRANCH // LANE):
            blk = _rope_lanes(acc[:, cb * LANE:(cb + 1) * LANE], cosm, sinm, MOBA_ROT // 2, mb_first)
            mb_ref[0, :, part * W_BRANCH + cb * LANE:part * W_BRANCH + (cb + 1) * LANE] = (
                blk * scale).astype(BF16)
    mb_ref[0, :, 2 * W_BRANCH:] = _dot(h, w_ref[:, R_MB + 2 * W_BRANCH:R_GATE]).astype(BF16)

    kpe = _rope_lanes(kpe, cosa, sina, MLA_ROPE // 2, mla_first)
    k_off = N_HEADS * LANE
    for hd in range(N_HEADS):
        qb = _rope_lanes(q[:, hd * LANE:(hd + 1) * LANE], cosa, sina, MLA_ROPE // 2, mla_first)
        mla_ref[0, :, hd * LANE:(hd + 1) * LANE] = (qb * mla_scale).astype(BF16)
        kb = kv[:, hd * LANE:(hd + 1) * LANE] + kpe
        mla_ref[0, :, k_off + hd * LANE:k_off + (hd + 1) * LANE] = kb.astype(BF16)
    mla_ref[0, :, 2 * k_off:] = kv[:, k_off:].astype(BF16)

    for br in range(3):
        c0 = R_GATE + br * D_MODEL
        gl = _dot(h, w_ref[:, c0:c0 + D_MODEL])
        gate_ref[0, :, br * D_MODEL:(br + 1) * D_MODEL] = (1.0 / (1.0 + jnp.exp(-gl))).astype(BF16)


def _const_spec(shape):
    nd = len(shape)
    return pl.BlockSpec(shape, lambda *_: (0,) * nd, pipeline_mode=pl.Buffered(1))


def _inproj(x, shift, scale, g1, wlat, wkpe, wrest, gq, wuq, gkv, wukv, rope):
    b, s, d = x.shape
    tm = TQ
    row = lambda w: pl.BlockSpec((1, tm, w), lambda i, j: (i, j, 0))
    vec = pl.BlockSpec((1, 1, d), lambda i, j: (i, 0, 0))
    return pl.pallas_call(
        _inproj_kernel,
        grid=(b, s // tm),
        in_specs=[row(d), vec, vec, _const_spec((1, d)), _const_spec(wlat.shape),
                  _const_spec(wkpe.shape), _const_spec(wrest.shape), _const_spec(gq.shape),
                  _const_spec(wuq.shape), _const_spec(gkv.shape), _const_spec(wukv.shape),
                  row(LANE)],
        out_specs=[row(W_MLA_OUT), row(3 * W_BRANCH), row(3 * W_BRANCH), row(3 * D_MODEL)],
        out_shape=[jax.ShapeDtypeStruct((b, s, W_MLA_OUT), BF16),
                   jax.ShapeDtypeStruct((b, s, 3 * W_BRANCH), BF16),
                   jax.ShapeDtypeStruct((b, s, 3 * W_BRANCH), BF16),
                   jax.ShapeDtypeStruct((b, s, 3 * D_MODEL), BF16)],
        compiler_params=pltpu.CompilerParams(
            dimension_semantics=("arbitrary", "arbitrary"), vmem_limit_bytes=VMEM_LIMIT),
        name="inproj",
    )(x, shift, scale, g1, wlat, wkpe, wrest, gq, wuq, gkv, wukv, rope)


def _transpose_values(v_ref, vt_ref, nkv, stride):
    for j in range(nkv):
        for c in range(W_BRANCH // LANE):
            blk = v_ref[0, j * TQ:(j + 1) * TQ, c * LANE:(c + 1) * LANE].astype(F32).T.astype(BF16)
            for a in range(2):
                h = 2 * c + a
                vt_ref[j, h * stride:h * stride + HEAD, :] = blk[a * HEAD:(a + 1) * HEAD, :]
                if stride > HEAD:
                    vt_ref[j, h * stride + HEAD:(h + 1) * stride, :] = jnp.ones((stride - HEAD, TQ), BF16)


def _store_heads(o_ref, acc_ref, stride, row0=0):
    for c in range(W_BRANCH // LANE):
        parts = []
        for h in (2 * c, 2 * c + 1):
            part = acc_ref[h * stride:h * stride + HEAD, :]
            if stride > HEAD:
                part = part / acc_ref[h * stride + HEAD:h * stride + HEAD + 1, :]
            parts.append(part)
        o_ref[0, row0:row0 + TQ, c * LANE:(c + 1) * LANE] = (
            jnp.concatenate(parts, axis=0).T.astype(BF16))


def _stack_pair_queries(q_ref, qbd_ref, row0=0):
    pairs, w, _ = qbd_ref.shape
    zero = jnp.zeros((w // 2, TQ), BF16)
    for p in range(pairs):
        qt = q_ref[0, row0:row0 + TQ, p * w:(p + 1) * w].astype(F32).T.astype(BF16)
        qbd_ref[p, :w // 2, :TQ] = qt[:w // 2, :]
        qbd_ref[p, :w // 2, TQ:] = zero
        qbd_ref[p, w // 2:, :TQ] = zero
        qbd_ref[p, w // 2:, TQ:] = qt[w // 2:, :]


def _pair_scores(k_ref, qbd_ref, row0, p):
    w = qbd_ref.shape[1]
    return _dot(k_ref[0, pl.ds(row0, TQ), p * w:(p + 1) * w], qbd_ref[p])


def _causal(strict):
    key = lax.broadcasted_iota(jnp.int32, (TQ, TQ), 0)
    qry = lax.broadcasted_iota(jnp.int32, (TQ, TQ), 1)
    return key < qry if strict else key <= qry


def _stage_pair(k_ref, qbd_ref, s_ref, mx_ref, j, p, mask):
    pair = _pair_scores(k_ref, qbd_ref, pl.multiple_of(j * TQ, TQ), p)
    for a in range(2):
        h = 2 * p + a
        s = mask(h, pair[:, a * TQ:(a + 1) * TQ])
        s_ref[p, :, a * TQ:(a + 1) * TQ] = s
        mx_ref[h:h + 1, :] = jnp.max(s, axis=0, keepdims=True)


def _update_pair(s_ref, mx_ref, vt_blk, m_ref, acc_ref, p, first):
    for a in range(2):
        h = 2 * p + a
        rows = slice(h * HEAD_L, (h + 1) * HEAD_L)
        m_new = mx_ref[h:h + 1, :]
        if not first:
            m_old = m_ref[h:h + 1, :]
            m_new = jnp.maximum(m_old, m_new)
            alpha = jnp.exp2(m_old - m_new)
        pr = jnp.exp2(s_ref[p, :, a * TQ:(a + 1) * TQ] - m_new)
        m_ref[h:h + 1, :] = m_new
        o = _dot(vt_blk[rows, :], pr.astype(BF16))
        acc_ref[rows, :] = o if first else alpha * acc_ref[rows, :] + o


def _paired_sweep(t, stage, update):
    pairs = range(N_HEADS // 2)
    tiles = (0, 1)

    def step(nxt_slot, nxt_j, cur_slot, cur_j, first):
        for p in pairs:
            for u in tiles:
                stage(u, nxt_slot, nxt_j, False, p)
            for u in tiles:
                update(u, cur_slot, cur_j[u], first, p)

    for p in pairs:
        for u in tiles:
            stage(u, 0, 2 * t + u, True, p)
    step(1, 0, 0, (2 * t, 2 * t + 1), True)

    def two_blocks(i, carry):
        j = 2 * i
        step(0, j + 1, 1, (j, j), False)
        step(1, j + 2, 0, (j + 1, j + 1), False)
        return carry

    lax.fori_loop(0, t, two_blocks, 0)
    for p in pairs:
        update(1, 1, 2 * t, False, p)


def _mla_kernel(q_ref, k_ref, v_ref, o_ref, vt_ref, qbd_ref, s_ref, mx_ref, m_ref, acc_ref, *, nkv):
    t = pl.program_id(1)

    @pl.when(t == 0)
    def _():
        _transpose_values(v_ref, vt_ref, nkv, HEAD_L)

    for u in range(2):
        _stack_pair_queries(q_ref, qbd_ref.at[u], u * TQ)

    def stage(u, slot, j, diag, p):
        mask = (lambda h, s: jnp.where(_causal(False), s, NEG_INF)) if diag else (lambda h, s: s)
        _stage_pair(k_ref, qbd_ref.at[u], s_ref.at[slot, u], mx_ref.at[slot, u], j, p, mask)

    def update(u, slot, j, first, p):
        _update_pair(s_ref.at[slot, u], mx_ref.at[slot, u], vt_ref.at[j], m_ref.at[u], acc_ref.at[u],
                     p, first)

    _paired_sweep(t, stage, update)
    for u in range(2):
        _store_heads(o_ref, acc_ref.at[u], HEAD_L, u * TQ)


def _sb_kernel(q_ref, k_ref, v_ref, o_ref, vt_ref, qbd_ref, run_ref, acc_ref, *, nkv):
    qi = pl.program_id(1)

    @pl.when(qi == 0)
    def _():
        _transpose_values(v_ref, vt_ref, nkv, HEAD)

    _stack_pair_queries(q_ref, qbd_ref)

    def block(j, first):
        row0 = pl.multiple_of(j * TQ, TQ)
        suffix = jnp.where(_causal(False), 1.0, 0.0).astype(BF16)
        zs = [_pair_scores(k_ref, qbd_ref, row0, p) for p in range(N_HEADS // 2)]
        zs = [zs[h // 2][:, (h % 2) * TQ:(h % 2 + 1) * TQ] for h in range(N_HEADS)]
        incls = []
        for h in range(N_HEADS):
            z = zs[h]
            neg_abs = lax.bitcast_convert_type(
                lax.bitcast_convert_type(z, jnp.uint32) | jnp.uint32(0x80000000), F32)
            drop = jnp.maximum(z, 0.0) + jnp.log(1.0 + jnp.exp(neg_abs))
            if first:
                drop = jnp.where(_causal(True), drop, 0.0)
            hi = lax.bitcast_convert_type(
                lax.bitcast_convert_type(drop, jnp.uint32) & jnp.uint32(0xFFFF0000), F32)
            both = _dot(suffix, jnp.concatenate([hi.astype(BF16), (drop - hi).astype(BF16)], axis=1))
            incls.append(both[:, :TQ] + both[:, TQ:])
        for h in range(N_HEADS):
            rows = slice(h * HEAD, (h + 1) * HEAD)
            if first:
                w = jnp.where(_causal(True), jnp.exp(zs[h] - incls[h]), 0.0)
                run_ref[h:h + 1, :] = -incls[h][0:1, :]
                acc_ref[rows, :] = _dot(vt_ref[j, rows, :], w.astype(BF16))
            else:
                run = run_ref[h:h + 1, :]
                w = jnp.exp((zs[h] + run) - incls[h])
                run_ref[h:h + 1, :] = run - incls[h][0:1, :]
                acc_ref[rows, :] = acc_ref[rows, :] + _dot(vt_ref[j, rows, :], w.astype(BF16))

    def live(carry):
        i, top = carry
        return (i < qi) & (top > SB_DEAD)

    def body(carry):
        i, _ = carry
        block(qi - 1 - i, False)
        return i + 1, jnp.max(run_ref[...])

    block(qi, True)
    lax.while_loop(live, body, (jnp.int32(0), jnp.max(run_ref[...])))
    _store_heads(o_ref, acc_ref, HEAD)


def _moba_kernel(q_ref, k_ref, v_ref, o_ref, vt_ref, qbd_ref, s_ref, mx_ref, m_ref, acc_ref, km_ref,
                 sel_ref, *, nkv, nbp):
    t = pl.program_id(1)
    lane = lax.broadcasted_iota(jnp.int32, (1, W_BRANCH), 1)

    @pl.when(t == 0)
    def _():
        _transpose_values(v_ref, vt_ref, nkv, HEAD_L)
        km_ref[...] = jnp.zeros_like(km_ref)
        for j in range(nkv):
            mean = jnp.mean(k_ref[0, j * TQ:(j + 1) * TQ, :].astype(F32), axis=0, keepdims=True)
            for h in range(N_HEADS):
                in_head = (lane >= h * HEAD) & (lane < (h + 1) * HEAD)
                km_ref[h * nbp + j:h * nbp + j + 1, :] = jnp.where(in_head, mean, 0.0)

    blk = lax.broadcasted_iota(jnp.int32, (nbp, TQ), 0)
    for u in range(2):
        _stack_pair_queries(q_ref, qbd_ref.at[u], u * TQ)
        own = 2 * t + u
        gates = lax.dot_general(km_ref[...], q_ref[0, u * TQ:(u + 1) * TQ, :].astype(F32),
                                (((1,), (1,)), ((), ())), precision=lax.Precision.HIGHEST,
                                preferred_element_type=F32)
        for h in range(N_HEADS):
            gate = jnp.where(blk < own, gates[h * nbp:(h + 1) * nbp, :], NEG_INF)
            rank = jnp.zeros((nbp, TQ), jnp.int32)
            for j in range(nkv):
                gj = gate[j:j + 1, :]
                ahead = (gj > gate) | ((gj == gate) & (blk > j))
                rank = rank + jnp.where(ahead, 1, 0)
            sel = (rank < MOBA_TOPK) & (rank < own)
            sel_ref[u, h * nbp:(h + 1) * nbp, :] = jnp.where(sel, 1.0, 0.0)

    def stage(u, slot, j, diag, p):
        if diag:
            mask = lambda h, s: jnp.where(_causal(False), s, NEG_INF)
        else:
            mask = lambda h, s: jnp.where(sel_ref[u, pl.ds(h * nbp + j, 1), :] > 0.5, s, NEG_INF)
        _stage_pair(k_ref, qbd_ref.at[u], s_ref.at[slot, u], mx_ref.at[slot, u], j, p, mask)

    def update(u, slot, j, first, p):
        _update_pair(s_ref.at[slot, u], mx_ref.at[slot, u], vt_ref.at[j], m_ref.at[u], acc_ref.at[u],
                     p, first)

    _paired_sweep(t, stage, update)
    for u in range(2):
        _store_heads(o_ref, acc_ref.at[u], HEAD_L, u * TQ)


def _attention(kind, src, qk_width, v_col):
    b, s, _ = src.shape
    nkv = s // TQ
    pair_w = 2 * qk_width // N_HEADS
    if kind == "sb":
        tiles = 1
        rows = N_HEADS * HEAD
        body = functools.partial(_sb_kernel, nkv=nkv)
        scratch = [pltpu.VMEM((nkv, rows, TQ), BF16), pltpu.VMEM((N_HEADS // 2, pair_w, 2 * TQ), BF16),
                   pltpu.VMEM((N_HEADS, TQ), F32), pltpu.VMEM((rows, TQ), F32)]
    else:
        tiles = 2
        rows = N_HEADS * HEAD_L
        scratch = [pltpu.VMEM((nkv, rows, TQ), BF16),
                   pltpu.VMEM((tiles, N_HEADS // 2, pair_w, 2 * TQ), BF16),
                   pltpu.VMEM((2, tiles, N_HEADS // 2, TQ, 2 * TQ), F32),
                   pltpu.VMEM((2, tiles, N_HEADS, TQ), F32),
                   pltpu.VMEM((tiles, N_HEADS, TQ), F32),
                   pltpu.VMEM((tiles, rows, TQ), F32)]
        if kind == "mla":
            body = functools.partial(_mla_kernel, nkv=nkv)
        else:
            nbp = -(-nkv // 8) * 8
            body = functools.partial(_moba_kernel, nkv=nkv, nbp=nbp)
            scratch += [pltpu.VMEM((N_HEADS * nbp, W_BRANCH), F32),
                        pltpu.VMEM((tiles, N_HEADS * nbp, TQ), F32)]
    tq = tiles * TQ
    return pl.pallas_call(
        body,
        grid=(b, s // tq),
        in_specs=[
            pl.BlockSpec((1, tq, qk_width), lambda i, t: (i, t, 0)),
            pl.BlockSpec((1, s, qk_width), lambda i, t: (i, 0, 1)),
            pl.BlockSpec((1, s, W_BRANCH), lambda i, t: (i, 0, v_col)),
        ],
        out_specs=pl.BlockSpec((1, tq, W_BRANCH), lambda i, t: (i, t, 0)),
        out_shape=jax.ShapeDtypeStruct((b, s, W_BRANCH), BF16),
        scratch_shapes=scratch,
        compiler_params=pltpu.CompilerParams(
            dimension_semantics=("arbitrary", "arbitrary"), vmem_limit_bytes=VMEM_LIMIT),
        name="attn_" + kind,
    )(src, src, src)


def _merge_kernel(x_ref, gate1_ref, oa_ref, ob_ref, oc_ref, g_ref, wa_ref, wb_ref, wc_ref, wo_ref,
                  o_ref):
    merged = g_ref[0, :, :D_MODEL].astype(F32) * _dot(oa_ref[0], wa_ref[...])
    merged = merged + g_ref[0, :, D_MODEL:2 * D_MODEL].astype(F32) * _dot(ob_ref[0], wb_ref[...])
    merged = merged + g_ref[0, :, 2 * D_MODEL:].astype(F32) * _dot(oc_ref[0], wc_ref[...])
    o_ref[0] = x_ref[0] + gate1_ref[0] * _dot(merged.astype(BF16), wo_ref[...])


def _merge(x, gate1, oa, ob, oc, gates, wa, wb, wc, wo):
    b, s, d = x.shape
    tm = 512
    row = lambda w: pl.BlockSpec((1, tm, w), lambda i, j: (i, j, 0))
    vec = pl.BlockSpec((1, 1, d), lambda i, j: (i, 0, 0))
    return pl.pallas_call(
        _merge_kernel,
        grid=(b, s // tm),
        in_specs=[row(d), vec, row(W_BRANCH), row(W_BRANCH), row(W_BRANCH), row(3 * d),
                  _const_spec(wa.shape), _const_spec(wb.shape), _const_spec(wc.shape),
                  _const_spec(wo.shape)],
        out_specs=row(d),
        out_shape=jax.ShapeDtypeStruct((b, s, d), F32),
        compiler_params=pltpu.CompilerParams(
            dimension_semantics=("arbitrary", "arbitrary"), vmem_limit_bytes=VMEM_LIMIT),
        name="merge",
    )(x, gate1, oa, ob, oc, gates, wa, wb, wc, wo)


def _ffn_kernel(x_ref, shift_ref, scale_ref, gate_ref, g2_ref, w1_ref, w2_ref, gf_ref, o_ref, *,
                final_norm):
    x = x_ref[0]
    h = (_rms(x, g2_ref[...]) * (1.0 + scale_ref[0]) + shift_ref[0]).astype(BF16)
    acc = jnp.zeros_like(x)
    chunk = D_MODEL
    for c in range(D_FF // chunk):
        u = jnp.maximum(_dot(h, w1_ref[:, c * chunk:(c + 1) * chunk]), 0.0)
        acc = acc + _dot((u * u).astype(BF16), w2_ref[c * chunk:(c + 1) * chunk, :])
    y = x + gate_ref[0] * acc
    if final_norm:
        y = _rms(y, gf_ref[...])
    o_ref[0] = y


def _ffn(x, shift, scale, gate, g2, w1, w2, gf, final_norm):
    b, s, d = x.shape
    tm = 512
    row = pl.BlockSpec((1, tm, d), lambda i, j: (i, j, 0))
    vec = pl.BlockSpec((1, 1, d), lambda i, j: (i, 0, 0))
    return pl.pallas_call(
        functools.partial(_ffn_kernel, final_norm=final_norm),
        grid=(b, s // tm),
        in_specs=[row, vec, vec, vec, _const_spec((1, d)), _const_spec(w1.shape),
                  _const_spec(w2.shape), _const_spec((1, d))],
        out_specs=row,
        out_shape=jax.ShapeDtypeStruct((b, s, d), F32),
        compiler_params=pltpu.CompilerParams(
            dimension_semantics=("arbitrary", "arbitrary"), vmem_limit_bytes=VMEM_LIMIT),
        name="ffn",
    )(x, shift, scale, gate, g2, w1, w2, gf)


def _rope_table(positions):
    pos = positions.astype(F32)[..., None]
    parts = []
    for rot_dim in (MLA_ROPE, MOBA_ROT):
        inv = ROPE_THETA ** (-jnp.arange(0, rot_dim, 2, dtype=F32) / rot_dim)
        parts += [jnp.cos(pos * inv), jnp.sin(pos * inv)]
    used = MLA_ROPE + MOBA_ROT
    return jnp.pad(jnp.concatenate(parts, axis=-1), ((0, 0), (0, 0), (0, LANE - used)))


def _layer_weights(w_in, w_uq, w_ukv):
    wlat = w_in[:, :C_KPE].astype(BF16)
    wkpe = jnp.pad(w_in[:, C_KPE:C_REST], ((0, 0), (MLA_NOPE, LANE - MLA_QK))).astype(BF16)
    wrest = w_in[:, C_REST:].astype(BF16)
    wuq = jnp.pad(w_uq.reshape(MLA_Q_RANK, N_HEADS, MLA_QK),
                  ((0, 0), (0, 0), (0, LANE - MLA_QK))).reshape(MLA_Q_RANK, N_HEADS * LANE)
    ukv = w_ukv.reshape(MLA_KV_RANK, N_HEADS, MLA_NOPE + MLA_V)
    wk = jnp.pad(ukv[:, :, :MLA_NOPE], ((0, 0), (0, 0), (0, LANE - MLA_NOPE)))
    wukv = jnp.concatenate([wk.reshape(MLA_KV_RANK, N_HEADS * LANE),
                            ukv[:, :, MLA_NOPE:].reshape(MLA_KV_RANK, N_HEADS * MLA_V)], axis=1)
    return wlat, wkpe, wrest, wuq.astype(BF16), wukv.astype(BF16)


def kernel(x, c, positions, w_ada, b_ada, norm1_g, norm2_g, w_in, q_norm_g, w_uq, kv_norm_g, w_ukv,
           w_o_mla, w_o_sb, w_o_moba, w_out, w_ff1, w_ff2, final_norm_g):
    b, s, d = x.shape
    depth = w_ada.shape[0]
    assert d == D_MODEL and s % 512 == 0 and w_in.shape[-1] == D_IN
    rope = _rope_table(positions)
    mod = _adaln_mod(c, w_ada, b_ada)
    gf = final_norm_g.reshape(1, d)
    for l in range(depth):
        shift1, scale1, gate1, shift2, scale2, gate2 = [
            m.reshape(b, 1, d) for m in jnp.split(mod[l], 6, axis=-1)]
        wlat, wkpe, wrest, wuq, wukv = _layer_weights(w_in[l], w_uq[l], w_ukv[l])
        mla, sb, mb, gates = _inproj(
            x, shift1, scale1, norm1_g[l].reshape(1, d), wlat, wkpe, wrest,
            q_norm_g[l].reshape(1, -1), wuq, kv_norm_g[l].reshape(1, -1), wukv, rope)
        o_mla = _attention("mla", mla, N_HEADS * LANE, 4)
        o_sb = _attention("sb", sb, W_BRANCH, 2)
        o_mb = _attention("moba", mb, W_BRANCH, 2)
        x = _merge(x, gate1, o_mla, o_sb, o_mb, gates, w_o_mla[l].astype(BF16),
                   w_o_sb[l].astype(BF16), w_o_moba[l].astype(BF16), w_out[l].astype(BF16))
        x = _ffn(x, shift2, scale2, gate2, norm2_g[l].reshape(1, d), w_ff1[l].astype(BF16),
                 w_ff2[l].astype(BF16), gf, l == depth - 1)
    return x
```

```python
import functools
import math

import jax
import jax.numpy as jnp
from jax import lax
from jax.experimental import pallas as pl
from jax.experimental.pallas import tpu as pltpu

F32 = jnp.float32
BF16 = jnp.bfloat16

D_MODEL = 1024
N_HEADS = 8
MLA_NOPE = 64
MLA_ROPE = 32
MLA_V = 64
MLA_QK = MLA_NOPE + MLA_ROPE
MLA_Q_RANK = 768
MLA_KV_RANK = 256
HEAD = 64
W_BRANCH = N_HEADS * HEAD
MOBA_BLOCK = 256
MOBA_TOPK = 3
MOBA_ROT = HEAD // 4
ROPE_THETA = 500000.0
D_FF = 4 * D_MODEL
EPS = 1e-6
NEG_INF = -1e30
SB_DEAD = -104.0
LOG2E = 1.4426950408889634

LANE = 128
TQ = MOBA_BLOCK
TM_DENSE = 2 * TQ
TN_MOD = 1536
BF16_ROWS = 16
HEAD_L = HEAD + BF16_ROWS
VMEM_LIMIT = 56 * 1024 * 1024

C_KPE = MLA_Q_RANK + MLA_KV_RANK
C_REST = C_KPE + MLA_ROPE
R_MB = 3 * W_BRANCH
R_GATE = R_MB + 3 * W_BRANCH
D_IN = C_REST + R_GATE + 3 * D_MODEL
W_MLA_OUT = 2 * N_HEADS * LANE + N_HEADS * MLA_V


def _rms(xf, g):
    return xf * lax.rsqrt(jnp.mean(xf * xf, axis=-1, keepdims=True) + EPS) * g


def _dot(a, b):
    return jnp.dot(a, b, preferred_element_type=F32)


def _rope_lanes(xb, cos, sin, half, first_half):
    rot = jnp.where(first_half, pltpu.roll(xb, LANE - half, 1), pltpu.roll(xb, half, 1))
    return xb * cos + rot * sin


def _mod_kernel(c_ref, w_ref, b_ref, o_ref):
    c = c_ref[...]
    c_act = c * (1.0 / (1.0 + jnp.exp(-c)))
    o_ref[0] = _dot(c_act.astype(BF16), w_ref[0].astype(BF16)) + b_ref[0]


def _expand_rope(t, lane):
    def put(table, lo, width, src):
        moved = pltpu.roll(t, (lo - src) % LANE, 1)
        return jnp.where((lane >= lo) & (lane < lo + width), moved, table)

    h1, h2 = MLA_ROPE // 2, MOBA_ROT // 2
    one, zero = jnp.ones_like(t), jnp.zeros_like(t)
    x1 = MLA_NOPE
    cosa = put(put(one, x1, h1, 0), x1 + h1, h1, 0)
    sina = put(zero, x1 + h1, h1, h1) - put(zero, x1, h1, h1)
    cosm, sinp, sinn = one, zero, zero
    for head in range(LANE // HEAD):
        cosm = put(put(cosm, head * HEAD, h2, 2 * h1), head * HEAD + h2, h2, 2 * h1)
        sinn = put(sinn, head * HEAD, h2, 2 * h1 + h2)
        sinp = put(sinp, head * HEAD + h2, h2, 2 * h1 + h2)
    return cosa, sina, cosm, sinp - sinn


def _adaln_mod(c, w_ada, b_ada):
    depth, d, n = w_ada.shape
    b = c.shape[0]
    tn = TN_MOD
    return pl.pallas_call(
        _mod_kernel,
        grid=(depth, n // tn),
        in_specs=[
            pl.BlockSpec((b, d), lambda l, j: (0, 0)),
            pl.BlockSpec((1, d, tn), lambda l, j: (l, 0, j)),
            pl.BlockSpec((1, 1, tn), lambda l, j: (l, 0, j)),
        ],
        out_specs=pl.BlockSpec((1, b, tn), lambda l, j: (l, 0, j)),
        out_shape=jax.ShapeDtypeStruct((depth, b, n), F32),
        compiler_params=pltpu.CompilerParams(
            dimension_semantics=("arbitrary", "arbitrary"), vmem_limit_bytes=VMEM_LIMIT),
        name="adaln_mod",
    )(c, w_ada, b_ada.reshape(depth, 1, n))


def _inproj_kernel(x_ref, shift_ref, scale_ref, g1_ref, wlat_ref, wkpe_ref, w_ref, gq_ref, wuq_ref,
                   gkv_ref, wukv_ref, rope_ref, mla_ref, sb_ref, mb_ref, gate_ref):
    x = x_ref[0]
    h = _rms(x, g1_ref[...]) * (1.0 + scale_ref[0]) + shift_ref[0]
    h = h.astype(BF16)
    lane = lax.broadcasted_iota(jnp.int32, (1, LANE), 1)

    cosa, sina, cosm, sinm = _expand_rope(rope_ref[0], lane)
    mla_first = lane < MLA_NOPE + MLA_ROPE // 2
    mb_first = (lane % HEAD) < MOBA_ROT // 2
    mla_scale = LOG2E / math.sqrt(MLA_QK)
    sb_scale = 1.0 / math.sqrt(HEAD)

    qlat = _dot(h, wlat_ref[:, :MLA_Q_RANK])
    ckv = _dot(h, wlat_ref[:, MLA_Q_RANK:])
    kpe = _dot(h, wkpe_ref[...])

    sb_ref[0, :, :W_BRANCH] = (_dot(h, w_ref[:, :W_BRANCH]) * sb_scale).astype(BF16)
    sb_ref[0, :, W_BRANCH:] = _dot(h, w_ref[:, W_BRANCH:R_MB]).astype(BF16)

    q = _dot(_rms(qlat, gq_ref[...]).astype(BF16), wuq_ref[...])
    kv = _dot(_rms(ckv, gkv_ref[...]).astype(BF16), wukv_ref[...])

    for part, scale in ((0, sb_scale * LOG2E), (1, 1.0)):
        c0 = R_MB + part * W_BRANCH
        acc = _dot(h, w_ref[:, c0:c0 + W_BRANCH])
        for cb in range(W_BRANCH // LANE):
            blk = _rope_lanes(acc[:, cb * LANE:(cb + 1) * LANE], cosm, sinm, MOBA_ROT // 2, mb_first)
            mb_ref[0, :, part * W_BRANCH + cb * LANE:part * W_BRANCH + (cb + 1) * LANE] = (
                blk * scale).astype(BF16)
    mb_ref[0, :, 2 * W_BRANCH:] = _dot(h, w_ref[:, R_MB + 2 * W_BRANCH:R_GATE]).astype(BF16)

    kpe = _rope_lanes(kpe, cosa, sina, MLA_ROPE // 2, mla_first)
    k_off = N_HEADS * LANE
    for hd in range(N_HEADS):
        qb = _rope_lanes(q[:, hd * LANE:(hd + 1) * LANE], cosa, sina, MLA_ROPE // 2, mla_first)
        mla_ref[0, :, hd * LANE:(hd + 1) * LANE] = (qb * mla_scale).astype(BF16)
        kb = kv[:, hd * LANE:(hd + 1) * LANE] + kpe
        mla_ref[0, :, k_off + hd * LANE:k_off + (hd + 1) * LANE] = kb.astype(BF16)
    mla_ref[0, :, 2 * k_off:] = kv[:, k_off:].astype(BF16)

    for br in range(3):
        c0 = R_GATE + br * D_MODEL
        gl = _dot(h, w_ref[:, c0:c0 + D_MODEL])
        gate_ref[0, :, br * D_MODEL:(br + 1) * D_MODEL] = (1.0 / (1.0 + jnp.exp(-gl))).astype(BF16)


def _const_spec(shape):
    nd = len(shape)
    return pl.BlockSpec(shape, lambda *_: (0,) * nd, pipeline_mode=pl.Buffered(1))


def _inproj(x, shift, scale, g1, wlat, wkpe, wrest, gq, wuq, gkv, wukv, rope):
    b, s, d = x.shape
    tm = TQ
    row = lambda w: pl.BlockSpec((1, tm, w), lambda i, j: (i, j, 0))
    vec = pl.BlockSpec((1, 1, d), lambda i, j: (i, 0, 0))
    return pl.pallas_call(
        _inproj_kernel,
        grid=(b, s // tm),
        in_specs=[row(d), vec, vec, _const_spec((1, d)), _const_spec(wlat.shape),
                  _const_spec(wkpe.shape), _const_spec(wrest.shape), _const_spec(gq.shape),
                  _const_spec(wuq.shape), _const_spec(gkv.shape), _const_spec(wukv.shape),
                  row(LANE)],
        out_specs=[row(W_MLA_OUT), row(3 * W_BRANCH), row(3 * W_BRANCH), row(3 * D_MODEL)],
        out_shape=[jax.ShapeDtypeStruct((b, s, W_MLA_OUT), BF16),
                   jax.ShapeDtypeStruct((b, s, 3 * W_BRANCH), BF16),
                   jax.ShapeDtypeStruct((b, s, 3 * W_BRANCH), BF16),
                   jax.ShapeDtypeStruct((b, s, 3 * D_MODEL), BF16)],
        compiler_params=pltpu.CompilerParams(
            dimension_semantics=("arbitrary", "arbitrary"), vmem_limit_bytes=VMEM_LIMIT),
        name="inproj",
    )(x, shift, scale, g1, wlat, wkpe, wrest, gq, wuq, gkv, wukv, rope)


def _transpose_values(v_ref, vt_ref, nkv, stride):
    for j in range(nkv):
        for c in range(W_BRANCH // LANE):
            blk = v_ref[0, j * TQ:(j + 1) * TQ, c * LANE:(c + 1) * LANE].astype(F32).T.astype(BF16)
            for a in range(2):
                h = 2 * c + a
                vt_ref[j, h * stride:h * stride + HEAD, :] = blk[a * HEAD:(a + 1) * HEAD, :]
                if stride > HEAD:
                    vt_ref[j, h * stride + HEAD:(h + 1) * stride, :] = jnp.ones((stride - HEAD, TQ), BF16)


def _store_heads(o_ref, acc_ref, stride, row0=0):
    for c in range(W_BRANCH // LANE):
        parts = []
        for h in (2 * c, 2 * c + 1):
            part = acc_ref[h * stride:h * stride + HEAD, :]
            if stride > HEAD:
                part = part / acc_ref[h * stride + HEAD:h * stride + HEAD + 1, :]
            parts.append(part)
        o_ref[0, row0:row0 + TQ, c * LANE:(c + 1) * LANE] = (
            jnp.concatenate(parts, axis=0).T.astype(BF16))


def _stack_pair_queries(q_ref, qbd_ref, row0=0):
    pairs, w, _ = qbd_ref.shape
    zero = jnp.zeros((w // 2, TQ), BF16)
    for p in range(pairs):
        qt = q_ref[0, row0:row0 + TQ, p * w:(p + 1) * w].astype(F32).T.astype(BF16)
        qbd_ref[p, :w // 2, :TQ] = qt[:w // 2, :]
        qbd_ref[p, :w // 2, TQ:] = zero
        qbd_ref[p, w // 2:, :TQ] = zero
        qbd_ref[p, w // 2:, TQ:] = qt[w // 2:, :]


def _pair_scores(k_ref, qbd_ref, row0, p):
    w = qbd_ref.shape[1]
    return _dot(k_ref[0, pl.ds(row0, TQ), p * w:(p + 1) * w], qbd_ref[p])


def _causal(strict):
    key = lax.broadcasted_iota(jnp.int32, (TQ, TQ), 0)
    qry = lax.broadcasted_iota(jnp.int32, (TQ, TQ), 1)
    return key < qry if strict else key <= qry


def _stage_pair(k_ref, qbd_ref, s_ref, mx_ref, j, p, mask):
    pair = _pair_scores(k_ref, qbd_ref, pl.multiple_of(j * TQ, TQ), p)
    for a in range(2):
        h = 2 * p + a
        s = mask(h, pair[:, a * TQ:(a + 1) * TQ])
        s_ref[p, :, a * TQ:(a + 1) * TQ] = s
        mx_ref[h:h + 1, :] = jnp.max(s, axis=0, keepdims=True)


def _update_pair(s_ref, mx_ref, vt_blk, m_ref, acc_ref, p, first):
    for a in range(2):
        h = 2 * p + a
        rows = slice(h * HEAD_L, (h + 1) * HEAD_L)
        m_new = mx_ref[h:h + 1, :]
        if not first:
            m_old = m_ref[h:h + 1, :]
            m_new = jnp.maximum(m_old, m_new)
            alpha = jnp.exp2(m_old - m_new)
        pr = jnp.exp2(s_ref[p, :, a * TQ:(a + 1) * TQ] - m_new)
        m_ref[h:h + 1, :] = m_new
        o = _dot(vt_blk[rows, :], pr.astype(BF16))
        acc_ref[rows, :] = o if first else alpha * acc_ref[rows, :] + o


def _paired_sweep(t, stage, update):
    pairs = range(N_HEADS // 2)
    tiles = (0, 1)

    def step(nxt_slot, nxt_j, cur_slot, cur_j, first):
        for p in pairs:
            for u in tiles:
                stage(u, nxt_slot, nxt_j, False, p)
            for u in tiles:
                update(u, cur_slot, cur_j[u], first, p)

    for p in pairs:
        for u in tiles:
            stage(u, 0, 2 * t + u, True, p)
    step(1, 0, 0, (2 * t, 2 * t + 1), True)

    def two_blocks(i, carry):
        j = 2 * i
        step(0, j + 1, 1, (j, j), False)
        step(1, j + 2, 0, (j + 1, j + 1), False)
        return carry

    lax.fori_loop(0, t, two_blocks, 0)
    for p in pairs:
        update(1, 1, 2 * t, False, p)


def _mla_kernel(q_ref, k_ref, v_ref, o_ref, vt_ref, qbd_ref, s_ref, mx_ref, m_ref, acc_ref, *, nkv):
    t = pl.program_id(1)

    @pl.when(t == 0)
    def _():
        _transpose_values(v_ref, vt_ref, nkv, HEAD_L)

    for u in range(2):
        _stack_pair_queries(q_ref, qbd_ref.at[u], u * TQ)

    def stage(u, slot, j, diag, p):
        mask = (lambda h, s: jnp.where(_causal(False), s, NEG_INF)) if diag else (lambda h, s: s)
        _stage_pair(k_ref, qbd_ref.at[u], s_ref.at[slot, u], mx_ref.at[slot, u], j, p, mask)

    def update(u, slot, j, first, p):
        _update_pair(s_ref.at[slot, u], mx_ref.at[slot, u], vt_ref.at[j], m_ref.at[u], acc_ref.at[u],
                     p, first)

    _paired_sweep(t, stage, update)
    for u in range(2):
        _store_heads(o_ref, acc_ref.at[u], HEAD_L, u * TQ)


def _sb_kernel(q_ref, k_ref, v_ref, o_ref, vt_ref, qbd_ref, run_ref, acc_ref, *, nkv):
    qi = pl.program_id(1)

    @pl.when(qi == 0)
    def _():
        _transpose_values(v_ref, vt_ref, nkv, HEAD)

    _stack_pair_queries(q_ref, qbd_ref)

    def block(j, first):
        row0 = pl.multiple_of(j * TQ, TQ)
        suffix = jnp.where(_causal(False), 1.0, 0.0).astype(BF16)
        zs = [_pair_scores(k_ref, qbd_ref, row0, p) for p in range(N_HEADS // 2)]
        zs = [zs[h // 2][:, (h % 2) * TQ:(h % 2 + 1) * TQ] for h in range(N_HEADS)]
        incls = []
        for h in range(N_HEADS):
            z = zs[h]
            neg_abs = lax.bitcast_convert_type(
                lax.bitcast_convert_type(z, jnp.uint32) | jnp.uint32(0x80000000), F32)
            drop = jnp.maximum(z, 0.0) + jnp.log(1.0 + jnp.exp(neg_abs))
            if first:
                drop = jnp.where(_causal(True), drop, 0.0)
            hi = lax.bitcast_convert_type(
                lax.bitcast_convert_type(drop, jnp.uint32) & jnp.uint32(0xFFFF0000), F32)
            both = _dot(suffix, jnp.concatenate([hi.astype(BF16), (drop - hi).astype(BF16)], axis=1))
            incls.append(both[:, :TQ] + both[:, TQ:])
        for h in range(N_HEADS):
            rows = slice(h * HEAD, (h + 1) * HEAD)
            if first:
                w = jnp.where(_causal(True), jnp.exp(zs[h] - incls[h]), 0.0)
                run_ref[h:h + 1, :] = -incls[h][0:1, :]
                acc_ref[rows, :] = _dot(vt_ref[j, rows, :], w.astype(BF16))
            else:
                run = run_ref[h:h + 1, :]
                w = jnp.exp((zs[h] + run) - incls[h])
                run_ref[h:h + 1, :] = run - incls[h][0:1, :]
                acc_ref[rows, :] = acc_ref[rows, :] + _dot(vt_ref[j, rows, :], w.astype(BF16))

    def live(carry):
        i, top = carry
        return (i < qi) & (top > SB_DEAD)

    def body(carry):
        i, _ = carry
        block(qi - 1 - i, False)
        return i + 1, jnp.max(run_ref[...])

    block(qi, True)
    lax.while_loop(live, body, (jnp.int32(0), jnp.max(run_ref[...])))
    _store_heads(o_ref, acc_ref, HEAD)


def _top_blocks(gate, blk, own):
    taken = -3.0e38
    sel = jnp.zeros(gate.shape, F32)
    for r in range(MOBA_TOPK):
        best = jnp.max(gate, axis=0, keepdims=True)
        first = jnp.min(jnp.where(gate == best, blk, gate.shape[0]), axis=0, keepdims=True)
        chosen = blk == first
        sel = jnp.where(chosen, jnp.where(r < own, 1.0, 0.0), sel)
        gate = jnp.where(chosen, taken, gate)
    return sel


def _moba_kernel(q_ref, k_ref, v_ref, o_ref, vt_ref, qbd_ref, s_ref, mx_ref, m_ref, acc_ref, km_ref,
                 sel_ref, *, nkv, nbp):
    t = pl.program_id(1)
    lane = lax.broadcasted_iota(jnp.int32, (1, W_BRANCH), 1)

    @pl.when(t == 0)
    def _():
        _transpose_values(v_ref, vt_ref, nkv, HEAD_L)
        km_ref[...] = jnp.zeros_like(km_ref)
        for j in range(nkv):
            mean = jnp.mean(k_ref[0, j * TQ:(j + 1) * TQ, :].astype(F32), axis=0, keepdims=True)
            for h in range(N_HEADS):
                in_head = (lane >= h * HEAD) & (lane < (h + 1) * HEAD)
                km_ref[h * nbp + j:h * nbp + j + 1, :] = jnp.where(in_head, mean, 0.0)

    blk = lax.broadcasted_iota(jnp.int32, (nbp, TQ), 0)
    for u in range(2):
        _stack_pair_queries(q_ref, qbd_ref.at[u], u * TQ)
        own = 2 * t + u
        gates = lax.dot_general(km_ref[...], q_ref[0, u * TQ:(u + 1) * TQ, :].astype(F32),
                                (((1,), (1,)), ((), ())), precision=lax.Precision.HIGHEST,
                                preferred_element_type=F32)
        for h in range(N_HEADS):
            gate = jnp.where(blk < own, gates[h * nbp:(h + 1) * nbp, :], NEG_INF)
            sel_ref[u, h * nbp:(h + 1) * nbp, :] = _top_blocks(gate, blk, own)

    def stage(u, slot, j, diag, p):
        if diag:
            mask = lambda h, s: jnp.where(_causal(False), s, NEG_INF)
        else:
            mask = lambda h, s: jnp.where(sel_ref[u, pl.ds(h * nbp + j, 1), :] > 0.5, s, NEG_INF)
        _stage_pair(k_ref, qbd_ref.at[u], s_ref.at[slot, u], mx_ref.at[slot, u], j, p, mask)

    def update(u, slot, j, first, p):
        _update_pair(s_ref.at[slot, u], mx_ref.at[slot, u], vt_ref.at[j], m_ref.at[u], acc_ref.at[u],
                     p, first)

    _paired_sweep(t, stage, update)
    for u in range(2):
        _store_heads(o_ref, acc_ref.at[u], HEAD_L, u * TQ)


def _attention(kind, src, qk_width, v_col):
    b, s, _ = src.shape
    nkv = s // TQ
    pair_w = 2 * qk_width // N_HEADS
    if kind == "sb":
        tiles = 1
        rows = N_HEADS * HEAD
        body = functools.partial(_sb_kernel, nkv=nkv)
        scratch = [pltpu.VMEM((nkv, rows, TQ), BF16), pltpu.VMEM((N_HEADS // 2, pair_w, 2 * TQ), BF16),
                   pltpu.VMEM((N_HEADS, TQ), F32), pltpu.VMEM((rows, TQ), F32)]
    else:
        tiles = 2
        rows = N_HEADS * HEAD_L
        scratch = [pltpu.VMEM((nkv, rows, TQ), BF16),
                   pltpu.VMEM((tiles, N_HEADS // 2, pair_w, 2 * TQ), BF16),
                   pltpu.VMEM((2, tiles, N_HEADS // 2, TQ, 2 * TQ), F32),
                   pltpu.VMEM((2, tiles, N_HEADS, TQ), F32),
                   pltpu.VMEM((tiles, N_HEADS, TQ), F32),
                   pltpu.VMEM((tiles, rows, TQ), F32)]
        if kind == "mla":
            body = functools.partial(_mla_kernel, nkv=nkv)
        else:
            nbp = -(-nkv // 8) * 8
            body = functools.partial(_moba_kernel, nkv=nkv, nbp=nbp)
            scratch += [pltpu.VMEM((N_HEADS * nbp, W_BRANCH), F32),
                        pltpu.VMEM((tiles, N_HEADS * nbp, TQ), F32)]
    tq = tiles * TQ
    return pl.pallas_call(
        body,
        grid=(b, s // tq),
        in_specs=[
            pl.BlockSpec((1, tq, qk_width), lambda i, t: (i, t, 0)),
            pl.BlockSpec((1, s, qk_width), lambda i, t: (i, 0, 1)),
            pl.BlockSpec((1, s, W_BRANCH), lambda i, t: (i, 0, v_col)),
        ],
        out_specs=pl.BlockSpec((1, tq, W_BRANCH), lambda i, t: (i, t, 0)),
        out_shape=jax.ShapeDtypeStruct((b, s, W_BRANCH), BF16),
        scratch_shapes=scratch,
        compiler_params=pltpu.CompilerParams(
            dimension_semantics=("arbitrary", "arbitrary"), vmem_limit_bytes=VMEM_LIMIT),
        name="attn_" + kind,
    )(src, src, src)


def _merge_kernel(x_ref, gate1_ref, oa_ref, ob_ref, oc_ref, g_ref, wa_ref, wb_ref, wc_ref, wo_ref,
                  o_ref):
    merged = g_ref[0, :, :D_MODEL].astype(F32) * _dot(oa_ref[0], wa_ref[...])
    merged = merged + g_ref[0, :, D_MODEL:2 * D_MODEL].astype(F32) * _dot(ob_ref[0], wb_ref[...])
    merged = merged + g_ref[0, :, 2 * D_MODEL:].astype(F32) * _dot(oc_ref[0], wc_ref[...])
    o_ref[0] = x_ref[0] + gate1_ref[0] * _dot(merged.astype(BF16), wo_ref[...])


def _merge(x, gate1, oa, ob, oc, gates, wa, wb, wc, wo):
    b, s, d = x.shape
    tm = TM_DENSE
    row = lambda w: pl.BlockSpec((1, tm, w), lambda i, j: (i, j, 0))
    vec = pl.BlockSpec((1, 1, d), lambda i, j: (i, 0, 0))
    return pl.pallas_call(
        _merge_kernel,
        grid=(b, s // tm),
        in_specs=[row(d), vec, row(W_BRANCH), row(W_BRANCH), row(W_BRANCH), row(3 * d),
                  _const_spec(wa.shape), _const_spec(wb.shape), _const_spec(wc.shape),
                  _const_spec(wo.shape)],
        out_specs=row(d),
        out_shape=jax.ShapeDtypeStruct((b, s, d), F32),
        compiler_params=pltpu.CompilerParams(
            dimension_semantics=("arbitrary", "arbitrary"), vmem_limit_bytes=VMEM_LIMIT),
        name="merge",
    )(x, gate1, oa, ob, oc, gates, wa, wb, wc, wo)


def _ffn_kernel(x_ref, shift_ref, scale_ref, gate_ref, g2_ref, w1_ref, w2_ref, gf_ref, o_ref, *,
                final_norm):
    x = x_ref[0]
    h = (_rms(x, g2_ref[...]) * (1.0 + scale_ref[0]) + shift_ref[0]).astype(BF16)
    acc = jnp.zeros_like(x)
    chunk = D_MODEL
    for c in range(D_FF // chunk):
        u = jnp.maximum(_dot(h, w1_ref[:, c * chunk:(c + 1) * chunk]), 0.0)
        acc = acc + _dot((u * u).astype(BF16), w2_ref[c * chunk:(c + 1) * chunk, :])
    y = x + gate_ref[0] * acc
    if final_norm:
        y = _rms(y, gf_ref[...])
    o_ref[0] = y


def _ffn(x, shift, scale, gate, g2, w1, w2, gf, final_norm):
    b, s, d = x.shape
    tm = TM_DENSE
    row = pl.BlockSpec((1, tm, d), lambda i, j: (i, j, 0))
    vec = pl.BlockSpec((1, 1, d), lambda i, j: (i, 0, 0))
    return pl.pallas_call(
        functools.partial(_ffn_kernel, final_norm=final_norm),
        grid=(b, s // tm),
        in_specs=[row, vec, vec, vec, _const_spec((1, d)), _const_spec(w1.shape),
                  _const_spec(w2.shape), _const_spec((1, d))],
        out_specs=row,
        out_shape=jax.ShapeDtypeStruct((b, s, d), F32),
        compiler_params=pltpu.CompilerParams(
            dimension_semantics=("arbitrary", "arbitrary"), vmem_limit_bytes=VMEM_LIMIT),
        name="ffn",
    )(x, shift, scale, gate, g2, w1, w2, gf)


def _rope_table(positions):
    pos = positions.astype(F32)[..., None]
    parts = []
    for rot_dim in (MLA_ROPE, MOBA_ROT):
        inv = ROPE_THETA ** (-jnp.arange(0, rot_dim, 2, dtype=F32) / rot_dim)
        parts += [jnp.cos(pos * inv), jnp.sin(pos * inv)]
    used = MLA_ROPE + MOBA_ROT
    return jnp.pad(jnp.concatenate(parts, axis=-1), ((0, 0), (0, 0), (0, LANE - used)))


def _layer_weights(w_in, w_uq, w_ukv):
    wlat = w_in[:, :C_KPE].astype(BF16)
    wkpe = jnp.pad(w_in[:, C_KPE:C_REST], ((0, 0), (MLA_NOPE, LANE - MLA_QK))).astype(BF16)
    wrest = w_in[:, C_REST:].astype(BF16)
    wuq = jnp.pad(w_uq.reshape(MLA_Q_RANK, N_HEADS, MLA_QK),
                  ((0, 0), (0, 0), (0, LANE - MLA_QK))).reshape(MLA_Q_RANK, N_HEADS * LANE)
    ukv = w_ukv.reshape(MLA_KV_RANK, N_HEADS, MLA_NOPE + MLA_V)
    wk = jnp.pad(ukv[:, :, :MLA_NOPE], ((0, 0), (0, 0), (0, LANE - MLA_NOPE)))
    wukv = jnp.concatenate([wk.reshape(MLA_KV_RANK, N_HEADS * LANE),
                            ukv[:, :, MLA_NOPE:].reshape(MLA_KV_RANK, N_HEADS * MLA_V)], axis=1)
    return wlat, wkpe, wrest, wuq.astype(BF16), wukv.astype(BF16)


def kernel(x, c, positions, w_ada, b_ada, norm1_g, norm2_g, w_in, q_norm_g, w_uq, kv_norm_g, w_ukv,
           w_o_mla, w_o_sb, w_o_moba, w_out, w_ff1, w_ff2, final_norm_g):
    b, s, d = x.shape
    depth = w_ada.shape[0]
    assert d == D_MODEL and s % TM_DENSE == 0 and w_in.shape[-1] == D_IN
    rope = _rope_table(positions)
    mod = _adaln_mod(c, w_ada, b_ada)
    gf = final_norm_g.reshape(1, d)
    for l in range(depth):
        shift1, scale1, gate1, shift2, scale2, gate2 = [
            m.reshape(b, 1, d) for m in jnp.split(mod[l], 6, axis=-1)]
        wlat, wkpe, wrest, wuq, wukv = _layer_weights(w_in[l], w_uq[l], w_ukv[l])
        mla, sb, mb, gates = _inproj(
            x, shift1, scale1, norm1_g[l].reshape(1, d), wlat, wkpe, wrest,
            q_norm_g[l].reshape(1, -1), wuq, kv_norm_g[l].reshape(1, -1), wukv, rope)
        o_mla = _attention("mla", mla, N_HEADS * LANE, 4)
        o_sb = _attention("sb", sb, W_BRANCH, 2)
        o_mb = _attention("moba", mb, W_BRANCH, 2)
        x = _merge(x, gate1, o_mla, o_sb, o_mb, gates, w_o_mla[l].astype(BF16),
                   w_o_sb[l].astype(BF16), w_o_moba[l].astype(BF16), w_out[l].astype(BF16))
        x = _ffn(x, shift2, scale2, gate2, norm2_g[l].reshape(1, d), w_ff1[l].astype(BF16),
                 w_ff2[l].astype(BF16), gf, l == depth - 1)
    return x
```

```python
import functools
import math

import jax
import jax.numpy as jnp
from jax import lax
from jax.experimental import pallas as pl
from jax.experimental.pallas import tpu as pltpu

F32 = jnp.float32
BF16 = jnp.bfloat16

D_MODEL = 1024
N_HEADS = 8
MLA_NOPE = 64
MLA_ROPE = 32
MLA_V = 64
MLA_QK = MLA_NOPE + MLA_ROPE
MLA_Q_RANK = 768
MLA_KV_RANK = 256
HEAD = 64
W_BRANCH = N_HEADS * HEAD
MOBA_BLOCK = 256
MOBA_TOPK = 3
MOBA_ROT = HEAD // 4
ROPE_THETA = 500000.0
D_FF = 4 * D_MODEL
EPS = 1e-6
NEG_INF = -1e30
SB_DEAD = -104.0
LOG2E = 1.4426950408889634

LANE = 128
TQ = MOBA_BLOCK
TM_DENSE = 2 * TQ
TN_MOD = 1536
BF16_ROWS = 16
HEAD_L = HEAD + BF16_ROWS
VMEM_LIMIT = 56 * 1024 * 1024

C_KPE = MLA_Q_RANK + MLA_KV_RANK
C_REST = C_KPE + MLA_ROPE
R_MB = 3 * W_BRANCH
R_GATE = R_MB + 3 * W_BRANCH
D_IN = C_REST + R_GATE + 3 * D_MODEL
W_MLA_OUT = 2 * N_HEADS * LANE + N_HEADS * MLA_V


def _rms(xf, g):
    return xf * lax.rsqrt(jnp.mean(xf * xf, axis=-1, keepdims=True) + EPS) * g


def _dot(a, b):
    return jnp.dot(a, b, preferred_element_type=F32)


def _rope_lanes(xb, cos, sin, half, first_half):
    rot = jnp.where(first_half, pltpu.roll(xb, LANE - half, 1), pltpu.roll(xb, half, 1))
    return xb * cos + rot * sin


def _mod_kernel(c_ref, w_ref, b_ref, o_ref):
    c = c_ref[...]
    c_act = c * (1.0 / (1.0 + jnp.exp(-c)))
    o_ref[0] = _dot(c_act.astype(BF16), w_ref[0].astype(BF16)) + b_ref[0]


def _expand_rope(t, lane):
    def put(table, lo, width, src):
        moved = pltpu.roll(t, (lo - src) % LANE, 1)
        return jnp.where((lane >= lo) & (lane < lo + width), moved, table)

    h1, h2 = MLA_ROPE // 2, MOBA_ROT // 2
    one, zero = jnp.ones_like(t), jnp.zeros_like(t)
    x1 = MLA_NOPE
    cosa = put(put(one, x1, h1, 0), x1 + h1, h1, 0)
    sina = put(zero, x1 + h1, h1, h1) - put(zero, x1, h1, h1)
    cosm, sinp, sinn = one, zero, zero
    for head in range(LANE // HEAD):
        cosm = put(put(cosm, head * HEAD, h2, 2 * h1), head * HEAD + h2, h2, 2 * h1)
        sinn = put(sinn, head * HEAD, h2, 2 * h1 + h2)
        sinp = put(sinp, head * HEAD + h2, h2, 2 * h1 + h2)
    return cosa, sina, cosm, sinp - sinn


def _adaln_mod(c, w_ada, b_ada):
    depth, d, n = w_ada.shape
    b = c.shape[0]
    tn = TN_MOD
    return pl.pallas_call(
        _mod_kernel,
        grid=(depth, n // tn),
        in_specs=[
            pl.BlockSpec((b, d), lambda l, j: (0, 0)),
            pl.BlockSpec((1, d, tn), lambda l, j: (l, 0, j)),
            pl.BlockSpec((1, 1, tn), lambda l, j: (l, 0, j)),
        ],
        out_specs=pl.BlockSpec((1, b, tn), lambda l, j: (l, 0, j)),
        out_shape=jax.ShapeDtypeStruct((depth, b, n), F32),
        compiler_params=pltpu.CompilerParams(
            dimension_semantics=("arbitrary", "arbitrary"), vmem_limit_bytes=VMEM_LIMIT),
        name="adaln_mod",
    )(c, w_ada, b_ada.reshape(depth, 1, n))


def _inproj_kernel(x_ref, shift_ref, scale_ref, g1_ref, wlat_ref, wkpe_ref, w_ref, gq_ref, wuq_ref,
                   gkv_ref, wukv_ref, rope_ref, mla_ref, sb_ref, mb_ref, gate_ref):
    x = x_ref[0]
    h = _rms(x, g1_ref[...]) * (1.0 + scale_ref[0]) + shift_ref[0]
    h = h.astype(BF16)
    lane = lax.broadcasted_iota(jnp.int32, (1, LANE), 1)

    cosa, sina, cosm, sinm = _expand_rope(rope_ref[0], lane)
    mla_first = lane < MLA_NOPE + MLA_ROPE // 2
    mb_first = (lane % HEAD) < MOBA_ROT // 2
    mla_scale = LOG2E / math.sqrt(MLA_QK)
    sb_scale = 1.0 / math.sqrt(HEAD)

    qlat = _dot(h, wlat_ref[:, :MLA_Q_RANK])
    ckv = _dot(h, wlat_ref[:, MLA_Q_RANK:])
    kpe = _dot(h, wkpe_ref[...])

    sb_ref[0, :, :W_BRANCH] = (_dot(h, w_ref[:, :W_BRANCH]) * sb_scale).astype(BF16)
    sb_ref[0, :, W_BRANCH:] = _dot(h, w_ref[:, W_BRANCH:R_MB]).astype(BF16)

    q = _dot(_rms(qlat, gq_ref[...]).astype(BF16), wuq_ref[...])
    kv = _dot(_rms(ckv, gkv_ref[...]).astype(BF16), wukv_ref[...])

    for part, scale in ((0, sb_scale * LOG2E), (1, 1.0)):
        c0 = R_MB + part * W_BRANCH
        acc = _dot(h, w_ref[:, c0:c0 + W_BRANCH])
        for cb in range(W_BRANCH // LANE):
            blk = _rope_lanes(acc[:, cb * LANE:(cb + 1) * LANE], cosm, sinm, MOBA_ROT // 2, mb_first)
            mb_ref[0, :, part * W_BRANCH + cb * LANE:part * W_BRANCH + (cb + 1) * LANE] = (
                blk * scale).astype(BF16)
    mb_ref[0, :, 2 * W_BRANCH:] = _dot(h, w_ref[:, R_MB + 2 * W_BRANCH:R_GATE]).astype(BF16)

    kpe = _rope_lanes(kpe, cosa, sina, MLA_ROPE // 2, mla_first)
    k_off = N_HEADS * LANE
    for hd in range(N_HEADS):
        qb = _rope_lanes(q[:, hd * LANE:(hd + 1) * LANE], cosa, sina, MLA_ROPE // 2, mla_first)
        mla_ref[0, :, hd * LANE:(hd + 1) * LANE] = (qb * mla_scale).astype(BF16)
        kb = kv[:, hd * LANE:(hd + 1) * LANE] + kpe
        mla_ref[0, :, k_off + hd * LANE:k_off + (hd + 1) * LANE] = kb.astype(BF16)
    mla_ref[0, :, 2 * k_off:] = kv[:, k_off:].astype(BF16)

    for br in range(3):
        c0 = R_GATE + br * D_MODEL
        gl = _dot(h, w_ref[:, c0:c0 + D_MODEL])
        gate_ref[0, :, br * D_MODEL:(br + 1) * D_MODEL] = (1.0 / (1.0 + jnp.exp(-gl))).astype(BF16)


def _const_spec(shape):
    nd = len(shape)
    return pl.BlockSpec(shape, lambda *_: (0,) * nd, pipeline_mode=pl.Buffered(1))


def _inproj(x, shift, scale, g1, wlat, wkpe, wrest, gq, wuq, gkv, wukv, rope):
    b, s, d = x.shape
    tm = TQ
    row = lambda w: pl.BlockSpec((1, tm, w), lambda i, j: (i, j, 0))
    vec = pl.BlockSpec((1, 1, d), lambda i, j: (i, 0, 0))
    return pl.pallas_call(
        _inproj_kernel,
        grid=(b, s // tm),
        in_specs=[row(d), vec, vec, _const_spec((1, d)), _const_spec(wlat.shape),
                  _const_spec(wkpe.shape), _const_spec(wrest.shape), _const_spec(gq.shape),
                  _const_spec(wuq.shape), _const_spec(gkv.shape), _const_spec(wukv.shape),
                  row(LANE)],
        out_specs=[row(W_MLA_OUT), row(3 * W_BRANCH), row(3 * W_BRANCH), row(3 * D_MODEL)],
        out_shape=[jax.ShapeDtypeStruct((b, s, W_MLA_OUT), BF16),
                   jax.ShapeDtypeStruct((b, s, 3 * W_BRANCH), BF16),
                   jax.ShapeDtypeStruct((b, s, 3 * W_BRANCH), BF16),
                   jax.ShapeDtypeStruct((b, s, 3 * D_MODEL), BF16)],
        compiler_params=pltpu.CompilerParams(
            dimension_semantics=("arbitrary", "arbitrary"), vmem_limit_bytes=VMEM_LIMIT),
        name="inproj",
    )(x, shift, scale, g1, wlat, wkpe, wrest, gq, wuq, gkv, wukv, rope)


def _transpose_values(v_ref, vt_ref, nkv, stride):
    for j in range(nkv):
        for c in range(W_BRANCH // LANE):
            blk = v_ref[0, j * TQ:(j + 1) * TQ, c * LANE:(c + 1) * LANE].astype(F32).T.astype(BF16)
            for a in range(2):
                h = 2 * c + a
                vt_ref[j, h * stride:h * stride + HEAD, :] = blk[a * HEAD:(a + 1) * HEAD, :]
                if stride > HEAD:
                    vt_ref[j, h * stride + HEAD:(h + 1) * stride, :] = jnp.ones((stride - HEAD, TQ), BF16)


def _store_heads(o_ref, acc_ref, stride, row0=0):
    for c in range(W_BRANCH // LANE):
        parts = []
        for h in (2 * c, 2 * c + 1):
            part = acc_ref[h * stride:h * stride + HEAD, :]
            if stride > HEAD:
                part = part / acc_ref[h * stride + HEAD:h * stride + HEAD + 1, :]
            parts.append(part)
        o_ref[0, row0:row0 + TQ, c * LANE:(c + 1) * LANE] = (
            jnp.concatenate(parts, axis=0).T.astype(BF16))


def _stack_pair_queries(q_ref, qbd_ref, row0=0):
    pairs, w, _ = qbd_ref.shape
    zero = jnp.zeros((w // 2, TQ), BF16)
    for p in range(pairs):
        qt = q_ref[0, row0:row0 + TQ, p * w:(p + 1) * w].astype(F32).T.astype(BF16)
        qbd_ref[p, :w // 2, :TQ] = qt[:w // 2, :]
        qbd_ref[p, :w // 2, TQ:] = zero
        qbd_ref[p, w // 2:, :TQ] = zero
        qbd_ref[p, w // 2:, TQ:] = qt[w // 2:, :]


def _pair_scores(k_ref, qbd_ref, row0, p):
    w = qbd_ref.shape[1]
    return _dot(k_ref[0, pl.ds(row0, TQ), p * w:(p + 1) * w], qbd_ref[p])


def _causal(strict):
    key = lax.broadcasted_iota(jnp.int32, (TQ, TQ), 0)
    qry = lax.broadcasted_iota(jnp.int32, (TQ, TQ), 1)
    return key < qry if strict else key <= qry


def _stage_pair(k_ref, qbd_ref, s_ref, mx_ref, j, p, mask):
    pair = _pair_scores(k_ref, qbd_ref, pl.multiple_of(j * TQ, TQ), p)
    for a in range(2):
        h = 2 * p + a
        s = mask(h, pair[:, a * TQ:(a + 1) * TQ])
        s_ref[p, :, a * TQ:(a + 1) * TQ] = s
        mx_ref[h:h + 1, :] = jnp.max(s, axis=0, keepdims=True)


def _update_pair(s_ref, mx_ref, vt_blk, m_ref, acc_ref, p, first):
    for a in range(2):
        h = 2 * p + a
        rows = slice(h * HEAD_L, (h + 1) * HEAD_L)
        m_new = mx_ref[h:h + 1, :]
        if not first:
            m_old = m_ref[h:h + 1, :]
            m_new = jnp.maximum(m_old, m_new)
            alpha = jnp.exp2(m_old - m_new)
        pr = jnp.exp2(s_ref[p, :, a * TQ:(a + 1) * TQ] - m_new)
        m_ref[h:h + 1, :] = m_new
        o = _dot(vt_blk[rows, :], pr.astype(BF16))
        acc_ref[rows, :] = o if first else alpha * acc_ref[rows, :] + o


def _paired_sweep(t, stage, update):
    pairs = range(N_HEADS // 2)
    tiles = (0, 1)

    def step(nxt_slot, nxt_j, cur_slot, cur_j, first):
        for p in pairs:
            for u in tiles:
                stage(u, nxt_slot, nxt_j, False, p)
                update(u, cur_slot, cur_j[u], first, p)

    for p in pairs:
        for u in tiles:
            stage(u, 0, 2 * t + u, True, p)
    step(1, 0, 0, (2 * t, 2 * t + 1), True)

    def two_blocks(i, carry):
        j = 2 * i
        step(0, j + 1, 1, (j, j), False)
        step(1, j + 2, 0, (j + 1, j + 1), False)
        return carry

    lax.fori_loop(0, t, two_blocks, 0)
    for p in pairs:
        update(1, 1, 2 * t, False, p)


def _mla_kernel(q_ref, k_ref, v_ref, o_ref, vt_ref, qbd_ref, s_ref, mx_ref, m_ref, acc_ref, *, nkv):
    t = pl.program_id(1)

    @pl.when(t == 0)
    def _():
        _transpose_values(v_ref, vt_ref, nkv, HEAD_L)

    for u in range(2):
        _stack_pair_queries(q_ref, qbd_ref.at[u], u * TQ)

    def stage(u, slot, j, diag, p):
        mask = (lambda h, s: jnp.where(_causal(False), s, NEG_INF)) if diag else (lambda h, s: s)
        _stage_pair(k_ref, qbd_ref.at[u], s_ref.at[slot, u], mx_ref.at[slot, u], j, p, mask)

    def update(u, slot, j, first, p):
        _update_pair(s_ref.at[slot, u], mx_ref.at[slot, u], vt_ref.at[j], m_ref.at[u], acc_ref.at[u],
                     p, first)

    _paired_sweep(t, stage, update)
    for u in range(2):
        _store_heads(o_ref, acc_ref.at[u], HEAD_L, u * TQ)


def _sb_kernel(q_ref, k_ref, v_ref, o_ref, vt_ref, qbd_ref, run_ref, acc_ref, *, nkv):
    qi = pl.program_id(1)

    @pl.when(qi == 0)
    def _():
        _transpose_values(v_ref, vt_ref, nkv, HEAD)

    _stack_pair_queries(q_ref, qbd_ref)

    def block(j, first):
        row0 = pl.multiple_of(j * TQ, TQ)
        suffix = jnp.where(_causal(False), 1.0, 0.0).astype(BF16)
        zs = [_pair_scores(k_ref, qbd_ref, row0, p) for p in range(N_HEADS // 2)]
        zs = [zs[h // 2][:, (h % 2) * TQ:(h % 2 + 1) * TQ] for h in range(N_HEADS)]
        incls = []
        for h in range(N_HEADS):
            z = zs[h]
            neg_abs = lax.bitcast_convert_type(
                lax.bitcast_convert_type(z, jnp.uint32) | jnp.uint32(0x80000000), F32)
            drop = jnp.maximum(z, 0.0) + jnp.log(1.0 + jnp.exp(neg_abs))
            if first:
                drop = jnp.where(_causal(True), drop, 0.0)
            hi = lax.bitcast_convert_type(
                lax.bitcast_convert_type(drop, jnp.uint32) & jnp.uint32(0xFFFF0000), F32)
            both = _dot(suffix, jnp.concatenate([hi.astype(BF16), (drop - hi).astype(BF16)], axis=1))
            incls.append(both[:, :TQ] + both[:, TQ:])
        for h in range(N_HEADS):
            rows = slice(h * HEAD, (h + 1) * HEAD)
            if first:
                w = jnp.where(_causal(True), jnp.exp(zs[h] - incls[h]), 0.0)
                run_ref[h:h + 1, :] = -incls[h][0:1, :]
                acc_ref[rows, :] = _dot(vt_ref[j, rows, :], w.astype(BF16))
            else:
                run = run_ref[h:h + 1, :]
                w = jnp.exp((zs[h] + run) - incls[h])
                run_ref[h:h + 1, :] = run - incls[h][0:1, :]
                acc_ref[rows, :] = acc_ref[rows, :] + _dot(vt_ref[j, rows, :], w.astype(BF16))

    def live(carry):
        i, top = carry
        return (i < qi) & (top > SB_DEAD)

    def body(carry):
        i, _ = carry
        block(qi - 1 - i, False)
        return i + 1, jnp.max(run_ref[...])

    block(qi, True)
    lax.while_loop(live, body, (jnp.int32(0), jnp.max(run_ref[...])))
    _store_heads(o_ref, acc_ref, HEAD)


def _top_blocks(gate, blk, own):
    taken = -3.0e38
    sel = jnp.zeros(gate.shape, F32)
    for r in range(MOBA_TOPK):
        best = jnp.max(gate, axis=0, keepdims=True)
        first = jnp.min(jnp.where(gate == best, blk, gate.shape[0]), axis=0, keepdims=True)
        chosen = blk == first
        sel = jnp.where(chosen, jnp.where(r < own, 1.0, 0.0), sel)
        gate = jnp.where(chosen, taken, gate)
    return sel


def _moba_kernel(q_ref, k_ref, v_ref, o_ref, vt_ref, qbd_ref, s_ref, mx_ref, m_ref, acc_ref, km_ref,
                 sel_ref, *, nkv, nbp):
    t = pl.program_id(1)
    lane = lax.broadcasted_iota(jnp.int32, (1, W_BRANCH), 1)

    @pl.when(t == 0)
    def _():
        _transpose_values(v_ref, vt_ref, nkv, HEAD_L)
        km_ref[...] = jnp.zeros_like(km_ref)
        for j in range(nkv):
            mean = jnp.mean(k_ref[0, j * TQ:(j + 1) * TQ, :].astype(F32), axis=0, keepdims=True)
            for h in range(N_HEADS):
                in_head = (lane >= h * HEAD) & (lane < (h + 1) * HEAD)
                km_ref[h * nbp + j:h * nbp + j + 1, :] = jnp.where(in_head, mean, 0.0)

    blk = lax.broadcasted_iota(jnp.int32, (nbp, TQ), 0)
    for u in range(2):
        _stack_pair_queries(q_ref, qbd_ref.at[u], u * TQ)
        own = 2 * t + u
        gates = lax.dot_general(km_ref[...], q_ref[0, u * TQ:(u + 1) * TQ, :].astype(F32),
                                (((1,), (1,)), ((), ())), precision=lax.Precision.HIGHEST,
                                preferred_element_type=F32)
        for h in range(N_HEADS):
            gate = jnp.where(blk < own, gates[h * nbp:(h + 1) * nbp, :], NEG_INF)
            sel_ref[u, h * nbp:(h + 1) * nbp, :] = _top_blocks(gate, blk, own)

    def stage(u, slot, j, diag, p):
        if diag:
            mask = lambda h, s: jnp.where(_causal(False), s, NEG_INF)
        else:
            mask = lambda h, s: jnp.where(sel_ref[u, pl.ds(h * nbp + j, 1), :] > 0.5, s, NEG_INF)
        _stage_pair(k_ref, qbd_ref.at[u], s_ref.at[slot, u], mx_ref.at[slot, u], j, p, mask)

    def update(u, slot, j, first, p):
        _update_pair(s_ref.at[slot, u], mx_ref.at[slot, u], vt_ref.at[j], m_ref.at[u], acc_ref.at[u],
                     p, first)

    _paired_sweep(t, stage, update)
    for u in range(2):
        _store_heads(o_ref, acc_ref.at[u], HEAD_L, u * TQ)


def _attention(kind, src, qk_width, v_col):
    b, s, _ = src.shape
    nkv = s // TQ
    pair_w = 2 * qk_width // N_HEADS
    if kind == "sb":
        tiles = 1
        rows = N_HEADS * HEAD
        body = functools.partial(_sb_kernel, nkv=nkv)
        scratch = [pltpu.VMEM((nkv, rows, TQ), BF16), pltpu.VMEM((N_HEADS // 2, pair_w, 2 * TQ), BF16),
                   pltpu.VMEM((N_HEADS, TQ), F32), pltpu.VMEM((rows, TQ), F32)]
    else:
        tiles = 2
        rows = N_HEADS * HEAD_L
        scratch = [pltpu.VMEM((nkv, rows, TQ), BF16),
                   pltpu.VMEM((tiles, N_HEADS // 2, pair_w, 2 * TQ), BF16),
                   pltpu.VMEM((2, tiles, N_HEADS // 2, TQ, 2 * TQ), F32),
                   pltpu.VMEM((2, tiles, N_HEADS, TQ), F32),
                   pltpu.VMEM((tiles, N_HEADS, TQ), F32),
                   pltpu.VMEM((tiles, rows, TQ), F32)]
        if kind == "mla":
            body = functools.partial(_mla_kernel, nkv=nkv)
        else:
            nbp = -(-nkv // 8) * 8
            body = functools.partial(_moba_kernel, nkv=nkv, nbp=nbp)
            scratch += [pltpu.VMEM((N_HEADS * nbp, W_BRANCH), F32),
                        pltpu.VMEM((tiles, N_HEADS * nbp, TQ), F32)]
    tq = tiles * TQ
    return pl.pallas_call(
        body,
        grid=(b, s // tq),
        in_specs=[
            pl.BlockSpec((1, tq, qk_width), lambda i, t: (i, t, 0)),
            pl.BlockSpec((1, s, qk_width), lambda i, t: (i, 0, 1)),
            pl.BlockSpec((1, s, W_BRANCH), lambda i, t: (i, 0, v_col)),
        ],
        out_specs=pl.BlockSpec((1, tq, W_BRANCH), lambda i, t: (i, t, 0)),
        out_shape=jax.ShapeDtypeStruct((b, s, W_BRANCH), BF16),
        scratch_shapes=scratch,
        compiler_params=pltpu.CompilerParams(
            dimension_semantics=("arbitrary", "arbitrary"), vmem_limit_bytes=VMEM_LIMIT),
        name="attn_" + kind,
    )(src, src, src)


def _merge_kernel(x_ref, gate1_ref, oa_ref, ob_ref, oc_ref, g_ref, wa_ref, wb_ref, wc_ref, wo_ref,
                  o_ref):
    merged = g_ref[0, :, :D_MODEL].astype(F32) * _dot(oa_ref[0], wa_ref[...])
    merged = merged + g_ref[0, :, D_MODEL:2 * D_MODEL].astype(F32) * _dot(ob_ref[0], wb_ref[...])
    merged = merged + g_ref[0, :, 2 * D_MODEL:].astype(F32) * _dot(oc_ref[0], wc_ref[...])
    o_ref[0] = x_ref[0] + gate1_ref[0] * _dot(merged.astype(BF16), wo_ref[...])


def _merge(x, gate1, oa, ob, oc, gates, wa, wb, wc, wo):
    b, s, d = x.shape
    tm = TM_DENSE
    row = lambda w: pl.BlockSpec((1, tm, w), lambda i, j: (i, j, 0))
    vec = pl.BlockSpec((1, 1, d), lambda i, j: (i, 0, 0))
    return pl.pallas_call(
        _merge_kernel,
        grid=(b, s // tm),
        in_specs=[row(d), vec, row(W_BRANCH), row(W_BRANCH), row(W_BRANCH), row(3 * d),
                  _const_spec(wa.shape), _const_spec(wb.shape), _const_spec(wc.shape),
                  _const_spec(wo.shape)],
        out_specs=row(d),
        out_shape=jax.ShapeDtypeStruct((b, s, d), F32),
        compiler_params=pltpu.CompilerParams(
            dimension_semantics=("arbitrary", "arbitrary"), vmem_limit_bytes=VMEM_LIMIT),
        name="merge",
    )(x, gate1, oa, ob, oc, gates, wa, wb, wc, wo)


def _ffn_kernel(x_ref, shift_ref, scale_ref, gate_ref, g2_ref, w1_ref, w2_ref, gf_ref, o_ref, *,
                final_norm):
    x = x_ref[0]
    h = (_rms(x, g2_ref[...]) * (1.0 + scale_ref[0]) + shift_ref[0]).astype(BF16)
    acc = jnp.zeros_like(x)
    chunk = D_MODEL
    for c in range(D_FF // chunk):
        u = jnp.maximum(_dot(h, w1_ref[:, c * chunk:(c + 1) * chunk]), 0.0)
        acc = acc + _dot((u * u).astype(BF16), w2_ref[c * chunk:(c + 1) * chunk, :])
    y = x + gate_ref[0] * acc
    if final_norm:
        y = _rms(y, gf_ref[...])
    o_ref[0] = y


def _ffn(x, shift, scale, gate, g2, w1, w2, gf, final_norm):
    b, s, d = x.shape
    tm = TM_DENSE
    row = pl.BlockSpec((1, tm, d), lambda i, j: (i, j, 0))
    vec = pl.BlockSpec((1, 1, d), lambda i, j: (i, 0, 0))
    return pl.pallas_call(
        functools.partial(_ffn_kernel, final_norm=final_norm),
        grid=(b, s // tm),
        in_specs=[row, vec, vec, vec, _const_spec((1, d)), _const_spec(w1.shape),
                  _const_spec(w2.shape), _const_spec((1, d))],
        out_specs=row,
        out_shape=jax.ShapeDtypeStruct((b, s, d), F32),
        compiler_params=pltpu.CompilerParams(
            dimension_semantics=("arbitrary", "arbitrary"), vmem_limit_bytes=VMEM_LIMIT),
        name="ffn",
    )(x, shift, scale, gate, g2, w1, w2, gf)


def _rope_table(positions):
    pos = positions.astype(F32)[..., None]
    parts = []
    for rot_dim in (MLA_ROPE, MOBA_ROT):
        inv = ROPE_THETA ** (-jnp.arange(0, rot_dim, 2, dtype=F32) / rot_dim)
        parts += [jnp.cos(pos * inv), jnp.sin(pos * inv)]
    used = MLA_ROPE + MOBA_ROT
    return jnp.pad(jnp.concatenate(parts, axis=-1), ((0, 0), (0, 0), (0, LANE - used)))


def _layer_weights(w_in, w_uq, w_ukv):
    wlat = w_in[:, :C_KPE].astype(BF16)
    wkpe = jnp.pad(w_in[:, C_KPE:C_REST], ((0, 0), (MLA_NOPE, LANE - MLA_QK))).astype(BF16)
    wrest = w_in[:, C_REST:].astype(BF16)
    wuq = jnp.pad(w_uq.reshape(MLA_Q_RANK, N_HEADS, MLA_QK),
                  ((0, 0), (0, 0), (0, LANE - MLA_QK))).reshape(MLA_Q_RANK, N_HEADS * LANE)
    ukv = w_ukv.reshape(MLA_KV_RANK, N_HEADS, MLA_NOPE + MLA_V)
    wk = jnp.pad(ukv[:, :, :MLA_NOPE], ((0, 0), (0, 0), (0, LANE - MLA_NOPE)))
    wukv = jnp.concatenate([wk.reshape(MLA_KV_RANK, N_HEADS * LANE),
                            ukv[:, :, MLA_NOPE:].reshape(MLA_KV_RANK, N_HEADS * MLA_V)], axis=1)
    return wlat, wkpe, wrest, wuq.astype(BF16), wukv.astype(BF16)


def kernel(x, c, positions, w_ada, b_ada, norm1_g, norm2_g, w_in, q_norm_g, w_uq, kv_norm_g, w_ukv,
           w_o_mla, w_o_sb, w_o_moba, w_out, w_ff1, w_ff2, final_norm_g):
    b, s, d = x.shape
    depth = w_ada.shape[0]
    assert d == D_MODEL and s % TM_DENSE == 0 and w_in.shape[-1] == D_IN
    rope = _rope_table(positions)
    mod = _adaln_mod(c, w_ada, b_ada)
    gf = final_norm_g.reshape(1, d)
    for l in range(depth):
        shift1, scale1, gate1, shift2, scale2, gate2 = [
            m.reshape(b, 1, d) for m in jnp.split(mod[l], 6, axis=-1)]
        wlat, wkpe, wrest, wuq, wukv = _layer_weights(w_in[l], w_uq[l], w_ukv[l])
        mla, sb, mb, gates = _inproj(
            x, shift1, scale1, norm1_g[l].reshape(1, d), wlat, wkpe, wrest,
            q_norm_g[l].reshape(1, -1), wuq, kv_norm_g[l].reshape(1, -1), wukv, rope)
        o_mla = _attention("mla", mla, N_HEADS * LANE, 4)
        o_sb = _attention("sb", sb, W_BRANCH, 2)
        o_mb = _attention("moba", mb, W_BRANCH, 2)
        x = _merge(x, gate1, o_mla, o_sb, o_mb, gates, w_o_mla[l].astype(BF16),
                   w_o_sb[l].astype(BF16), w_o_moba[l].astype(BF16), w_out[l].astype(BF16))
        x = _ffn(x, shift2, scale2, gate2, norm2_g[l].reshape(1, d), w_ff1[l].astype(BF16),
                 w_ff2[l].astype(BF16), gf, l == depth - 1)
    return x
```

```python
import functools
import math

import jax
import jax.numpy as jnp
from jax import lax
from jax.experimental import pallas as pl
from jax.experimental.pallas import tpu as pltpu

F32 = jnp.float32
BF16 = jnp.bfloat16

D_MODEL = 1024
N_HEADS = 8
MLA_NOPE = 64
MLA_ROPE = 32
MLA_V = 64
MLA_QK = MLA_NOPE + MLA_ROPE
MLA_Q_RANK = 768
MLA_KV_RANK = 256
HEAD = 64
W_BRANCH = N_HEADS * HEAD
MOBA_BLOCK = 256
MOBA_TOPK = 3
MOBA_ROT = HEAD // 4
ROPE_THETA = 500000.0
D_FF = 4 * D_MODEL
EPS = 1e-6
NEG_INF = -1e30
SB_DEAD = -104.0
LOG2E = 1.4426950408889634

LANE = 128
TQ = MOBA_BLOCK
TM_DENSE = 2 * TQ
TN_MOD = 1536
BF16_ROWS = 16
HEAD_L = HEAD + BF16_ROWS
VMEM_LIMIT = 56 * 1024 * 1024

C_KPE = MLA_Q_RANK + MLA_KV_RANK
C_REST = C_KPE + MLA_ROPE
R_MB = 3 * W_BRANCH
R_GATE = R_MB + 3 * W_BRANCH
D_IN = C_REST + R_GATE + 3 * D_MODEL
W_MLA_OUT = 2 * N_HEADS * LANE + N_HEADS * MLA_V


def _rms(xf, g):
    return xf * lax.rsqrt(jnp.mean(xf * xf, axis=-1, keepdims=True) + EPS) * g


def _dot(a, b):
    return jnp.dot(a, b, preferred_element_type=F32)


def _rope_lanes(xb, cos, sin, half, first_half):
    rot = jnp.where(first_half, pltpu.roll(xb, LANE - half, 1), pltpu.roll(xb, half, 1))
    return xb * cos + rot * sin


def _mod_kernel(c_ref, w_ref, b_ref, o_ref):
    c = c_ref[...]
    c_act = c * (1.0 / (1.0 + jnp.exp(-c)))
    o_ref[0] = _dot(c_act.astype(BF16), w_ref[0].astype(BF16)) + b_ref[0]


def _expand_rope(t, lane):
    def put(table, lo, width, src):
        moved = pltpu.roll(t, (lo - src) % LANE, 1)
        return jnp.where((lane >= lo) & (lane < lo + width), moved, table)

    h1, h2 = MLA_ROPE // 2, MOBA_ROT // 2
    one, zero = jnp.ones_like(t), jnp.zeros_like(t)
    x1 = MLA_NOPE
    cosa = put(put(one, x1, h1, 0), x1 + h1, h1, 0)
    sina = put(zero, x1 + h1, h1, h1) - put(zero, x1, h1, h1)
    cosm, sinp, sinn = one, zero, zero
    for head in range(LANE // HEAD):
        cosm = put(put(cosm, head * HEAD, h2, 2 * h1), head * HEAD + h2, h2, 2 * h1)
        sinn = put(sinn, head * HEAD, h2, 2 * h1 + h2)
        sinp = put(sinp, head * HEAD + h2, h2, 2 * h1 + h2)
    return cosa, sina, cosm, sinp - sinn


def _adaln_mod(c, w_ada, b_ada):
    depth, d, n = w_ada.shape
    b = c.shape[0]
    tn = TN_MOD
    return pl.pallas_call(
        _mod_kernel,
        grid=(depth, n // tn),
        in_specs=[
            pl.BlockSpec((b, d), lambda l, j: (0, 0)),
            pl.BlockSpec((1, d, tn), lambda l, j: (l, 0, j)),
            pl.BlockSpec((1, 1, tn), lambda l, j: (l, 0, j)),
        ],
        out_specs=pl.BlockSpec((1, b, tn), lambda l, j: (l, 0, j)),
        out_shape=jax.ShapeDtypeStruct((depth, b, n), F32),
        compiler_params=pltpu.CompilerParams(
            dimension_semantics=("arbitrary", "arbitrary"), vmem_limit_bytes=VMEM_LIMIT),
        name="adaln_mod",
    )(c, w_ada, b_ada.reshape(depth, 1, n))


def _inproj_kernel(x_ref, shift_ref, scale_ref, g1_ref, wlat_ref, wkpe_ref, w_ref, gq_ref, wuq_ref,
                   gkv_ref, wukv_ref, rope_ref, mla_ref, sb_ref, mb_ref, gate_ref):
    x = x_ref[0]
    h = _rms(x, g1_ref[...]) * (1.0 + scale_ref[0]) + shift_ref[0]
    h = h.astype(BF16)
    lane = lax.broadcasted_iota(jnp.int32, (1, LANE), 1)

    cosa, sina, cosm, sinm = _expand_rope(rope_ref[0], lane)
    mla_first = lane < MLA_NOPE + MLA_ROPE // 2
    mb_first = (lane % HEAD) < MOBA_ROT // 2
    mla_scale = LOG2E / math.sqrt(MLA_QK)
    sb_scale = 1.0 / math.sqrt(HEAD)

    qlat = _dot(h, wlat_ref[:, :MLA_Q_RANK])
    ckv = _dot(h, wlat_ref[:, MLA_Q_RANK:])
    kpe = _dot(h, wkpe_ref[...])

    sb_ref[0, :, :W_BRANCH] = (_dot(h, w_ref[:, :W_BRANCH]) * sb_scale).astype(BF16)
    sb_ref[0, :, W_BRANCH:] = _dot(h, w_ref[:, W_BRANCH:R_MB]).astype(BF16)

    q = _dot(_rms(qlat, gq_ref[...]).astype(BF16), wuq_ref[...])
    kv = _dot(_rms(ckv, gkv_ref[...]).astype(BF16), wukv_ref[...])

    for part, scale in ((0, sb_scale * LOG2E), (1, 1.0)):
        c0 = R_MB + part * W_BRANCH
        acc = _dot(h, w_ref[:, c0:c0 + W_BRANCH])
        for cb in range(W_BRANCH // LANE):
            blk = _rope_lanes(acc[:, cb * LANE:(cb + 1) * LANE], cosm, sinm, MOBA_ROT // 2, mb_first)
            mb_ref[0, :, part * W_BRANCH + cb * LANE:part * W_BRANCH + (cb + 1) * LANE] = (
                blk * scale).astype(BF16)
    mb_ref[0, :, 2 * W_BRANCH:] = _dot(h, w_ref[:, R_MB + 2 * W_BRANCH:R_GATE]).astype(BF16)

    kpe = _rope_lanes(kpe, cosa, sina, MLA_ROPE // 2, mla_first)
    k_off = N_HEADS * LANE
    for hd in range(N_HEADS):
        qb = _rope_lanes(q[:, hd * LANE:(hd + 1) * LANE], cosa, sina, MLA_ROPE // 2, mla_first)
        mla_ref[0, :, hd * LANE:(hd + 1) * LANE] = (qb * mla_scale).astype(BF16)
        kb = kv[:, hd * LANE:(hd + 1) * LANE] + kpe
        mla_ref[0, :, k_off + hd * LANE:k_off + (hd + 1) * LANE] = kb.astype(BF16)
    mla_ref[0, :, 2 * k_off:] = kv[:, k_off:].astype(BF16)

    for br in range(3):
        c0 = R_GATE + br * D_MODEL
        gl = _dot(h, w_ref[:, c0:c0 + D_MODEL])
        gate_ref[0, :, br * D_MODEL:(br + 1) * D_MODEL] = (1.0 / (1.0 + jnp.exp(-gl))).astype(BF16)


def _const_spec(shape):
    nd = len(shape)
    return pl.BlockSpec(shape, lambda *_: (0,) * nd, pipeline_mode=pl.Buffered(1))


def _inproj(x, shift, scale, g1, wlat, wkpe, wrest, gq, wuq, gkv, wukv, rope):
    b, s, d = x.shape
    tm = TQ
    row = lambda w: pl.BlockSpec((1, tm, w), lambda i, j: (i, j, 0))
    vec = pl.BlockSpec((1, 1, d), lambda i, j: (i, 0, 0))
    return pl.pallas_call(
        _inproj_kernel,
        grid=(b, s // tm),
        in_specs=[row(d), vec, vec, _const_spec((1, d)), _const_spec(wlat.shape),
                  _const_spec(wkpe.shape), _const_spec(wrest.shape), _const_spec(gq.shape),
                  _const_spec(wuq.shape), _const_spec(gkv.shape), _const_spec(wukv.shape),
                  row(LANE)],
        out_specs=[row(W_MLA_OUT), row(3 * W_BRANCH), row(3 * W_BRANCH), row(3 * D_MODEL)],
        out_shape=[jax.ShapeDtypeStruct((b, s, W_MLA_OUT), BF16),
                   jax.ShapeDtypeStruct((b, s, 3 * W_BRANCH), BF16),
                   jax.ShapeDtypeStruct((b, s, 3 * W_BRANCH), BF16),
                   jax.ShapeDtypeStruct((b, s, 3 * D_MODEL), BF16)],
        compiler_params=pltpu.CompilerParams(
            dimension_semantics=("arbitrary", "arbitrary"), vmem_limit_bytes=VMEM_LIMIT),
        name="inproj",
    )(x, shift, scale, g1, wlat, wkpe, wrest, gq, wuq, gkv, wukv, rope)


def _transpose_values(v_ref, vt_ref, nkv, stride):
    for j in range(nkv):
        for c in range(W_BRANCH // LANE):
            blk = v_ref[0, j * TQ:(j + 1) * TQ, c * LANE:(c + 1) * LANE].astype(F32).T.astype(BF16)
            for a in range(2):
                h = 2 * c + a
                vt_ref[j, h * stride:h * stride + HEAD, :] = blk[a * HEAD:(a + 1) * HEAD, :]
                if stride > HEAD:
                    vt_ref[j, h * stride + HEAD:(h + 1) * stride, :] = jnp.ones((stride - HEAD, TQ), BF16)


def _store_heads(o_ref, acc_ref, stride, row0=0):
    for c in range(W_BRANCH // LANE):
        parts = []
        for h in (2 * c, 2 * c + 1):
            part = acc_ref[h * stride:h * stride + HEAD, :]
            if stride > HEAD:
                part = part / acc_ref[h * stride + HEAD:h * stride + HEAD + 1, :]
            parts.append(part)
        o_ref[0, row0:row0 + TQ, c * LANE:(c + 1) * LANE] = (
            jnp.concatenate(parts, axis=0).T.astype(BF16))


def _stack_pair_queries(q_ref, qbd_ref, row0=0):
    pairs, w, _ = qbd_ref.shape
    zero = jnp.zeros((w // 2, TQ), BF16)
    for p in range(pairs):
        qt = q_ref[0, row0:row0 + TQ, p * w:(p + 1) * w].astype(F32).T.astype(BF16)
        qbd_ref[p, :w // 2, :TQ] = qt[:w // 2, :]
        qbd_ref[p, :w // 2, TQ:] = zero
        qbd_ref[p, w // 2:, :TQ] = zero
        qbd_ref[p, w // 2:, TQ:] = qt[w // 2:, :]


def _pair_scores(k_ref, qbd_ref, row0, p):
    w = qbd_ref.shape[1]
    return _dot(k_ref[0, pl.ds(row0, TQ), p * w:(p + 1) * w], qbd_ref[p])


def _causal(strict):
    key = lax.broadcasted_iota(jnp.int32, (TQ, TQ), 0)
    qry = lax.broadcasted_iota(jnp.int32, (TQ, TQ), 1)
    return key < qry if strict else key <= qry


def _stage_pair(k_ref, qbd_ref, s_ref, mx_ref, j, p, mask):
    pair = _pair_scores(k_ref, qbd_ref, pl.multiple_of(j * TQ, TQ), p)
    for a in range(2):
        h = 2 * p + a
        s = mask(h, pair[:, a * TQ:(a + 1) * TQ])
        s_ref[p, :, a * TQ:(a + 1) * TQ] = s
        mx_ref[h:h + 1, :] = jnp.max(s, axis=0, keepdims=True)


def _update_pair(s_ref, mx_ref, vt_blk, m_ref, acc_ref, p, first):
    for a in range(2):
        h = 2 * p + a
        rows = slice(h * HEAD_L, (h + 1) * HEAD_L)
        m_new = mx_ref[h:h + 1, :]
        if not first:
            m_old = m_ref[h:h + 1, :]
            m_new = jnp.maximum(m_old, m_new)
            alpha = jnp.exp2(m_old - m_new)
        pr = jnp.exp2(s_ref[p, :, a * TQ:(a + 1) * TQ] - m_new)
        m_ref[h:h + 1, :] = m_new
        o = _dot(vt_blk[rows, :], pr.astype(BF16))
        acc_ref[rows, :] = o if first else alpha * acc_ref[rows, :] + o


def _paired_sweep(t, stage, update, tile_by_tile):
    pairs = range(N_HEADS // 2)
    tiles = (0, 1)

    def step(nxt_slot, nxt_j, cur_slot, cur_j, first):
        for p in pairs:
            if tile_by_tile:
                for u in tiles:
                    stage(u, nxt_slot, nxt_j, False, p)
                    update(u, cur_slot, cur_j[u], first, p)
            else:
                for u in tiles:
                    stage(u, nxt_slot, nxt_j, False, p)
                for u in tiles:
                    update(u, cur_slot, cur_j[u], first, p)

    for p in pairs:
        for u in tiles:
            stage(u, 0, 2 * t + u, True, p)
    step(1, 0, 0, (2 * t, 2 * t + 1), True)

    def two_blocks(i, carry):
        j = 2 * i
        step(0, j + 1, 1, (j, j), False)
        step(1, j + 2, 0, (j + 1, j + 1), False)
        return carry

    lax.fori_loop(0, t, two_blocks, 0)
    for p in pairs:
        update(1, 1, 2 * t, False, p)


def _mla_kernel(q_ref, k_ref, v_ref, o_ref, vt_ref, qbd_ref, s_ref, mx_ref, m_ref, acc_ref, *, nkv):
    t = pl.program_id(1)

    @pl.when(t == 0)
    def _():
        _transpose_values(v_ref, vt_ref, nkv, HEAD_L)

    for u in range(2):
        _stack_pair_queries(q_ref, qbd_ref.at[u], u * TQ)

    def stage(u, slot, j, diag, p):
        mask = (lambda h, s: jnp.where(_causal(False), s, NEG_INF)) if diag else (lambda h, s: s)
        _stage_pair(k_ref, qbd_ref.at[u], s_ref.at[slot, u], mx_ref.at[slot, u], j, p, mask)

    def update(u, slot, j, first, p):
        _update_pair(s_ref.at[slot, u], mx_ref.at[slot, u], vt_ref.at[j], m_ref.at[u], acc_ref.at[u],
                     p, first)

    _paired_sweep(t, stage, update, tile_by_tile=False)
    for u in range(2):
        _store_heads(o_ref, acc_ref.at[u], HEAD_L, u * TQ)


def _sb_kernel(q_ref, k_ref, v_ref, o_ref, vt_ref, qbd_ref, run_ref, acc_ref, *, nkv):
    qi = pl.program_id(1)

    @pl.when(qi == 0)
    def _():
        _transpose_values(v_ref, vt_ref, nkv, HEAD)

    _stack_pair_queries(q_ref, qbd_ref)

    def block(j, first):
        row0 = pl.multiple_of(j * TQ, TQ)
        suffix = jnp.where(_causal(False), 1.0, 0.0).astype(BF16)
        zs = [_pair_scores(k_ref, qbd_ref, row0, p) for p in range(N_HEADS // 2)]
        zs = [zs[h // 2][:, (h % 2) * TQ:(h % 2 + 1) * TQ] for h in range(N_HEADS)]
        incls = []
        for h in range(N_HEADS):
            z = zs[h]
            neg_abs = lax.bitcast_convert_type(
                lax.bitcast_convert_type(z, jnp.uint32) | jnp.uint32(0x80000000), F32)
            drop = jnp.maximum(z, 0.0) + jnp.log(1.0 + jnp.exp(neg_abs))
            if first:
                drop = jnp.where(_causal(True), drop, 0.0)
            hi = lax.bitcast_convert_type(
                lax.bitcast_convert_type(drop, jnp.uint32) & jnp.uint32(0xFFFF0000), F32)
            both = _dot(suffix, jnp.concatenate([hi.astype(BF16), (drop - hi).astype(BF16)], axis=1))
            incls.append(both[:, :TQ] + both[:, TQ:])
        for h in range(N_HEADS):
            rows = slice(h * HEAD, (h + 1) * HEAD)
            if first:
                w = jnp.where(_causal(True), jnp.exp(zs[h] - incls[h]), 0.0)
                run_ref[h:h + 1, :] = -incls[h][0:1, :]
                acc_ref[rows, :] = _dot(vt_ref[j, rows, :], w.astype(BF16))
            else:
                run = run_ref[h:h + 1, :]
                w = jnp.exp((zs[h] + run) - incls[h])
                run_ref[h:h + 1, :] = run - incls[h][0:1, :]
                acc_ref[rows, :] = acc_ref[rows, :] + _dot(vt_ref[j, rows, :], w.astype(BF16))

    def live(carry):
        i, top = carry
        return (i < qi) & (top > SB_DEAD)

    def body(carry):
        i, _ = carry
        block(qi - 1 - i, False)
        return i + 1, jnp.max(run_ref[...])

    block(qi, True)
    lax.while_loop(live, body, (jnp.int32(0), jnp.max(run_ref[...])))
    _store_heads(o_ref, acc_ref, HEAD)


def _top_blocks(gate, blk, own):
    taken = -3.0e38
    sel = jnp.zeros(gate.shape, F32)
    for r in range(MOBA_TOPK):
        best = jnp.max(gate, axis=0, keepdims=True)
        first = jnp.min(jnp.where(gate == best, blk, gate.shape[0]), axis=0, keepdims=True)
        chosen = blk == first
        sel = jnp.where(chosen, jnp.where(r < own, 1.0, 0.0), sel)
        gate = jnp.where(chosen, taken, gate)
    return sel


def _moba_kernel(q_ref, k_ref, v_ref, o_ref, vt_ref, qbd_ref, s_ref, mx_ref, m_ref, acc_ref, km_ref,
                 sel_ref, *, nkv, nbp):
    t = pl.program_id(1)
    lane = lax.broadcasted_iota(jnp.int32, (1, W_BRANCH), 1)

    @pl.when(t == 0)
    def _():
        _transpose_values(v_ref, vt_ref, nkv, HEAD_L)
        km_ref[...] = jnp.zeros_like(km_ref)
        for j in range(nkv):
            mean = jnp.mean(k_ref[0, j * TQ:(j + 1) * TQ, :].astype(F32), axis=0, keepdims=True)
            for h in range(N_HEADS):
                in_head = (lane >= h * HEAD) & (lane < (h + 1) * HEAD)
                km_ref[h * nbp + j:h * nbp + j + 1, :] = jnp.where(in_head, mean, 0.0)

    blk = lax.broadcasted_iota(jnp.int32, (nbp, TQ), 0)
    for u in range(2):
        _stack_pair_queries(q_ref, qbd_ref.at[u], u * TQ)
        own = 2 * t + u
        gates = lax.dot_general(km_ref[...], q_ref[0, u * TQ:(u + 1) * TQ, :].astype(F32),
                                (((1,), (1,)), ((), ())), precision=lax.Precision.HIGHEST,
                                preferred_element_type=F32)
        for h in range(N_HEADS):
            gate = jnp.where(blk < own, gates[h * nbp:(h + 1) * nbp, :], NEG_INF)
            sel_ref[u, h * nbp:(h + 1) * nbp, :] = _top_blocks(gate, blk, own)

    def stage(u, slot, j, diag, p):
        if diag:
            mask = lambda h, s: jnp.where(_causal(False), s, NEG_INF)
        else:
            mask = lambda h, s: jnp.where(sel_ref[u, pl.ds(h * nbp + j, 1), :] > 0.5, s, NEG_INF)
        _stage_pair(k_ref, qbd_ref.at[u], s_ref.at[slot, u], mx_ref.at[slot, u], j, p, mask)

    def update(u, slot, j, first, p):
        _update_pair(s_ref.at[slot, u], mx_ref.at[slot, u], vt_ref.at[j], m_ref.at[u], acc_ref.at[u],
                     p, first)

    _paired_sweep(t, stage, update, tile_by_tile=True)
    for u in range(2):
        _store_heads(o_ref, acc_ref.at[u], HEAD_L, u * TQ)


def _attention(kind, src, qk_width, v_col):
    b, s, _ = src.shape
    nkv = s // TQ
    pair_w = 2 * qk_width // N_HEADS
    if kind == "sb":
        tiles = 1
        rows = N_HEADS * HEAD
        body = functools.partial(_sb_kernel, nkv=nkv)
        scratch = [pltpu.VMEM((nkv, rows, TQ), BF16), pltpu.VMEM((N_HEADS // 2, pair_w, 2 * TQ), BF16),
                   pltpu.VMEM((N_HEADS, TQ), F32), pltpu.VMEM((rows, TQ), F32)]
    else:
        tiles = 2
        rows = N_HEADS * HEAD_L
        scratch = [pltpu.VMEM((nkv, rows, TQ), BF16),
                   pltpu.VMEM((tiles, N_HEADS // 2, pair_w, 2 * TQ), BF16),
                   pltpu.VMEM((2, tiles, N_HEADS // 2, TQ, 2 * TQ), F32),
                   pltpu.VMEM((2, tiles, N_HEADS, TQ), F32),
                   pltpu.VMEM((tiles, N_HEADS, TQ), F32),
                   pltpu.VMEM((tiles, rows, TQ), F32)]
        if kind == "mla":
            body = functools.partial(_mla_kernel, nkv=nkv)
        else:
            nbp = -(-nkv // 8) * 8
            body = functools.partial(_moba_kernel, nkv=nkv, nbp=nbp)
            scratch += [pltpu.VMEM((N_HEADS * nbp, W_BRANCH), F32),
                        pltpu.VMEM((tiles, N_HEADS * nbp, TQ), F32)]
    tq = tiles * TQ
    return pl.pallas_call(
        body,
        grid=(b, s // tq),
        in_specs=[
            pl.BlockSpec((1, tq, qk_width), lambda i, t: (i, t, 0)),
            pl.BlockSpec((1, s, qk_width), lambda i, t: (i, 0, 1)),
            pl.BlockSpec((1, s, W_BRANCH), lambda i, t: (i, 0, v_col)),
        ],
        out_specs=pl.BlockSpec((1, tq, W_BRANCH), lambda i, t: (i, t, 0)),
        out_shape=jax.ShapeDtypeStruct((b, s, W_BRANCH), BF16),
        scratch_shapes=scratch,
        compiler_params=pltpu.CompilerParams(
            dimension_semantics=("arbitrary", "arbitrary"), vmem_limit_bytes=VMEM_LIMIT),
        name="attn_" + kind,
    )(src, src, src)


def _merge_kernel(x_ref, gate1_ref, oa_ref, ob_ref, oc_ref, g_ref, wa_ref, wb_ref, wc_ref, wo_ref,
                  o_ref):
    merged = g_ref[0, :, :D_MODEL].astype(F32) * _dot(oa_ref[0], wa_ref[...])
    merged = merged + g_ref[0, :, D_MODEL:2 * D_MODEL].astype(F32) * _dot(ob_ref[0], wb_ref[...])
    merged = merged + g_ref[0, :, 2 * D_MODEL:].astype(F32) * _dot(oc_ref[0], wc_ref[...])
    o_ref[0] = x_ref[0] + gate1_ref[0] * _dot(merged.astype(BF16), wo_ref[...])


def _merge(x, gate1, oa, ob, oc, gates, wa, wb, wc, wo):
    b, s, d = x.shape
    tm = TM_DENSE
    row = lambda w: pl.BlockSpec((1, tm, w), lambda i, j: (i, j, 0))
    vec = pl.BlockSpec((1, 1, d), lambda i, j: (i, 0, 0))
    return pl.pallas_call(
        _merge_kernel,
        grid=(b, s // tm),
        in_specs=[row(d), vec, row(W_BRANCH), row(W_BRANCH), row(W_BRANCH), row(3 * d),
                  _const_spec(wa.shape), _const_spec(wb.shape), _const_spec(wc.shape),
                  _const_spec(wo.shape)],
        out_specs=row(d),
        out_shape=jax.ShapeDtypeStruct((b, s, d), F32),
        compiler_params=pltpu.CompilerParams(
            dimension_semantics=("arbitrary", "arbitrary"), vmem_limit_bytes=VMEM_LIMIT),
        name="merge",
    )(x, gate1, oa, ob, oc, gates, wa, wb, wc, wo)


def _ffn_kernel(x_ref, shift_ref, scale_ref, gate_ref, g2_ref, w1_ref, w2_ref, gf_ref, o_ref, *,
                final_norm):
    x = x_ref[0]
    h = (_rms(x, g2_ref[...]) * (1.0 + scale_ref[0]) + shift_ref[0]).astype(BF16)
    acc = jnp.zeros_like(x)
    chunk = D_MODEL
    for c in range(D_FF // chunk):
        u = jnp.maximum(_dot(h, w1_ref[:, c * chunk:(c + 1) * chunk]), 0.0)
        acc = acc + _dot((u * u).astype(BF16), w2_ref[c * chunk:(c + 1) * chunk, :])
    y = x + gate_ref[0] * acc
    if final_norm:
        y = _rms(y, gf_ref[...])
    o_ref[0] = y


def _ffn(x, shift, scale, gate, g2, w1, w2, gf, final_norm):
    b, s, d = x.shape
    tm = TM_DENSE
    row = pl.BlockSpec((1, tm, d), lambda i, j: (i, j, 0))
    vec = pl.BlockSpec((1, 1, d), lambda i, j: (i, 0, 0))
    return pl.pallas_call(
        functools.partial(_ffn_kernel, final_norm=final_norm),
        grid=(b, s // tm),
        in_specs=[row, vec, vec, vec, _const_spec((1, d)), _const_spec(w1.shape),
                  _const_spec(w2.shape), _const_spec((1, d))],
        out_specs=row,
        out_shape=jax.ShapeDtypeStruct((b, s, d), F32),
        compiler_params=pltpu.CompilerParams(
            dimension_semantics=("arbitrary", "arbitrary"), vmem_limit_bytes=VMEM_LIMIT),
        name="ffn",
    )(x, shift, scale, gate, g2, w1, w2, gf)


def _rope_table(positions):
    pos = positions.astype(F32)[..., None]
    parts = []
    for rot_dim in (MLA_ROPE, MOBA_ROT):
        inv = ROPE_THETA ** (-jnp.arange(0, rot_dim, 2, dtype=F32) / rot_dim)
        parts += [jnp.cos(pos * inv), jnp.sin(pos * inv)]
    used = MLA_ROPE + MOBA_ROT
    return jnp.pad(jnp.concatenate(parts, axis=-1), ((0, 0), (0, 0), (0, LANE - used)))


def _layer_weights(w_in, w_uq, w_ukv):
    wlat = w_in[:, :C_KPE].astype(BF16)
    wkpe = jnp.pad(w_in[:, C_KPE:C_REST], ((0, 0), (MLA_NOPE, LANE - MLA_QK))).astype(BF16)
    wrest = w_in[:, C_REST:].astype(BF16)
    wuq = jnp.pad(w_uq.reshape(MLA_Q_RANK, N_HEADS, MLA_QK),
                  ((0, 0), (0, 0), (0, LANE - MLA_QK))).reshape(MLA_Q_RANK, N_HEADS * LANE)
    ukv = w_ukv.reshape(MLA_KV_RANK, N_HEADS, MLA_NOPE + MLA_V)
    wk = jnp.pad(ukv[:, :, :MLA_NOPE], ((0, 0), (0, 0), (0, LANE - MLA_NOPE)))
    wukv = jnp.concatenate([wk.reshape(MLA_KV_RANK, N_HEADS * LANE),
                            ukv[:, :, MLA_NOPE:].reshape(MLA_KV_RANK, N_HEADS * MLA_V)], axis=1)
    return wlat, wkpe, wrest, wuq.astype(BF16), wukv.astype(BF16)


def kernel(x, c, positions, w_ada, b_ada, norm1_g, norm2_g, w_in, q_norm_g, w_uq, kv_norm_g, w_ukv,
           w_o_mla, w_o_sb, w_o_moba, w_out, w_ff1, w_ff2, final_norm_g):
    b, s, d = x.shape
    depth = w_ada.shape[0]
    assert d == D_MODEL and s % TM_DENSE == 0 and w_in.shape[-1] == D_IN
    rope = _rope_table(positions)
    mod = _adaln_mod(c, w_ada, b_ada)
    gf = final_norm_g.reshape(1, d)
    for l in range(depth):
        shift1, scale1, gate1, shift2, scale2, gate2 = [
            m.reshape(b, 1, d) for m in jnp.split(mod[l], 6, axis=-1)]
        wlat, wkpe, wrest, wuq, wukv = _layer_weights(w_in[l], w_uq[l], w_ukv[l])
        mla, sb, mb, gates = _inproj(
            x, shift1, scale1, norm1_g[l].reshape(1, d), wlat, wkpe, wrest,
            q_norm_g[l].reshape(1, -1), wuq, kv_norm_g[l].reshape(1, -1), wukv, rope)
        o_mla = _attention("mla", mla, N_HEADS * LANE, 4)
        o_sb = _attention("sb", sb, W_BRANCH, 2)
        o_mb = _attention("moba", mb, W_BRANCH, 2)
        x = _merge(x, gate1, o_mla, o_sb, o_mb, gates, w_o_mla[l].astype(BF16),
                   w_o_sb[l].astype(BF16), w_o_moba[l].astype(BF16), w_out[l].astype(BF16))
        x = _ffn(x, shift2, scale2, gate2, norm2_g[l].reshape(1, d), w_ff1[l].astype(BF16),
                 w_ff2[l].astype(BF16), gf, l == depth - 1)
    return x
```

```python
import functools
import math

import jax
import jax.numpy as jnp
from jax import lax
from jax.experimental import pallas as pl
from jax.experimental.pallas import tpu as pltpu

F32 = jnp.float32
BF16 = jnp.bfloat16

D_MODEL = 1024
N_HEADS = 8
MLA_NOPE = 64
MLA_ROPE = 32
MLA_V = 64
MLA_QK = MLA_NOPE + MLA_ROPE
MLA_Q_RANK = 768
MLA_KV_RANK = 256
HEAD = 64
W_BRANCH = N_HEADS * HEAD
MOBA_BLOCK = 256
MOBA_TOPK = 3
MOBA_ROT = HEAD // 4
ROPE_THETA = 500000.0
D_FF = 4 * D_MODEL
EPS = 1e-6
NEG_INF = -1e30
SB_DEAD = -104.0
LOG2E = 1.4426950408889634

LANE = 128
TQ = MOBA_BLOCK
TM_DENSE = 2 * TQ
TN_MOD = 1536
BF16_ROWS = 16
HEAD_L = HEAD + BF16_ROWS
VMEM_LIMIT = 56 * 1024 * 1024

C_KPE = MLA_Q_RANK + MLA_KV_RANK
C_REST = C_KPE + MLA_ROPE
R_MB = 3 * W_BRANCH
R_GATE = R_MB + 3 * W_BRANCH
D_IN = C_REST + R_GATE + 3 * D_MODEL
W_MLA_OUT = 2 * N_HEADS * LANE + N_HEADS * MLA_V


def _rms(xf, g):
    return xf * lax.rsqrt(jnp.mean(xf * xf, axis=-1, keepdims=True) + EPS) * g


def _dot(a, b):
    return jnp.dot(a, b, preferred_element_type=F32)


def _rope_lanes(xb, cos, sin, half, first_half):
    rot = jnp.where(first_half, pltpu.roll(xb, LANE - half, 1), pltpu.roll(xb, half, 1))
    return xb * cos + rot * sin


def _mod_kernel(c_ref, w_ref, b_ref, o_ref):
    c = c_ref[...]
    c_act = c * (1.0 / (1.0 + jnp.exp(-c)))
    o_ref[0] = _dot(c_act.astype(BF16), w_ref[0].astype(BF16)) + b_ref[0]


def _expand_rope(t, lane):
    def put(table, lo, width, src):
        moved = pltpu.roll(t, (lo - src) % LANE, 1)
        return jnp.where((lane >= lo) & (lane < lo + width), moved, table)

    h1, h2 = MLA_ROPE // 2, MOBA_ROT // 2
    one, zero = jnp.ones_like(t), jnp.zeros_like(t)
    x1 = MLA_NOPE
    cosa = put(put(one, x1, h1, 0), x1 + h1, h1, 0)
    sina = put(zero, x1 + h1, h1, h1) - put(zero, x1, h1, h1)
    cosm, sinp, sinn = one, zero, zero
    for head in range(LANE // HEAD):
        cosm = put(put(cosm, head * HEAD, h2, 2 * h1), head * HEAD + h2, h2, 2 * h1)
        sinn = put(sinn, head * HEAD, h2, 2 * h1 + h2)
        sinp = put(sinp, head * HEAD + h2, h2, 2 * h1 + h2)
    return cosa, sina, cosm, sinp - sinn


def _adaln_mod(c, w_ada, b_ada):
    depth, d, n = w_ada.shape
    b = c.shape[0]
    tn = TN_MOD
    return pl.pallas_call(
        _mod_kernel,
        grid=(depth, n // tn),
        in_specs=[
            pl.BlockSpec((b, d), lambda l, j: (0, 0)),
            pl.BlockSpec((1, d, tn), lambda l, j: (l, 0, j)),
            pl.BlockSpec((1, 1, tn), lambda l, j: (l, 0, j)),
        ],
        out_specs=pl.BlockSpec((1, b, tn), lambda l, j: (l, 0, j)),
        out_shape=jax.ShapeDtypeStruct((depth, b, n), F32),
        compiler_params=pltpu.CompilerParams(
            dimension_semantics=("arbitrary", "arbitrary"), vmem_limit_bytes=VMEM_LIMIT),
        name="adaln_mod",
    )(c, w_ada, b_ada.reshape(depth, 1, n))


def _inproj_kernel(x_ref, shift_ref, scale_ref, g1_ref, wlat_ref, wkpe_ref, w_ref, gq_ref, wuq_ref,
                   gkv_ref, wukv_ref, rope_ref, mla_ref, sb_ref, mb_ref, gate_ref):
    x = x_ref[0]
    h = _rms(x, g1_ref[...]) * (1.0 + scale_ref[0]) + shift_ref[0]
    h = h.astype(BF16)
    lane = lax.broadcasted_iota(jnp.int32, (1, LANE), 1)

    cosa, sina, cosm, sinm = _expand_rope(rope_ref[0], lane)
    mla_first = lane < MLA_NOPE + MLA_ROPE // 2
    mb_first = (lane % HEAD) < MOBA_ROT // 2
    mla_scale = LOG2E / math.sqrt(MLA_QK)
    sb_scale = 1.0 / math.sqrt(HEAD)

    qlat = _dot(h, wlat_ref[:, :MLA_Q_RANK])
    ckv = _dot(h, wlat_ref[:, MLA_Q_RANK:])
    kpe = _dot(h, wkpe_ref[...])

    sb_ref[0, :, :W_BRANCH] = (_dot(h, w_ref[:, :W_BRANCH]) * sb_scale).astype(BF16)
    sb_ref[0, :, W_BRANCH:] = _dot(h, w_ref[:, W_BRANCH:R_MB]).astype(BF16)

    q = _dot(_rms(qlat, gq_ref[...]).astype(BF16), wuq_ref[...])
    kv = _dot(_rms(ckv, gkv_ref[...]).astype(BF16), wukv_ref[...])

    for part, scale in ((0, sb_scale * LOG2E), (1, 1.0)):
        c0 = R_MB + part * W_BRANCH
        acc = _dot(h, w_ref[:, c0:c0 + W_BRANCH])
        for cb in range(W_BRANCH // LANE):
            blk = _rope_lanes(acc[:, cb * LANE:(cb + 1) * LANE], cosm, sinm, MOBA_ROT // 2, mb_first)
            mb_ref[0, :, part * W_BRANCH + cb * LANE:part * W_BRANCH + (cb + 1) * LANE] = (
                blk * scale).astype(BF16)
    mb_ref[0, :, 2 * W_BRANCH:] = _dot(h, w_ref[:, R_MB + 2 * W_BRANCH:R_GATE]).astype(BF16)

    kpe = _rope_lanes(kpe, cosa, sina, MLA_ROPE // 2, mla_first)
    k_off = N_HEADS * LANE
    for hd in range(N_HEADS):
        qb = _rope_lanes(q[:, hd * LANE:(hd + 1) * LANE], cosa, sina, MLA_ROPE // 2, mla_first)
        mla_ref[0, :, hd * LANE:(hd + 1) * LANE] = (qb * mla_scale).astype(BF16)
        kb = kv[:, hd * LANE:(hd + 1) * LANE] + kpe
        mla_ref[0, :, k_off + hd * LANE:k_off + (hd + 1) * LANE] = kb.astype(BF16)
    mla_ref[0, :, 2 * k_off:] = kv[:, k_off:].astype(BF16)

    for br in range(3):
        c0 = R_GATE + br * D_MODEL
        gl = _dot(h, w_ref[:, c0:c0 + D_MODEL])
        gate_ref[0, :, br * D_MODEL:(br + 1) * D_MODEL] = (1.0 / (1.0 + jnp.exp(-gl))).astype(BF16)


def _const_spec(shape):
    nd = len(shape)
    return pl.BlockSpec(shape, lambda *_: (0,) * nd, pipeline_mode=pl.Buffered(1))


def _inproj(x, shift, scale, g1, wlat, wkpe, wrest, gq, wuq, gkv, wukv, rope):
    b, s, d = x.shape
    tm = TQ
    row = lambda w: pl.BlockSpec((1, tm, w), lambda i, j: (i, j, 0))
    vec = pl.BlockSpec((1, 1, d), lambda i, j: (i, 0, 0))
    return pl.pallas_call(
        _inproj_kernel,
        grid=(b, s // tm),
        in_specs=[row(d), vec, vec, _const_spec((1, d)), _const_spec(wlat.shape),
                  _const_spec(wkpe.shape), _const_spec(wrest.shape), _const_spec(gq.shape),
                  _const_spec(wuq.shape), _const_spec(gkv.shape), _const_spec(wukv.shape),
                  row(LANE)],
        out_specs=[row(W_MLA_OUT), row(3 * W_BRANCH), row(3 * W_BRANCH), row(3 * D_MODEL)],
        out_shape=[jax.ShapeDtypeStruct((b, s, W_MLA_OUT), BF16),
                   jax.ShapeDtypeStruct((b, s, 3 * W_BRANCH), BF16),
                   jax.ShapeDtypeStruct((b, s, 3 * W_BRANCH), BF16),
                   jax.ShapeDtypeStruct((b, s, 3 * D_MODEL), BF16)],
        compiler_params=pltpu.CompilerParams(
            dimension_semantics=("arbitrary", "arbitrary"), vmem_limit_bytes=VMEM_LIMIT),
        name="inproj",
    )(x, shift, scale, g1, wlat, wkpe, wrest, gq, wuq, gkv, wukv, rope)


def _transpose_values(v_ref, vt_ref, nkv, stride):
    for j in range(nkv):
        for c in range(W_BRANCH // LANE):
            blk = v_ref[0, j * TQ:(j + 1) * TQ, c * LANE:(c + 1) * LANE].astype(F32).T.astype(BF16)
            for a in range(2):
                h = 2 * c + a
                vt_ref[j, h * stride:h * stride + HEAD, :] = blk[a * HEAD:(a + 1) * HEAD, :]
                if stride > HEAD:
                    vt_ref[j, h * stride + HEAD:(h + 1) * stride, :] = jnp.ones((stride - HEAD, TQ), BF16)


def _store_heads(o_ref, acc_ref, stride, row0=0):
    for c in range(W_BRANCH // LANE):
        parts = []
        for h in (2 * c, 2 * c + 1):
            part = acc_ref[h * stride:h * stride + HEAD, :]
            if stride > HEAD:
                part = part / acc_ref[h * stride + HEAD:h * stride + HEAD + 1, :]
            parts.append(part)
        o_ref[0, row0:row0 + TQ, c * LANE:(c + 1) * LANE] = (
            jnp.concatenate(parts, axis=0).T.astype(BF16))


def _stack_pair_queries(q_ref, qbd_ref, row0=0):
    pairs, w, _ = qbd_ref.shape
    zero = jnp.zeros((w // 2, TQ), BF16)
    for p in range(pairs):
        qt = q_ref[0, row0:row0 + TQ, p * w:(p + 1) * w].astype(F32).T.astype(BF16)
        qbd_ref[p, :w // 2, :TQ] = qt[:w // 2, :]
        qbd_ref[p, :w // 2, TQ:] = zero
        qbd_ref[p, w // 2:, :TQ] = zero
        qbd_ref[p, w // 2:, TQ:] = qt[w // 2:, :]


def _pair_scores(k_ref, qbd_ref, row0, p):
    w = qbd_ref.shape[1]
    return _dot(k_ref[0, pl.ds(row0, TQ), p * w:(p + 1) * w], qbd_ref[p])


def _causal(strict):
    key = lax.broadcasted_iota(jnp.int32, (TQ, TQ), 0)
    qry = lax.broadcasted_iota(jnp.int32, (TQ, TQ), 1)
    return key < qry if strict else key <= qry


def _stage_pair(k_ref, qbd_ref, s_ref, mx_ref, j, p, mask):
    pair = _pair_scores(k_ref, qbd_ref, pl.multiple_of(j * TQ, TQ), p)
    for a in range(2):
        h = 2 * p + a
        s = mask(h, pair[:, a * TQ:(a + 1) * TQ])
        s_ref[p, :, a * TQ:(a + 1) * TQ] = s
        mx_ref[h:h + 1, :] = jnp.max(s, axis=0, keepdims=True)


def _update_pair(s_ref, mx_ref, vt_blk, m_ref, acc_ref, p, first):
    for a in range(2):
        h = 2 * p + a
        rows = slice(h * HEAD_L, (h + 1) * HEAD_L)
        m_new = mx_ref[h:h + 1, :]
        if not first:
            m_old = m_ref[h:h + 1, :]
            m_new = jnp.maximum(m_old, m_new)
            alpha = jnp.exp2(m_old - m_new)
        pr = jnp.exp2(s_ref[p, :, a * TQ:(a + 1) * TQ] - m_new)
        m_ref[h:h + 1, :] = m_new
        o = _dot(vt_blk[rows, :], pr.astype(BF16))
        acc_ref[rows, :] = o if first else alpha * acc_ref[rows, :] + o


def _paired_sweep(t, stage, update, tile_by_tile, double_trips):
    pairs = range(N_HEADS // 2)
    tiles = (0, 1)

    def step(nxt_slot, nxt_j, cur_slot, cur_j, first):
        for p in pairs:
            if tile_by_tile:
                for u in tiles:
                    stage(u, nxt_slot, nxt_j, False, p)
                    update(u, cur_slot, cur_j[u], first, p)
            else:
                for u in tiles:
                    stage(u, nxt_slot, nxt_j, False, p)
                for u in tiles:
                    update(u, cur_slot, cur_j[u], first, p)

    for p in pairs:
        for u in tiles:
            stage(u, 0, 2 * t + u, True, p)
    step(1, 0, 0, (2 * t, 2 * t + 1), True)

    def two_blocks(j):
        step(0, j + 1, 1, (j, j), False)
        step(1, j + 2, 0, (j + 1, j + 1), False)

    if double_trips:
        def four_blocks(i, carry):
            two_blocks(4 * i)
            two_blocks(4 * i + 2)
            return carry

        lax.fori_loop(0, lax.shift_right_logical(t, 1), four_blocks, 0)

        @pl.when((t & 1) == 1)
        def _():
            two_blocks(2 * t - 2)
    else:
        def one_trip(i, carry):
            two_blocks(2 * i)
            return carry

        lax.fori_loop(0, t, one_trip, 0)

    for p in pairs:
        update(1, 1, 2 * t, False, p)


def _mla_kernel(q_ref, k_ref, v_ref, o_ref, vt_ref, qbd_ref, s_ref, mx_ref, m_ref, acc_ref, *, nkv):
    t = pl.program_id(1)

    @pl.when(t == 0)
    def _():
        _transpose_values(v_ref, vt_ref, nkv, HEAD_L)

    for u in range(2):
        _stack_pair_queries(q_ref, qbd_ref.at[u], u * TQ)

    def stage(u, slot, j, diag, p):
        mask = (lambda h, s: jnp.where(_causal(False), s, NEG_INF)) if diag else (lambda h, s: s)
        _stage_pair(k_ref, qbd_ref.at[u], s_ref.at[slot, u], mx_ref.at[slot, u], j, p, mask)

    def update(u, slot, j, first, p):
        _update_pair(s_ref.at[slot, u], mx_ref.at[slot, u], vt_ref.at[j], m_ref.at[u], acc_ref.at[u],
                     p, first)

    _paired_sweep(t, stage, update, tile_by_tile=False, double_trips=True)
    for u in range(2):
        _store_heads(o_ref, acc_ref.at[u], HEAD_L, u * TQ)


def _sb_kernel(q_ref, k_ref, v_ref, o_ref, vt_ref, qbd_ref, run_ref, acc_ref, *, nkv):
    t = pl.program_id(1)

    @pl.when(t == 0)
    def _():
        _transpose_values(v_ref, vt_ref, nkv, HEAD)

    for u in range(2):
        _stack_pair_queries(q_ref, qbd_ref.at[u], u * TQ)

    def block(blocks, first):
        suffix = jnp.where(_causal(False), 1.0, 0.0).astype(BF16)
        zs = {}
        for u, j in blocks.items():
            row0 = pl.multiple_of(j * TQ, TQ)
            pairs = [_pair_scores(k_ref, qbd_ref.at[u], row0, p) for p in range(N_HEADS // 2)]
            for h in range(N_HEADS):
                zs[u, h] = pairs[h // 2][:, (h % 2) * TQ:(h % 2 + 1) * TQ]
        incls = {}
        for key, z in zs.items():
            neg_abs = lax.bitcast_convert_type(
                lax.bitcast_convert_type(z, jnp.uint32) | jnp.uint32(0x80000000), F32)
            drop = jnp.maximum(z, 0.0) + jnp.log(1.0 + jnp.exp(neg_abs))
            if first:
                drop = jnp.where(_causal(True), drop, 0.0)
            hi = lax.bitcast_convert_type(
                lax.bitcast_convert_type(drop, jnp.uint32) & jnp.uint32(0xFFFF0000), F32)
            both = _dot(suffix, jnp.concatenate([hi.astype(BF16), (drop - hi).astype(BF16)], axis=1))
            incls[key] = both[:, :TQ] + both[:, TQ:]
        for (u, h), z in zs.items():
            rows = slice(h * HEAD, (h + 1) * HEAD)
            incl = incls[u, h]
            if first:
                w = jnp.where(_causal(True), jnp.exp(z - incl), 0.0)
                run_ref[u, h:h + 1, :] = -incl[0:1, :]
                acc_ref[u, rows, :] = _dot(vt_ref[blocks[u], rows, :], w.astype(BF16))
            else:
                run = run_ref[u, h:h + 1, :]
                w = jnp.exp((z + run) - incl)
                run_ref[u, h:h + 1, :] = run - incl[0:1, :]
                acc_ref[u, rows, :] = acc_ref[u, rows, :] + _dot(vt_ref[blocks[u], rows, :],
                                                                 w.astype(BF16))

    def live(carry):
        i, top = carry
        return (i < 2 * t) & (top > SB_DEAD)

    def body(carry):
        i, _ = carry
        block({0: 2 * t - 1 - i, 1: 2 * t - i}, False)
        return i + 1, jnp.max(run_ref[...])

    block({0: 2 * t, 1: 2 * t + 1}, True)
    trips, _ = lax.while_loop(live, body, (jnp.int32(0), jnp.max(run_ref[...])))

    @pl.when((trips == 2 * t) & (jnp.max(run_ref[1]) > SB_DEAD))
    def _():
        block({1: 0}, False)

    for u in range(2):
        _store_heads(o_ref, acc_ref.at[u], HEAD, u * TQ)


def _top_blocks(gate, blk, own):
    taken = -3.0e38
    sel = jnp.zeros(gate.shape, F32)
    for r in range(MOBA_TOPK):
        best = jnp.max(gate, axis=0, keepdims=True)
        first = jnp.min(jnp.where(gate == best, blk, gate.shape[0]), axis=0, keepdims=True)
        chosen = blk == first
        sel = jnp.where(chosen, jnp.where(r < own, 1.0, 0.0), sel)
        gate = jnp.where(chosen, taken, gate)
    return sel


def _moba_kernel(q_ref, k_ref, v_ref, o_ref, vt_ref, qbd_ref, s_ref, mx_ref, m_ref, acc_ref, km_ref,
                 sel_ref, *, nkv, nbp):
    t = pl.program_id(1)
    lane = lax.broadcasted_iota(jnp.int32, (1, W_BRANCH), 1)

    @pl.when(t == 0)
    def _():
        _transpose_values(v_ref, vt_ref, nkv, HEAD_L)
        km_ref[...] = jnp.zeros_like(km_ref)
        for j in range(nkv):
            mean = jnp.mean(k_ref[0, j * TQ:(j + 1) * TQ, :].astype(F32), axis=0, keepdims=True)
            for h in range(N_HEADS):
                in_head = (lane >= h * HEAD) & (lane < (h + 1) * HEAD)
                km_ref[h * nbp + j:h * nbp + j + 1, :] = jnp.where(in_head, mean, 0.0)

    blk = lax.broadcasted_iota(jnp.int32, (nbp, TQ), 0)
    for u in range(2):
        _stack_pair_queries(q_ref, qbd_ref.at[u], u * TQ)
        own = 2 * t + u
        gates = lax.dot_general(km_ref[...], q_ref[0, u * TQ:(u + 1) * TQ, :].astype(F32),
                                (((1,), (1,)), ((), ())), precision=lax.Precision.HIGHEST,
                                preferred_element_type=F32)
        for h in range(N_HEADS):
            gate = jnp.where(blk < own, gates[h * nbp:(h + 1) * nbp, :], NEG_INF)
            sel_ref[u, h * nbp:(h + 1) * nbp, :] = _top_blocks(gate, blk, own)

    def stage(u, slot, j, diag, p):
        if diag:
            mask = lambda h, s: jnp.where(_causal(False), s, NEG_INF)
        else:
            mask = lambda h, s: jnp.where(sel_ref[u, pl.ds(h * nbp + j, 1), :] > 0.5, s, NEG_INF)
        _stage_pair(k_ref, qbd_ref.at[u], s_ref.at[slot, u], mx_ref.at[slot, u], j, p, mask)

    def update(u, slot, j, first, p):
        _update_pair(s_ref.at[slot, u], mx_ref.at[slot, u], vt_ref.at[j], m_ref.at[u], acc_ref.at[u],
                     p, first)

    _paired_sweep(t, stage, update, tile_by_tile=True, double_trips=False)
    for u in range(2):
        _store_heads(o_ref, acc_ref.at[u], HEAD_L, u * TQ)


def _attention(kind, src, qk_width, v_col):
    b, s, _ = src.shape
    nkv = s // TQ
    pair_w = 2 * qk_width // N_HEADS
    tiles = 2
    if kind == "sb":
        rows = N_HEADS * HEAD
        body = functools.partial(_sb_kernel, nkv=nkv)
        scratch = [pltpu.VMEM((nkv, rows, TQ), BF16),
                   pltpu.VMEM((tiles, N_HEADS // 2, pair_w, 2 * TQ), BF16),
                   pltpu.VMEM((tiles, N_HEADS, TQ), F32), pltpu.VMEM((tiles, rows, TQ), F32)]
    else:
        rows = N_HEADS * HEAD_L
        scratch = [pltpu.VMEM((nkv, rows, TQ), BF16),
                   pltpu.VMEM((tiles, N_HEADS // 2, pair_w, 2 * TQ), BF16),
                   pltpu.VMEM((2, tiles, N_HEADS // 2, TQ, 2 * TQ), F32),
                   pltpu.VMEM((2, tiles, N_HEADS, TQ), F32),
                   pltpu.VMEM((tiles, N_HEADS, TQ), F32),
                   pltpu.VMEM((tiles, rows, TQ), F32)]
        if kind == "mla":
            body = functools.partial(_mla_kernel, nkv=nkv)
        else:
            nbp = -(-nkv // 8) * 8
            body = functools.partial(_moba_kernel, nkv=nkv, nbp=nbp)
            scratch += [pltpu.VMEM((N_HEADS * nbp, W_BRANCH), F32),
                        pltpu.VMEM((tiles, N_HEADS * nbp, TQ), F32)]
    tq = tiles * TQ
    return pl.pallas_call(
        body,
        grid=(b, s // tq),
        in_specs=[
            pl.BlockSpec((1, tq, qk_width), lambda i, t: (i, t, 0)),
            pl.BlockSpec((1, s, qk_width), lambda i, t: (i, 0, 1)),
            pl.BlockSpec((1, s, W_BRANCH), lambda i, t: (i, 0, v_col)),
        ],
        out_specs=pl.BlockSpec((1, tq, W_BRANCH), lambda i, t: (i, t, 0)),
        out_shape=jax.ShapeDtypeStruct((b, s, W_BRANCH), BF16),
        scratch_shapes=scratch,
        compiler_params=pltpu.CompilerParams(
            dimension_semantics=("arbitrary", "arbitrary"), vmem_limit_bytes=VMEM_LIMIT),
        name="attn_" + kind,
    )(src, src, src)


def _merge_kernel(x_ref, gate1_ref, oa_ref, ob_ref, oc_ref, g_ref, wa_ref, wb_ref, wc_ref, wo_ref,
                  o_ref):
    merged = g_ref[0, :, :D_MODEL].astype(F32) * _dot(oa_ref[0], wa_ref[...])
    merged = merged + g_ref[0, :, D_MODEL:2 * D_MODEL].astype(F32) * _dot(ob_ref[0], wb_ref[...])
    merged = merged + g_ref[0, :, 2 * D_MODEL:].astype(F32) * _dot(oc_ref[0], wc_ref[...])
    o_ref[0] = x_ref[0] + gate1_ref[0] * _dot(merged.astype(BF16), wo_ref[...])


def _merge(x, gate1, oa, ob, oc, gates, wa, wb, wc, wo):
    b, s, d = x.shape
    tm = TM_DENSE
    row = lambda w: pl.BlockSpec((1, tm, w), lambda i, j: (i, j, 0))
    vec = pl.BlockSpec((1, 1, d), lambda i, j: (i, 0, 0))
    return pl.pallas_call(
        _merge_kernel,
        grid=(b, s // tm),
        in_specs=[row(d), vec, row(W_BRANCH), row(W_BRANCH), row(W_BRANCH), row(3 * d),
                  _const_spec(wa.shape), _const_spec(wb.shape), _const_spec(wc.shape),
                  _const_spec(wo.shape)],
        out_specs=row(d),
        out_shape=jax.ShapeDtypeStruct((b, s, d), F32),
        compiler_params=pltpu.CompilerParams(
            dimension_semantics=("arbitrary", "arbitrary"), vmem_limit_bytes=VMEM_LIMIT),
        name="merge",
    )(x, gate1, oa, ob, oc, gates, wa, wb, wc, wo)


def _ffn_kernel(x_ref, shift_ref, scale_ref, gate_ref, g2_ref, w1_ref, w2_ref, gf_ref, o_ref, *,
                final_norm):
    x = x_ref[0]
    h = (_rms(x, g2_ref[...]) * (1.0 + scale_ref[0]) + shift_ref[0]).astype(BF16)
    acc = jnp.zeros_like(x)
    chunk = D_MODEL
    for c in range(D_FF // chunk):
        u = jnp.maximum(_dot(h, w1_ref[:, c * chunk:(c + 1) * chunk]), 0.0)
        acc = acc + _dot((u * u).astype(BF16), w2_ref[c * chunk:(c + 1) * chunk, :])
    y = x + gate_ref[0] * acc
    if final_norm:
        y = _rms(y, gf_ref[...])
    o_ref[0] = y


def _ffn(x, shift, scale, gate, g2, w1, w2, gf, final_norm):
    b, s, d = x.shape
    tm = TM_DENSE
    row = pl.BlockSpec((1, tm, d), lambda i, j: (i, j, 0))
    vec = pl.BlockSpec((1, 1, d), lambda i, j: (i, 0, 0))
    return pl.pallas_call(
        functools.partial(_ffn_kernel, final_norm=final_norm),
        grid=(b, s // tm),
        in_specs=[row, vec, vec, vec, _const_spec((1, d)), _const_spec(w1.shape),
                  _const_spec(w2.shape), _const_spec((1, d))],
        out_specs=row,
        out_shape=jax.ShapeDtypeStruct((b, s, d), F32),
        compiler_params=pltpu.CompilerParams(
            dimension_semantics=("arbitrary", "arbitrary"), vmem_limit_bytes=VMEM_LIMIT),
        name="ffn",
    )(x, shift, scale, gate, g2, w1, w2, gf)


def _rope_table(positions):
    pos = positions.astype(F32)[..., None]
    parts = []
    for rot_dim in (MLA_ROPE, MOBA_ROT):
        inv = ROPE_THETA ** (-jnp.arange(0, rot_dim, 2, dtype=F32) / rot_dim)
        parts += [jnp.cos(pos * inv), jnp.sin(pos * inv)]
    used = MLA_ROPE + MOBA_ROT
    return jnp.pad(jnp.concatenate(parts, axis=-1), ((0, 0), (0, 0), (0, LANE - used)))


def _layer_weights(w_in, w_uq, w_ukv):
    wlat = w_in[:, :C_KPE].astype(BF16)
    wkpe = jnp.pad(w_in[:, C_KPE:C_REST], ((0, 0), (MLA_NOPE, LANE - MLA_QK))).astype(BF16)
    wrest = w_in[:, C_REST:].astype(BF16)
    wuq = jnp.pad(w_uq.reshape(MLA_Q_RANK, N_HEADS, MLA_QK),
                  ((0, 0), (0, 0), (0, LANE - MLA_QK))).reshape(MLA_Q_RANK, N_HEADS * LANE)
    ukv = w_ukv.reshape(MLA_KV_RANK, N_HEADS, MLA_NOPE + MLA_V)
    wk = jnp.pad(ukv[:, :, :MLA_NOPE], ((0, 0), (0, 0), (0, LANE - MLA_NOPE)))
    wukv = jnp.concatenate([wk.reshape(MLA_KV_RANK, N_HEADS * LANE),
                            ukv[:, :, MLA_NOPE:].reshape(MLA_KV_RANK, N_HEADS * MLA_V)], axis=1)
    return wlat, wkpe, wrest, wuq.astype(BF16), wukv.astype(BF16)


def kernel(x, c, positions, w_ada, b_ada, norm1_g, norm2_g, w_in, q_norm_g, w_uq, kv_norm_g, w_ukv,
           w_o_mla, w_o_sb, w_o_moba, w_out, w_ff1, w_ff2, final_norm_g):
    b, s, d = x.shape
    depth = w_ada.shape[0]
    assert d == D_MODEL and s % TM_DENSE == 0 and w_in.shape[-1] == D_IN
    rope = _rope_table(positions)
    mod = _adaln_mod(c, w_ada, b_ada)
    gf = final_norm_g.reshape(1, d)
    for l in range(depth):
        shift1, scale1, gate1, shift2, scale2, gate2 = [
            m.reshape(b, 1, d) for m in jnp.split(mod[l], 6, axis=-1)]
        wlat, wkpe, wrest, wuq, wukv = _layer_weights(w_in[l], w_uq[l], w_ukv[l])
        mla, sb, mb, gates = _inproj(
            x, shift1, scale1, norm1_g[l].reshape(1, d), wlat, wkpe, wrest,
            q_norm_g[l].reshape(1, -1), wuq, kv_norm_g[l].reshape(1, -1), wukv, rope)
        o_mla = _attention("mla", mla, N_HEADS * LANE, 4)
        o_sb = _attention("sb", sb, W_BRANCH, 2)
        o_mb = _attention("moba", mb, W_BRANCH, 2)
        x = _merge(x, gate1, o_mla, o_sb, o_mb, gates, w_o_mla[l].astype(BF16),
                   w_o_sb[l].astype(BF16), w_o_moba[l].astype(BF16), w_out[l].astype(BF16))
        x = _ffn(x, shift2, scale2, gate2, norm2_g[l].reshape(1, d), w_ff1[l].astype(BF16),
                 w_ff2[l].astype(BF16), gf, l == depth - 1)
    return x
```

```python
import functools
import math

import jax
import jax.numpy as jnp
from jax import lax
from jax.experimental import pallas as pl
from jax.experimental.pallas import tpu as pltpu

F32 = jnp.float32
BF16 = jnp.bfloat16

D_MODEL = 1024
N_HEADS = 8
MLA_NOPE = 64
MLA_ROPE = 32
MLA_V = 64
MLA_QK = MLA_NOPE + MLA_ROPE
MLA_Q_RANK = 768
MLA_KV_RANK = 256
HEAD = 64
W_BRANCH = N_HEADS * HEAD
MOBA_BLOCK = 256
MOBA_TOPK = 3
MOBA_ROT = HEAD // 4
ROPE_THETA = 500000.0
D_FF = 4 * D_MODEL
EPS = 1e-6
NEG_INF = -1e30
SB_DEAD = -104.0
LOG2E = 1.4426950408889634

LANE = 128
TQ = MOBA_BLOCK
TM_DENSE = 2 * TQ
TN_MOD = 1536
BF16_ROWS = 16
HEAD_L = HEAD + BF16_ROWS
VMEM_LIMIT = 56 * 1024 * 1024

C_KPE = MLA_Q_RANK + MLA_KV_RANK
C_REST = C_KPE + MLA_ROPE
R_MB = 3 * W_BRANCH
R_GATE = R_MB + 3 * W_BRANCH
D_IN = C_REST + R_GATE + 3 * D_MODEL
W_MLA_OUT = 2 * N_HEADS * LANE + N_HEADS * MLA_V


def _rms(xf, g):
    return xf * lax.rsqrt(jnp.mean(xf * xf, axis=-1, keepdims=True) + EPS) * g


def _dot(a, b):
    return jnp.dot(a, b, preferred_element_type=F32)


def _rope_lanes(xb, cos, sin, half, first_half):
    rot = jnp.where(first_half, pltpu.roll(xb, LANE - half, 1), pltpu.roll(xb, half, 1))
    return xb * cos + rot * sin


def _mod_kernel(c_ref, w_ref, b_ref, o_ref):
    c = c_ref[...]
    c_act = c * (1.0 / (1.0 + jnp.exp(-c)))
    o_ref[0] = _dot(c_act.astype(BF16), w_ref[0].astype(BF16)) + b_ref[0]


def _expand_rope(t, lane):
    def put(table, lo, width, src):
        moved = pltpu.roll(t, (lo - src) % LANE, 1)
        return jnp.where((lane >= lo) & (lane < lo + width), moved, table)

    h1, h2 = MLA_ROPE // 2, MOBA_ROT // 2
    one, zero = jnp.ones_like(t), jnp.zeros_like(t)
    x1 = MLA_NOPE
    cosa = put(put(one, x1, h1, 0), x1 + h1, h1, 0)
    sina = put(zero, x1 + h1, h1, h1) - put(zero, x1, h1, h1)
    cosm, sinp, sinn = one, zero, zero
    for head in range(LANE // HEAD):
        cosm = put(put(cosm, head * HEAD, h2, 2 * h1), head * HEAD + h2, h2, 2 * h1)
        sinn = put(sinn, head * HEAD, h2, 2 * h1 + h2)
        sinp = put(sinp, head * HEAD + h2, h2, 2 * h1 + h2)
    return cosa, sina, cosm, sinp - sinn


def _adaln_mod(c, w_ada, b_ada):
    depth, d, n = w_ada.shape
    b = c.shape[0]
    tn = TN_MOD
    return pl.pallas_call(
        _mod_kernel,
        grid=(depth, n // tn),
        in_specs=[
            pl.BlockSpec((b, d), lambda l, j: (0, 0)),
            pl.BlockSpec((1, d, tn), lambda l, j: (l, 0, j)),
            pl.BlockSpec((1, 1, tn), lambda l, j: (l, 0, j)),
        ],
        out_specs=pl.BlockSpec((1, b, tn), lambda l, j: (l, 0, j)),
        out_shape=jax.ShapeDtypeStruct((depth, b, n), F32),
        compiler_params=pltpu.CompilerParams(
            dimension_semantics=("arbitrary", "arbitrary"), vmem_limit_bytes=VMEM_LIMIT),
        name="adaln_mod",
    )(c, w_ada, b_ada.reshape(depth, 1, n))


def _inproj_kernel(x_ref, shift_ref, scale_ref, g1_ref, wlat_ref, wkpe_ref, w_ref, gq_ref, wuq_ref,
                   gkv_ref, wukv_ref, rope_ref, mla_ref, sb_ref, mb_ref, gate_ref):
    x = x_ref[0]
    h = _rms(x, g1_ref[...]) * (1.0 + scale_ref[0]) + shift_ref[0]
    h = h.astype(BF16)
    lane = lax.broadcasted_iota(jnp.int32, (1, LANE), 1)

    cosa, sina, cosm, sinm = _expand_rope(rope_ref[0], lane)
    mla_first = lane < MLA_NOPE + MLA_ROPE // 2
    mb_first = (lane % HEAD) < MOBA_ROT // 2
    mla_scale = LOG2E / math.sqrt(MLA_QK)
    sb_scale = 1.0 / math.sqrt(HEAD)

    qlat = _dot(h, wlat_ref[:, :MLA_Q_RANK])
    ckv = _dot(h, wlat_ref[:, MLA_Q_RANK:])
    kpe = _dot(h, wkpe_ref[...])

    sb_ref[0, :, :W_BRANCH] = (_dot(h, w_ref[:, :W_BRANCH]) * sb_scale).astype(BF16)
    sb_ref[0, :, W_BRANCH:] = _dot(h, w_ref[:, W_BRANCH:R_MB]).astype(BF16)

    q = _dot(_rms(qlat, gq_ref[...]).astype(BF16), wuq_ref[...])
    kv = _dot(_rms(ckv, gkv_ref[...]).astype(BF16), wukv_ref[...])

    for part, scale in ((0, sb_scale * LOG2E), (1, 1.0)):
        c0 = R_MB + part * W_BRANCH
        acc = _dot(h, w_ref[:, c0:c0 + W_BRANCH])
        for cb in range(W_BRANCH // LANE):
            blk = _rope_lanes(acc[:, cb * LANE:(cb + 1) * LANE], cosm, sinm, MOBA_ROT // 2, mb_first)
            mb_ref[0, :, part * W_BRANCH + cb * LANE:part * W_BRANCH + (cb + 1) * LANE] = (
                blk * scale).astype(BF16)
    mb_ref[0, :, 2 * W_BRANCH:] = _dot(h, w_ref[:, R_MB + 2 * W_BRANCH:R_GATE]).astype(BF16)

    kpe = _rope_lanes(kpe, cosa, sina, MLA_ROPE // 2, mla_first)
    k_off = N_HEADS * LANE
    for hd in range(N_HEADS):
        qb = _rope_lanes(q[:, hd * LANE:(hd + 1) * LANE], cosa, sina, MLA_ROPE // 2, mla_first)
        mla_ref[0, :, hd * LANE:(hd + 1) * LANE] = (qb * mla_scale).astype(BF16)
        kb = kv[:, hd * LANE:(hd + 1) * LANE] + kpe
        mla_ref[0, :, k_off + hd * LANE:k_off + (hd + 1) * LANE] = kb.astype(BF16)
    mla_ref[0, :, 2 * k_off:] = kv[:, k_off:].astype(BF16)

    for br in range(3):
        c0 = R_GATE + br * D_MODEL
        gl = _dot(h, w_ref[:, c0:c0 + D_MODEL])
        gate_ref[0, :, br * D_MODEL:(br + 1) * D_MODEL] = (1.0 / (1.0 + jnp.exp(-gl))).astype(BF16)


def _const_spec(shape):
    nd = len(shape)
    return pl.BlockSpec(shape, lambda *_: (0,) * nd, pipeline_mode=pl.Buffered(1))


def _inproj(x, shift, scale, g1, wlat, wkpe, wrest, gq, wuq, gkv, wukv, rope):
    b, s, d = x.shape
    tm = TQ
    row = lambda w: pl.BlockSpec((1, tm, w), lambda i, j: (i, j, 0))
    vec = pl.BlockSpec((1, 1, d), lambda i, j: (i, 0, 0))
    return pl.pallas_call(
        _inproj_kernel,
        grid=(b, s // tm),
        in_specs=[row(d), vec, vec, _const_spec((1, d)), _const_spec(wlat.shape),
                  _const_spec(wkpe.shape), _const_spec(wrest.shape), _const_spec(gq.shape),
                  _const_spec(wuq.shape), _const_spec(gkv.shape), _const_spec(wukv.shape),
                  row(LANE)],
        out_specs=[row(W_MLA_OUT), row(3 * W_BRANCH), row(3 * W_BRANCH), row(3 * D_MODEL)],
        out_shape=[jax.ShapeDtypeStruct((b, s, W_MLA_OUT), BF16),
                   jax.ShapeDtypeStruct((b, s, 3 * W_BRANCH), BF16),
                   jax.ShapeDtypeStruct((b, s, 3 * W_BRANCH), BF16),
                   jax.ShapeDtypeStruct((b, s, 3 * D_MODEL), BF16)],
        compiler_params=pltpu.CompilerParams(
            dimension_semantics=("arbitrary", "arbitrary"), vmem_limit_bytes=VMEM_LIMIT),
        name="inproj",
    )(x, shift, scale, g1, wlat, wkpe, wrest, gq, wuq, gkv, wukv, rope)


def _transpose_values(v_ref, vt_ref, nkv, stride):
    for j in range(nkv):
        for c in range(W_BRANCH // LANE):
            blk = v_ref[0, j * TQ:(j + 1) * TQ, c * LANE:(c + 1) * LANE].astype(F32).T.astype(BF16)
            for a in range(2):
                h = 2 * c + a
                vt_ref[j, h * stride:h * stride + HEAD, :] = blk[a * HEAD:(a + 1) * HEAD, :]
                if stride > HEAD:
                    vt_ref[j, h * stride + HEAD:(h + 1) * stride, :] = jnp.ones((stride - HEAD, TQ), BF16)


def _store_heads(o_ref, acc_ref, stride, row0=0):
    for c in range(W_BRANCH // LANE):
        parts = []
        for h in (2 * c, 2 * c + 1):
            part = acc_ref[h * stride:h * stride + HEAD, :]
            if stride > HEAD:
                part = part / acc_ref[h * stride + HEAD:h * stride + HEAD + 1, :]
            parts.append(part)
        o_ref[0, row0:row0 + TQ, c * LANE:(c + 1) * LANE] = (
            jnp.concatenate(parts, axis=0).T.astype(BF16))


def _stack_pair_queries(q_ref, qbd_ref, row0=0):
    pairs, w, _ = qbd_ref.shape
    zero = jnp.zeros((w // 2, TQ), BF16)
    for p in range(pairs):
        qt = q_ref[0, row0:row0 + TQ, p * w:(p + 1) * w].astype(F32).T.astype(BF16)
        qbd_ref[p, :w // 2, :TQ] = qt[:w // 2, :]
        qbd_ref[p, :w // 2, TQ:] = zero
        qbd_ref[p, w // 2:, :TQ] = zero
        qbd_ref[p, w // 2:, TQ:] = qt[w // 2:, :]


def _pair_scores(k_ref, qbd_ref, row0, p):
    w = qbd_ref.shape[1]
    return _dot(k_ref[0, pl.ds(row0, TQ), p * w:(p + 1) * w], qbd_ref[p])


def _causal(strict):
    key = lax.broadcasted_iota(jnp.int32, (TQ, TQ), 0)
    qry = lax.broadcasted_iota(jnp.int32, (TQ, TQ), 1)
    return key < qry if strict else key <= qry


def _stage_pair(k_ref, qbd_ref, s_ref, mx_ref, j, p, mask):
    pair = _pair_scores(k_ref, qbd_ref, pl.multiple_of(j * TQ, TQ), p)
    for a in range(2):
        h = 2 * p + a
        s = mask(h, pair[:, a * TQ:(a + 1) * TQ])
        s_ref[p, :, a * TQ:(a + 1) * TQ] = s
        mx_ref[h:h + 1, :] = jnp.max(s, axis=0, keepdims=True)


def _update_pair(s_ref, mx_ref, vt_blk, m_ref, acc_ref, p, first):
    for a in range(2):
        h = 2 * p + a
        rows = slice(h * HEAD_L, (h + 1) * HEAD_L)
        m_new = mx_ref[h:h + 1, :]
        if not first:
            m_old = m_ref[h:h + 1, :]
            m_new = jnp.maximum(m_old, m_new)
            alpha = jnp.exp2(m_old - m_new)
        pr = jnp.exp2(s_ref[p, :, a * TQ:(a + 1) * TQ] - m_new)
        m_ref[h:h + 1, :] = m_new
        o = _dot(vt_blk[rows, :], pr.astype(BF16))
        acc_ref[rows, :] = o if first else alpha * acc_ref[rows, :] + o


def _paired_sweep(t, stage, update, tile_by_tile):
    pairs = range(N_HEADS // 2)
    tiles = (0, 1)

    def step(nxt_slot, nxt_j, cur_slot, cur_j, first):
        for p in pairs:
            if tile_by_tile:
                for u in tiles:
                    stage(u, nxt_slot, nxt_j, False, p)
                    update(u, cur_slot, cur_j[u], first, p)
            else:
                for u in tiles:
                    stage(u, nxt_slot, nxt_j, False, p)
                for u in tiles:
                    update(u, cur_slot, cur_j[u], first, p)

    for p in pairs:
        for u in tiles:
            stage(u, 0, 2 * t + u, True, p)
    step(1, 0, 0, (2 * t, 2 * t + 1), True)

    def two_blocks(i, carry):
        j = 2 * i
        step(0, j + 1, 1, (j, j), False)
        step(1, j + 2, 0, (j + 1, j + 1), False)
        return carry

    lax.fori_loop(0, t, two_blocks, 0)
    for p in pairs:
        update(1, 1, 2 * t, False, p)


def _mla_kernel(q_ref, k_ref, v_ref, o_ref, vt_ref, qbd_ref, s_ref, mx_ref, m_ref, acc_ref, *, nkv):
    t = pl.program_id(1)

    @pl.when(t == 0)
    def _():
        _transpose_values(v_ref, vt_ref, nkv, HEAD_L)

    for u in range(2):
        _stack_pair_queries(q_ref, qbd_ref.at[u], u * TQ)

    def stage(u, slot, j, diag, p):
        mask = (lambda h, s: jnp.where(_causal(False), s, NEG_INF)) if diag else (lambda h, s: s)
        _stage_pair(k_ref, qbd_ref.at[u], s_ref.at[slot, u], mx_ref.at[slot, u], j, p, mask)

    def update(u, slot, j, first, p):
        _update_pair(s_ref.at[slot, u], mx_ref.at[slot, u], vt_ref.at[j], m_ref.at[u], acc_ref.at[u],
                     p, first)

    _paired_sweep(t, stage, update, tile_by_tile=False)
    for u in range(2):
        _store_heads(o_ref, acc_ref.at[u], HEAD_L, u * TQ)


def _sb_kernel(q_ref, k_ref, v_ref, o_ref, vt_ref, qbd_ref, run_ref, acc_ref, *, nkv):
    qi = pl.program_id(1)

    @pl.when(qi == 0)
    def _():
        _transpose_values(v_ref, vt_ref, nkv, HEAD)

    _stack_pair_queries(q_ref, qbd_ref)

    def block(j, first):
        row0 = pl.multiple_of(j * TQ, TQ)
        suffix = jnp.where(_causal(False), 1.0, 0.0).astype(BF16)
        zs = [_pair_scores(k_ref, qbd_ref, row0, p) for p in range(N_HEADS // 2)]
        zs = [zs[h // 2][:, (h % 2) * TQ:(h % 2 + 1) * TQ] for h in range(N_HEADS)]
        incls = []
        for h in range(N_HEADS):
            z = zs[h]
            neg_abs = lax.bitcast_convert_type(
                lax.bitcast_convert_type(z, jnp.uint32) | jnp.uint32(0x80000000), F32)
            drop = jnp.maximum(z, 0.0) + jnp.log(1.0 + jnp.exp(neg_abs))
            if first:
                drop = jnp.where(_causal(True), drop, 0.0)
            hi = lax.bitcast_convert_type(
                lax.bitcast_convert_type(drop, jnp.uint32) & jnp.uint32(0xFFFF0000), F32)
            both = _dot(suffix, jnp.concatenate([hi.astype(BF16), (drop - hi).astype(BF16)], axis=1))
            incls.append(both[:, :TQ] + both[:, TQ:])
        for h in range(N_HEADS):
            rows = slice(h * HEAD, (h + 1) * HEAD)
            if first:
                w = jnp.where(_causal(True), jnp.exp(zs[h] - incls[h]), 0.0)
                run_ref[h:h + 1, :] = -incls[h][0:1, :]
                acc_ref[rows, :] = _dot(vt_ref[j, rows, :], w.astype(BF16))
            else:
                run = run_ref[h:h + 1, :]
                w = jnp.exp((zs[h] + run) - incls[h])
                run_ref[h:h + 1, :] = run - incls[h][0:1, :]
                acc_ref[rows, :] = acc_ref[rows, :] + _dot(vt_ref[j, rows, :], w.astype(BF16))

    def live(carry):
        i, top = carry
        return (i < qi) & (top > SB_DEAD)

    def body(carry):
        i, _ = carry
        block(qi - 1 - i, False)
        return i + 1, jnp.max(run_ref[...])

    block(qi, True)
    lax.while_loop(live, body, (jnp.int32(0), jnp.max(run_ref[...])))
    _store_heads(o_ref, acc_ref, HEAD)


def _top_blocks(gate, blk, own):
    taken = -3.0e38
    sel = jnp.zeros(gate.shape, F32)
    for r in range(MOBA_TOPK):
        best = jnp.max(gate, axis=0, keepdims=True)
        first = jnp.min(jnp.where(gate == best, blk, gate.shape[0]), axis=0, keepdims=True)
        chosen = blk == first
        sel = jnp.where(chosen, jnp.where(r < own, 1.0, 0.0), sel)
        gate = jnp.where(chosen, taken, gate)
    return sel


def _moba_kernel(q_ref, k_ref, v_ref, o_ref, vt_ref, qbd_ref, s_ref, mx_ref, m_ref, acc_ref, km_ref,
                 sel_ref, *, nkv, nbp):
    t = pl.program_id(1)
    lane = lax.broadcasted_iota(jnp.int32, (1, W_BRANCH), 1)

    @pl.when(t == 0)
    def _():
        _transpose_values(v_ref, vt_ref, nkv, HEAD_L)
        km_ref[...] = jnp.zeros_like(km_ref)
        for j in range(nkv):
            mean = jnp.mean(k_ref[0, j * TQ:(j + 1) * TQ, :].astype(F32), axis=0, keepdims=True)
            for h in range(N_HEADS):
                in_head = (lane >= h * HEAD) & (lane < (h + 1) * HEAD)
                km_ref[h * nbp + j:h * nbp + j + 1, :] = jnp.where(in_head, mean, 0.0)

    blk = lax.broadcasted_iota(jnp.int32, (nbp, TQ), 0)
    for u in range(2):
        _stack_pair_queries(q_ref, qbd_ref.at[u], u * TQ)
        own = 2 * t + u
        gates = lax.dot_general(km_ref[...], q_ref[0, u * TQ:(u + 1) * TQ, :].astype(F32),
                                (((1,), (1,)), ((), ())), precision=lax.Precision.HIGHEST,
                                preferred_element_type=F32)
        for h in range(N_HEADS):
            gate = jnp.where(blk < own, gates[h * nbp:(h + 1) * nbp, :], NEG_INF)
            sel_ref[u, h * nbp:(h + 1) * nbp, :] = _top_blocks(gate, blk, own)

    def stage(u, slot, j, diag, p):
        if diag:
            mask = lambda h, s: jnp.where(_causal(False), s, NEG_INF)
        else:
            mask = lambda h, s: jnp.where(sel_ref[u, pl.ds(h * nbp + j, 1), :] > 0.5, s, NEG_INF)
        _stage_pair(k_ref, qbd_ref.at[u], s_ref.at[slot, u], mx_ref.at[slot, u], j, p, mask)

    def update(u, slot, j, first, p):
        _update_pair(s_ref.at[slot, u], mx_ref.at[slot, u], vt_ref.at[j], m_ref.at[u], acc_ref.at[u],
                     p, first)

    _paired_sweep(t, stage, update, tile_by_tile=True)
    for u in range(2):
        _store_heads(o_ref, acc_ref.at[u], HEAD_L, u * TQ)


def _attention(kind, src, qk_width, v_col):
    b, s, _ = src.shape
    nkv = s // TQ
    pair_w = 2 * qk_width // N_HEADS
    if kind == "sb":
        tiles = 1
        rows = N_HEADS * HEAD
        body = functools.partial(_sb_kernel, nkv=nkv)
        scratch = [pltpu.VMEM((nkv, rows, TQ), BF16), pltpu.VMEM((N_HEADS // 2, pair_w, 2 * TQ), BF16),
                   pltpu.VMEM((N_HEADS, TQ), F32), pltpu.VMEM((rows, TQ), F32)]
    else:
        tiles = 2
        rows = N_HEADS * HEAD_L
        scratch = [pltpu.VMEM((nkv, rows, TQ), BF16),
                   pltpu.VMEM((tiles, N_HEADS // 2, pair_w, 2 * TQ), BF16),
                   pltpu.VMEM((2, tiles, N_HEADS // 2, TQ, 2 * TQ), F32),
                   pltpu.VMEM((2, tiles, N_HEADS, TQ), F32),
                   pltpu.VMEM((tiles, N_HEADS, TQ), F32),
                   pltpu.VMEM((tiles, rows, TQ), F32)]
        if kind == "mla":
            body = functools.partial(_mla_kernel, nkv=nkv)
        else:
            nbp = -(-nkv // 8) * 8
            body = functools.partial(_moba_kernel, nkv=nkv, nbp=nbp)
            scratch += [pltpu.VMEM((N_HEADS * nbp, W_BRANCH), F32),
                        pltpu.VMEM((tiles, N_HEADS * nbp, TQ), F32)]
    tq = tiles * TQ
    return pl.pallas_call(
        body,
        grid=(b, s // tq),
        in_specs=[
            pl.BlockSpec((1, tq, qk_width), lambda i, t: (i, t, 0)),
            pl.BlockSpec((1, s, qk_width), lambda i, t: (i, 0, 1)),
            pl.BlockSpec((1, s, W_BRANCH), lambda i, t: (i, 0, v_col)),
        ],
        out_specs=pl.BlockSpec((1, tq, W_BRANCH), lambda i, t: (i, t, 0)),
        out_shape=jax.ShapeDtypeStruct((b, s, W_BRANCH), BF16),
        scratch_shapes=scratch,
        compiler_params=pltpu.CompilerParams(
            dimension_semantics=("arbitrary", "arbitrary"), vmem_limit_bytes=VMEM_LIMIT),
        name="attn_" + kind,
    )(src, src, src)


def _tail_kernel(x_ref, gate1_ref, oa_ref, ob_ref, oc_ref, g_ref, wa_ref, wb_ref, wc_ref, wo_ref,
                 shift_ref, scale_ref, gate2_ref, g2_ref, w1_ref, w2_ref, gf_ref, o_ref, *, final_norm):
    merged = g_ref[0, :, :D_MODEL].astype(F32) * _dot(oa_ref[0], wa_ref[...])
    merged = merged + g_ref[0, :, D_MODEL:2 * D_MODEL].astype(F32) * _dot(ob_ref[0], wb_ref[...])
    merged = merged + g_ref[0, :, 2 * D_MODEL:].astype(F32) * _dot(oc_ref[0], wc_ref[...])
    x = x_ref[0] + gate1_ref[0] * _dot(merged.astype(BF16), wo_ref[...])

    h = (_rms(x, g2_ref[...]) * (1.0 + scale_ref[0]) + shift_ref[0]).astype(BF16)
    acc = jnp.zeros_like(x)
    chunk = D_MODEL
    for c in range(D_FF // chunk):
        u = jnp.maximum(_dot(h, w1_ref[:, c * chunk:(c + 1) * chunk]), 0.0)
        acc = acc + _dot((u * u).astype(BF16), w2_ref[c * chunk:(c + 1) * chunk, :])
    y = x + gate2_ref[0] * acc
    if final_norm:
        y = _rms(y, gf_ref[...])
    o_ref[0] = y


def _tail(x, gate1, oa, ob, oc, gates, wa, wb, wc, wo, shift, scale, gate2, g2, w1, w2, gf, final_norm):
    b, s, d = x.shape
    tm = TM_DENSE
    row = lambda w: pl.BlockSpec((1, tm, w), lambda i, j: (i, j, 0))
    vec = pl.BlockSpec((1, 1, d), lambda i, j: (i, 0, 0))
    return pl.pallas_call(
        functools.partial(_tail_kernel, final_norm=final_norm),
        grid=(b, s // tm),
        in_specs=[row(d), vec, row(W_BRANCH), row(W_BRANCH), row(W_BRANCH), row(3 * d),
                  _const_spec(wa.shape), _const_spec(wb.shape), _const_spec(wc.shape),
                  _const_spec(wo.shape), vec, vec, vec, _const_spec((1, d)), _const_spec(w1.shape),
                  _const_spec(w2.shape), _const_spec((1, d))],
        out_specs=row(d),
        out_shape=jax.ShapeDtypeStruct((b, s, d), F32),
        compiler_params=pltpu.CompilerParams(
            dimension_semantics=("arbitrary", "arbitrary"), vmem_limit_bytes=VMEM_LIMIT),
        name="tail",
    )(x, gate1, oa, ob, oc, gates, wa, wb, wc, wo, shift, scale, gate2, g2, w1, w2, gf)


def _rope_table(positions):
    pos = positions.astype(F32)[..., None]
    parts = []
    for rot_dim in (MLA_ROPE, MOBA_ROT):
        inv = ROPE_THETA ** (-jnp.arange(0, rot_dim, 2, dtype=F32) / rot_dim)
        parts += [jnp.cos(pos * inv), jnp.sin(pos * inv)]
    used = MLA_ROPE + MOBA_ROT
    return jnp.pad(jnp.concatenate(parts, axis=-1), ((0, 0), (0, 0), (0, LANE - used)))


def _layer_weights(w_in, w_uq, w_ukv):
    wlat = w_in[:, :C_KPE].astype(BF16)
    wkpe = jnp.pad(w_in[:, C_KPE:C_REST], ((0, 0), (MLA_NOPE, LANE - MLA_QK))).astype(BF16)
    wrest = w_in[:, C_REST:].astype(BF16)
    wuq = jnp.pad(w_uq.reshape(MLA_Q_RANK, N_HEADS, MLA_QK),
                  ((0, 0), (0, 0), (0, LANE - MLA_QK))).reshape(MLA_Q_RANK, N_HEADS * LANE)
    ukv = w_ukv.reshape(MLA_KV_RANK, N_HEADS, MLA_NOPE + MLA_V)
    wk = jnp.pad(ukv[:, :, :MLA_NOPE], ((0, 0), (0, 0), (0, LANE - MLA_NOPE)))
    wukv = jnp.concatenate([wk.reshape(MLA_KV_RANK, N_HEADS * LANE),
                            ukv[:, :, MLA_NOPE:].reshape(MLA_KV_RANK, N_HEADS * MLA_V)], axis=1)
    return wlat, wkpe, wrest, wuq.astype(BF16), wukv.astype(BF16)


def kernel(x, c, positions, w_ada, b_ada, norm1_g, norm2_g, w_in, q_norm_g, w_uq, kv_norm_g, w_ukv,
           w_o_mla, w_o_sb, w_o_moba, w_out, w_ff1, w_ff2, final_norm_g):
    b, s, d = x.shape
    depth = w_ada.shape[0]
    assert d == D_MODEL and s % TM_DENSE == 0 and w_in.shape[-1] == D_IN
    rope = _rope_table(positions)
    mod = _adaln_mod(c, w_ada, b_ada)
    gf = final_norm_g.reshape(1, d)
    for l in range(depth):
        shift1, scale1, gate1, shift2, scale2, gate2 = [
            m.reshape(b, 1, d) for m in jnp.split(mod[l], 6, axis=-1)]
        wlat, wkpe, wrest, wuq, wukv = _layer_weights(w_in[l], w_uq[l], w_ukv[l])
        mla, sb, mb, gates = _inproj(
            x, shift1, scale1, norm1_g[l].reshape(1, d), wlat, wkpe, wrest,
            q_norm_g[l].reshape(1, -1), wuq, kv_norm_g[l].reshape(1, -1), wukv, rope)
        o_mla = _attention("mla", mla, N_HEADS * LANE, 4)
        o_sb = _attention("sb", sb, W_BRANCH, 2)
        o_mb = _attention("moba", mb, W_BRANCH, 2)
        x = _tail(x, gate1, o_mla, o_sb, o_mb, gates, w_o_mla[l].astype(BF16), w_o_sb[l].astype(BF16),
                  w_o_moba[l].astype(BF16), w_out[l].astype(BF16), shift2, scale2, gate2,
                  norm2_g[l].reshape(1, d), w_ff1[l].astype(BF16), w_ff2[l].astype(BF16), gf,
                  l == depth - 1)
    return x
```

```python
import functools
import math

import jax
import jax.numpy as jnp
from jax import lax
from jax.experimental import pallas as pl
from jax.experimental.pallas import tpu as pltpu

F32 = jnp.float32
BF16 = jnp.bfloat16

D_MODEL = 1024
N_HEADS = 8
MLA_NOPE = 64
MLA_ROPE = 32
MLA_V = 64
MLA_QK = MLA_NOPE + MLA_ROPE
MLA_Q_RANK = 768
MLA_KV_RANK = 256
HEAD = 64
W_BRANCH = N_HEADS * HEAD
MOBA_BLOCK = 256
MOBA_TOPK = 3
MOBA_ROT = HEAD // 4
ROPE_THETA = 500000.0
D_FF = 4 * D_MODEL
EPS = 1e-6
NEG_INF = -1e30
SB_DEAD = -104.0
LOG2E = 1.4426950408889634

LANE = 128
TQ = MOBA_BLOCK
TM_DENSE = 2 * TQ
TN_MOD = 1536
BF16_ROWS = 16
HEAD_L = HEAD + BF16_ROWS
VMEM_LIMIT = 56 * 1024 * 1024

C_KPE = MLA_Q_RANK + MLA_KV_RANK
C_REST = C_KPE + MLA_ROPE
R_MB = 3 * W_BRANCH
R_GATE = R_MB + 3 * W_BRANCH
D_IN = C_REST + R_GATE + 3 * D_MODEL
W_MLA_OUT = 2 * N_HEADS * LANE + N_HEADS * MLA_V


def _rms(xf, g):
    return xf * lax.rsqrt(jnp.mean(xf * xf, axis=-1, keepdims=True) + EPS) * g


def _dot(a, b):
    return jnp.dot(a, b, preferred_element_type=F32)


def _rope_lanes(xb, cos, sin, half, first_half):
    rot = jnp.where(first_half, pltpu.roll(xb, LANE - half, 1), pltpu.roll(xb, half, 1))
    return xb * cos + rot * sin


def _mod_kernel(c_ref, w_ref, b_ref, o_ref):
    c = c_ref[...]
    c_act = c * (1.0 / (1.0 + jnp.exp(-c)))
    o_ref[0] = _dot(c_act.astype(BF16), w_ref[0].astype(BF16)) + b_ref[0]


def _expand_rope(t, lane):
    def put(table, lo, width, src):
        moved = pltpu.roll(t, (lo - src) % LANE, 1)
        return jnp.where((lane >= lo) & (lane < lo + width), moved, table)

    h1, h2 = MLA_ROPE // 2, MOBA_ROT // 2
    one, zero = jnp.ones_like(t), jnp.zeros_like(t)
    x1 = MLA_NOPE
    cosa = put(put(one, x1, h1, 0), x1 + h1, h1, 0)
    sina = put(zero, x1 + h1, h1, h1) - put(zero, x1, h1, h1)
    cosm, sinp, sinn = one, zero, zero
    for head in range(LANE // HEAD):
        cosm = put(put(cosm, head * HEAD, h2, 2 * h1), head * HEAD + h2, h2, 2 * h1)
        sinn = put(sinn, head * HEAD, h2, 2 * h1 + h2)
        sinp = put(sinp, head * HEAD + h2, h2, 2 * h1 + h2)
    return cosa, sina, cosm, sinp - sinn


def _adaln_mod(c, w_ada, b_ada):
    depth, d, n = w_ada.shape
    b = c.shape[0]
    tn = TN_MOD
    return pl.pallas_call(
        _mod_kernel,
        grid=(depth, n // tn),
        in_specs=[
            pl.BlockSpec((b, d), lambda l, j: (0, 0)),
            pl.BlockSpec((1, d, tn), lambda l, j: (l, 0, j)),
            pl.BlockSpec((1, 1, tn), lambda l, j: (l, 0, j)),
        ],
        out_specs=pl.BlockSpec((1, b, tn), lambda l, j: (l, 0, j)),
        out_shape=jax.ShapeDtypeStruct((depth, b, n), F32),
        compiler_params=pltpu.CompilerParams(
            dimension_semantics=("arbitrary", "arbitrary"), vmem_limit_bytes=VMEM_LIMIT),
        name="adaln_mod",
    )(c, w_ada, b_ada.reshape(depth, 1, n))


def _inproj_kernel(x_ref, shift_ref, scale_ref, g1_ref, wlat_ref, wkpe_ref, w_ref, gq_ref, wuq_ref,
                   gkv_ref, wukv_ref, rope_ref, mla_ref, sb_ref, mb_ref, gate_ref):
    x = x_ref[0]
    h = _rms(x, g1_ref[...]) * (1.0 + scale_ref[0]) + shift_ref[0]
    h = h.astype(BF16)
    lane = lax.broadcasted_iota(jnp.int32, (1, LANE), 1)

    cosa, sina, cosm, sinm = _expand_rope(rope_ref[0], lane)
    mla_first = lane < MLA_NOPE + MLA_ROPE // 2
    mb_first = (lane % HEAD) < MOBA_ROT // 2
    mla_scale = LOG2E / math.sqrt(MLA_QK)
    sb_scale = 1.0 / math.sqrt(HEAD)

    qlat = _dot(h, wlat_ref[:, :MLA_Q_RANK])
    ckv = _dot(h, wlat_ref[:, MLA_Q_RANK:])
    kpe = _dot(h, wkpe_ref[...])

    sb_ref[0, :, :W_BRANCH] = (_dot(h, w_ref[:, :W_BRANCH]) * sb_scale).astype(BF16)
    sb_ref[0, :, W_BRANCH:] = _dot(h, w_ref[:, W_BRANCH:R_MB]).astype(BF16)

    q = _dot(_rms(qlat, gq_ref[...]).astype(BF16), wuq_ref[...])
    kv = _dot(_rms(ckv, gkv_ref[...]).astype(BF16), wukv_ref[...])

    for part, scale in ((0, sb_scale * LOG2E), (1, 1.0)):
        c0 = R_MB + part * W_BRANCH
        acc = _dot(h, w_ref[:, c0:c0 + W_BRANCH])
        for cb in range(W_BRANCH // LANE):
            blk = _rope_lanes(acc[:, cb * LANE:(cb + 1) * LANE], cosm, sinm, MOBA_ROT // 2, mb_first)
            mb_ref[0, :, part * W_BRANCH + cb * LANE:part * W_BRANCH + (cb + 1) * LANE] = (
                blk * scale).astype(BF16)
    mb_ref[0, :, 2 * W_BRANCH:] = _dot(h, w_ref[:, R_MB + 2 * W_BRANCH:R_GATE]).astype(BF16)

    kpe = _rope_lanes(kpe, cosa, sina, MLA_ROPE // 2, mla_first)
    k_off = N_HEADS * LANE
    for hd in range(N_HEADS):
        qb = _rope_lanes(q[:, hd * LANE:(hd + 1) * LANE], cosa, sina, MLA_ROPE // 2, mla_first)
        mla_ref[0, :, hd * LANE:(hd + 1) * LANE] = (qb * mla_scale).astype(BF16)
        kb = kv[:, hd * LANE:(hd + 1) * LANE] + kpe
        mla_ref[0, :, k_off + hd * LANE:k_off + (hd + 1) * LANE] = kb.astype(BF16)
    mla_ref[0, :, 2 * k_off:] = kv[:, k_off:].astype(BF16)

    for br in range(3):
        c0 = R_GATE + br * D_MODEL
        gl = _dot(h, w_ref[:, c0:c0 + D_MODEL])
        gate_ref[0, :, br * D_MODEL:(br + 1) * D_MODEL] = (1.0 / (1.0 + jnp.exp(-gl))).astype(BF16)


def _const_spec(shape):
    nd = len(shape)
    return pl.BlockSpec(shape, lambda *_: (0,) * nd, pipeline_mode=pl.Buffered(1))


def _inproj(x, shift, scale, g1, wlat, wkpe, wrest, gq, wuq, gkv, wukv, rope):
    b, s, d = x.shape
    tm = TQ
    row = lambda w: pl.BlockSpec((1, tm, w), lambda i, j: (i, j, 0))
    vec = pl.BlockSpec((1, 1, d), lambda i, j: (i, 0, 0))
    return pl.pallas_call(
        _inproj_kernel,
        grid=(b, s // tm),
        in_specs=[row(d), vec, vec, _const_spec((1, d)), _const_spec(wlat.shape),
                  _const_spec(wkpe.shape), _const_spec(wrest.shape), _const_spec(gq.shape),
                  _const_spec(wuq.shape), _const_spec(gkv.shape), _const_spec(wukv.shape),
                  row(LANE)],
        out_specs=[row(W_MLA_OUT), row(3 * W_BRANCH), row(3 * W_BRANCH), row(3 * D_MODEL)],
        out_shape=[jax.ShapeDtypeStruct((b, s, W_MLA_OUT), BF16),
                   jax.ShapeDtypeStruct((b, s, 3 * W_BRANCH), BF16),
                   jax.ShapeDtypeStruct((b, s, 3 * W_BRANCH), BF16),
                   jax.ShapeDtypeStruct((b, s, 3 * D_MODEL), BF16)],
        compiler_params=pltpu.CompilerParams(
            dimension_semantics=("arbitrary", "arbitrary"), vmem_limit_bytes=VMEM_LIMIT),
        name="inproj",
    )(x, shift, scale, g1, wlat, wkpe, wrest, gq, wuq, gkv, wukv, rope)


def _transpose_values(v_ref, vt_ref, nkv, stride):
    for j in range(nkv):
        for c in range(W_BRANCH // LANE):
            blk = v_ref[0, j * TQ:(j + 1) * TQ, c * LANE:(c + 1) * LANE].astype(F32).T.astype(BF16)
            for a in range(2):
                h = 2 * c + a
                vt_ref[j, h * stride:h * stride + HEAD, :] = blk[a * HEAD:(a + 1) * HEAD, :]
                if stride > HEAD:
                    vt_ref[j, h * stride + HEAD:(h + 1) * stride, :] = jnp.ones((stride - HEAD, TQ), BF16)


def _store_heads(o_ref, acc_ref, stride, row0=0):
    for c in range(W_BRANCH // LANE):
        parts = []
        for h in (2 * c, 2 * c + 1):
            part = acc_ref[h * stride:h * stride + HEAD, :]
            if stride > HEAD:
                part = part / acc_ref[h * stride + HEAD:h * stride + HEAD + 1, :]
            parts.append(part)
        o_ref[0, row0:row0 + TQ, c * LANE:(c + 1) * LANE] = (
            jnp.concatenate(parts, axis=0).T.astype(BF16))


def _stack_pair_queries(q_ref, qbd_ref, row0=0):
    pairs, w, _ = qbd_ref.shape
    zero = jnp.zeros((w // 2, TQ), BF16)
    for p in range(pairs):
        qt = q_ref[0, row0:row0 + TQ, p * w:(p + 1) * w].astype(F32).T.astype(BF16)
        qbd_ref[p, :w // 2, :TQ] = qt[:w // 2, :]
        qbd_ref[p, :w // 2, TQ:] = zero
        qbd_ref[p, w // 2:, :TQ] = zero
        qbd_ref[p, w // 2:, TQ:] = qt[w // 2:, :]


def _pair_scores(k_ref, qbd_ref, row0, p):
    w = qbd_ref.shape[1]
    return _dot(k_ref[0, pl.ds(row0, TQ), p * w:(p + 1) * w], qbd_ref[p])


def _causal(strict):
    key = lax.broadcasted_iota(jnp.int32, (TQ, TQ), 0)
    qry = lax.broadcasted_iota(jnp.int32, (TQ, TQ), 1)
    return key < qry if strict else key <= qry


def _stage_pair(k_ref, qbd_ref, s_ref, mx_ref, j, p, mask):
    pair = _pair_scores(k_ref, qbd_ref, pl.multiple_of(j * TQ, TQ), p)
    for a in range(2):
        h = 2 * p + a
        s = mask(h, pair[:, a * TQ:(a + 1) * TQ])
        s_ref[p, :, a * TQ:(a + 1) * TQ] = s
        mx_ref[h:h + 1, :] = jnp.max(s, axis=0, keepdims=True)


def _update_pair(s_ref, mx_ref, vt_blk, m_ref, acc_ref, p, first):
    for a in range(2):
        h = 2 * p + a
        rows = slice(h * HEAD_L, (h + 1) * HEAD_L)
        m_new = mx_ref[h:h + 1, :]
        if not first:
            m_old = m_ref[h:h + 1, :]
            m_new = jnp.maximum(m_old, m_new)
            alpha = jnp.exp2(m_old - m_new)
        pr = jnp.exp2(s_ref[p, :, a * TQ:(a + 1) * TQ] - m_new)
        m_ref[h:h + 1, :] = m_new
        o = _dot(vt_blk[rows, :], pr.astype(BF16))
        acc_ref[rows, :] = o if first else alpha * acc_ref[rows, :] + o


def _paired_sweep(t, stage, update, tile_by_tile, double_trips):
    pairs = range(N_HEADS // 2)
    tiles = (0, 1)

    def step(nxt_slot, nxt_j, cur_slot, cur_j, first):
        for p in pairs:
            if tile_by_tile:
                for u in tiles:
                    stage(u, nxt_slot, nxt_j, False, p)
                    update(u, cur_slot, cur_j[u], first, p)
            else:
                for u in tiles:
                    stage(u, nxt_slot, nxt_j, False, p)
                for u in tiles:
                    update(u, cur_slot, cur_j[u], first, p)

    for p in pairs:
        for u in tiles:
            stage(u, 0, 2 * t + u, True, p)
    step(1, 0, 0, (2 * t, 2 * t + 1), True)

    def two_blocks(j):
        step(0, j + 1, 1, (j, j), False)
        step(1, j + 2, 0, (j + 1, j + 1), False)

    if double_trips:
        def four_blocks(i, carry):
            two_blocks(4 * i)
            two_blocks(4 * i + 2)
            return carry

        lax.fori_loop(0, lax.shift_right_logical(t, 1), four_blocks, 0)

        @pl.when((t & 1) == 1)
        def _():
            two_blocks(2 * t - 2)
    else:
        def one_trip(i, carry):
            two_blocks(2 * i)
            return carry

        lax.fori_loop(0, t, one_trip, 0)

    for p in pairs:
        update(1, 1, 2 * t, False, p)


def _mla_kernel(q_ref, k_ref, v_ref, o_ref, vt_ref, qbd_ref, s_ref, mx_ref, m_ref, acc_ref, *, nkv):
    t = pl.program_id(1)

    @pl.when(t == 0)
    def _():
        _transpose_values(v_ref, vt_ref, nkv, HEAD_L)

    for u in range(2):
        _stack_pair_queries(q_ref, qbd_ref.at[u], u * TQ)

    def stage(u, slot, j, diag, p):
        mask = (lambda h, s: jnp.where(_causal(False), s, NEG_INF)) if diag else (lambda h, s: s)
        _stage_pair(k_ref, qbd_ref.at[u], s_ref.at[slot, u], mx_ref.at[slot, u], j, p, mask)

    def update(u, slot, j, first, p):
        _update_pair(s_ref.at[slot, u], mx_ref.at[slot, u], vt_ref.at[j], m_ref.at[u], acc_ref.at[u],
                     p, first)

    _paired_sweep(t, stage, update, tile_by_tile=False, double_trips=True)
    for u in range(2):
        _store_heads(o_ref, acc_ref.at[u], HEAD_L, u * TQ)


def _sb_kernel(q_ref, k_ref, v_ref, o_ref, vt_ref, qbd_ref, run_ref, acc_ref, *, nkv):
    qi = pl.program_id(1)

    @pl.when(qi == 0)
    def _():
        _transpose_values(v_ref, vt_ref, nkv, HEAD)

    _stack_pair_queries(q_ref, qbd_ref)

    def block(j, first):
        row0 = pl.multiple_of(j * TQ, TQ)
        suffix = jnp.where(_causal(False), 1.0, 0.0).astype(BF16)
        zs = [_pair_scores(k_ref, qbd_ref, row0, p) for p in range(N_HEADS // 2)]
        zs = [zs[h // 2][:, (h % 2) * TQ:(h % 2 + 1) * TQ] for h in range(N_HEADS)]
        incls = []
        for h in range(N_HEADS):
            z = zs[h]
            neg_abs = lax.bitcast_convert_type(
                lax.bitcast_convert_type(z, jnp.uint32) | jnp.uint32(0x80000000), F32)
            drop = jnp.maximum(z, 0.0) + jnp.log(1.0 + jnp.exp(neg_abs))
            if first:
                drop = jnp.where(_causal(True), drop, 0.0)
            hi = lax.bitcast_convert_type(
                lax.bitcast_convert_type(drop, jnp.uint32) & jnp.uint32(0xFFFF0000), F32)
            both = _dot(suffix, jnp.concatenate([hi.astype(BF16), (drop - hi).astype(BF16)], axis=1))
            incls.append(both[:, :TQ] + both[:, TQ:])
        for h in range(N_HEADS):
            rows = slice(h * HEAD, (h + 1) * HEAD)
            if first:
                w = jnp.where(_causal(True), jnp.exp(zs[h] - incls[h]), 0.0)
                run_ref[h:h + 1, :] = -incls[h][0:1, :]
                acc_ref[rows, :] = _dot(vt_ref[j, rows, :], w.astype(BF16))
            else:
                run = run_ref[h:h + 1, :]
                w = jnp.exp((zs[h] + run) - incls[h])
                run_ref[h:h + 1, :] = run - incls[h][0:1, :]
                acc_ref[rows, :] = acc_ref[rows, :] + _dot(vt_ref[j, rows, :], w.astype(BF16))

    def live(carry):
        i, top = carry
        return (i < qi) & (top > SB_DEAD)

    def body(carry):
        i, _ = carry
        block(qi - 1 - i, False)
        return i + 1, jnp.max(run_ref[...])

    block(qi, True)
    lax.while_loop(live, body, (jnp.int32(0), jnp.max(run_ref[...])))
    _store_heads(o_ref, acc_ref, HEAD)


def _top_blocks(gate, blk, own):
    taken = -3.0e38
    sel = jnp.zeros(gate.shape, F32)
    for r in range(MOBA_TOPK):
        best = jnp.max(gate, axis=0, keepdims=True)
        first = jnp.min(jnp.where(gate == best, blk, gate.shape[0]), axis=0, keepdims=True)
        chosen = blk == first
        sel = jnp.where(chosen, jnp.where(r < own, 1.0, 0.0), sel)
        gate = jnp.where(chosen, taken, gate)
    return sel


def _moba_kernel(q_ref, k_ref, v_ref, o_ref, vt_ref, qbd_ref, s_ref, mx_ref, m_ref, acc_ref, km_ref,
                 sel_ref, *, nkv, nbp):
    t = pl.program_id(1)
    lane = lax.broadcasted_iota(jnp.int32, (1, W_BRANCH), 1)

    @pl.when(t == 0)
    def _():
        _transpose_values(v_ref, vt_ref, nkv, HEAD_L)
        km_ref[...] = jnp.zeros_like(km_ref)
        for j in range(nkv):
            mean = jnp.mean(k_ref[0, j * TQ:(j + 1) * TQ, :].astype(F32), axis=0, keepdims=True)
            for h in range(N_HEADS):
                in_head = (lane >= h * HEAD) & (lane < (h + 1) * HEAD)
                km_ref[h * nbp + j:h * nbp + j + 1, :] = jnp.where(in_head, mean, 0.0)

    blk = lax.broadcasted_iota(jnp.int32, (nbp, TQ), 0)
    for u in range(2):
        _stack_pair_queries(q_ref, qbd_ref.at[u], u * TQ)
        own = 2 * t + u
        gates = lax.dot_general(km_ref[...], q_ref[0, u * TQ:(u + 1) * TQ, :].astype(F32),
                                (((1,), (1,)), ((), ())), precision=lax.Precision.HIGHEST,
                                preferred_element_type=F32)
        for h in range(N_HEADS):
            gate = jnp.where(blk < own, gates[h * nbp:(h + 1) * nbp, :], NEG_INF)
            sel_ref[u, h * nbp:(h + 1) * nbp, :] = _top_blocks(gate, blk, own)

    def stage(u, slot, j, diag, p):
        if diag:
            mask = lambda h, s: jnp.where(_causal(False), s, NEG_INF)
        else:
            mask = lambda h, s: jnp.where(sel_ref[u, pl.ds(h * nbp + j, 1), :] > 0.5, s, NEG_INF)
        _stage_pair(k_ref, qbd_ref.at[u], s_ref.at[slot, u], mx_ref.at[slot, u], j, p, mask)

    def update(u, slot, j, first, p):
        _update_pair(s_ref.at[slot, u], mx_ref.at[slot, u], vt_ref.at[j], m_ref.at[u], acc_ref.at[u],
                     p, first)

    _paired_sweep(t, stage, update, tile_by_tile=True, double_trips=False)
    for u in range(2):
        _store_heads(o_ref, acc_ref.at[u], HEAD_L, u * TQ)


def _attention(kind, src, qk_width, v_col):
    b, s, _ = src.shape
    nkv = s // TQ
    pair_w = 2 * qk_width // N_HEADS
    if kind == "sb":
        tiles = 1
        rows = N_HEADS * HEAD
        body = functools.partial(_sb_kernel, nkv=nkv)
        scratch = [pltpu.VMEM((nkv, rows, TQ), BF16), pltpu.VMEM((N_HEADS // 2, pair_w, 2 * TQ), BF16),
                   pltpu.VMEM((N_HEADS, TQ), F32), pltpu.VMEM((rows, TQ), F32)]
    else:
        tiles = 2
        rows = N_HEADS * HEAD_L
        scratch = [pltpu.VMEM((nkv, rows, TQ), BF16),
                   pltpu.VMEM((tiles, N_HEADS // 2, pair_w, 2 * TQ), BF16),
                   pltpu.VMEM((2, tiles, N_HEADS // 2, TQ, 2 * TQ), F32),
                   pltpu.VMEM((2, tiles, N_HEADS, TQ), F32),
                   pltpu.VMEM((tiles, N_HEADS, TQ), F32),
                   pltpu.VMEM((tiles, rows, TQ), F32)]
        if kind == "mla":
            body = functools.partial(_mla_kernel, nkv=nkv)
        else:
            nbp = -(-nkv // 8) * 8
            body = functools.partial(_moba_kernel, nkv=nkv, nbp=nbp)
            scratch += [pltpu.VMEM((N_HEADS * nbp, W_BRANCH), F32),
                        pltpu.VMEM((tiles, N_HEADS * nbp, TQ), F32)]
    tq = tiles * TQ
    return pl.pallas_call(
        body,
        grid=(b, s // tq),
        in_specs=[
            pl.BlockSpec((1, tq, qk_width), lambda i, t: (i, t, 0)),
            pl.BlockSpec((1, s, qk_width), lambda i, t: (i, 0, 1)),
            pl.BlockSpec((1, s, W_BRANCH), lambda i, t: (i, 0, v_col)),
        ],
        out_specs=pl.BlockSpec((1, tq, W_BRANCH), lambda i, t: (i, t, 0)),
        out_shape=jax.ShapeDtypeStruct((b, s, W_BRANCH), BF16),
        scratch_shapes=scratch,
        compiler_params=pltpu.CompilerParams(
            dimension_semantics=("arbitrary", "arbitrary"), vmem_limit_bytes=VMEM_LIMIT),
        name="attn_" + kind,
    )(src, src, src)


def _tail_kernel(x_ref, gate1_ref, oa_ref, ob_ref, oc_ref, g_ref, wa_ref, wb_ref, wc_ref, wo_ref,
                 shift_ref, scale_ref, gate2_ref, g2_ref, w1_ref, w2_ref, gf_ref, o_ref, *, final_norm):
    merged = g_ref[0, :, :D_MODEL].astype(F32) * _dot(oa_ref[0], wa_ref[...])
    merged = merged + g_ref[0, :, D_MODEL:2 * D_MODEL].astype(F32) * _dot(ob_ref[0], wb_ref[...])
    merged = merged + g_ref[0, :, 2 * D_MODEL:].astype(F32) * _dot(oc_ref[0], wc_ref[...])
    x = x_ref[0] + gate1_ref[0] * _dot(merged.astype(BF16), wo_ref[...])

    h = (_rms(x, g2_ref[...]) * (1.0 + scale_ref[0]) + shift_ref[0]).astype(BF16)
    acc = jnp.zeros_like(x)
    chunk = D_MODEL
    for c in range(D_FF // chunk):
        u = jnp.maximum(_dot(h, w1_ref[:, c * chunk:(c + 1) * chunk]), 0.0)
        acc = acc + _dot((u * u).astype(BF16), w2_ref[c * chunk:(c + 1) * chunk, :])
    y = x + gate2_ref[0] * acc
    if final_norm:
        y = _rms(y, gf_ref[...])
    o_ref[0] = y


def _tail(x, gate1, oa, ob, oc, gates, wa, wb, wc, wo, shift, scale, gate2, g2, w1, w2, gf, final_norm):
    b, s, d = x.shape
    tm = TM_DENSE
    row = lambda w: pl.BlockSpec((1, tm, w), lambda i, j: (i, j, 0))
    vec = pl.BlockSpec((1, 1, d), lambda i, j: (i, 0, 0))
    return pl.pallas_call(
        functools.partial(_tail_kernel, final_norm=final_norm),
        grid=(b, s // tm),
        in_specs=[row(d), vec, row(W_BRANCH), row(W_BRANCH), row(W_BRANCH), row(3 * d),
                  _const_spec(wa.shape), _const_spec(wb.shape), _const_spec(wc.shape),
                  _const_spec(wo.shape), vec, vec, vec, _const_spec((1, d)), _const_spec(w1.shape),
                  _const_spec(w2.shape), _const_spec((1, d))],
        out_specs=row(d),
        out_shape=jax.ShapeDtypeStruct((b, s, d), F32),
        compiler_params=pltpu.CompilerParams(
            dimension_semantics=("arbitrary", "arbitrary"), vmem_limit_bytes=VMEM_LIMIT),
        name="tail",
    )(x, gate1, oa, ob, oc, gates, wa, wb, wc, wo, shift, scale, gate2, g2, w1, w2, gf)


def _rope_table(positions):
    pos = positions.astype(F32)[..., None]
    parts = []
    for rot_dim in (MLA_ROPE, MOBA_ROT):
        inv = ROPE_THETA ** (-jnp.arange(0, rot_dim, 2, dtype=F32) / rot_dim)
        parts += [jnp.cos(pos * inv), jnp.sin(pos * inv)]
    used = MLA_ROPE + MOBA_ROT
    return jnp.pad(jnp.concatenate(parts, axis=-1), ((0, 0), (0, 0), (0, LANE - used)))


def _layer_weights(w_in, w_uq, w_ukv):
    wlat = w_in[:, :C_KPE].astype(BF16)
    wkpe = jnp.pad(w_in[:, C_KPE:C_REST], ((0, 0), (MLA_NOPE, LANE - MLA_QK))).astype(BF16)
    wrest = w_in[:, C_REST:].astype(BF16)
    wuq = jnp.pad(w_uq.reshape(MLA_Q_RANK, N_HEADS, MLA_QK),
                  ((0, 0), (0, 0), (0, LANE - MLA_QK))).reshape(MLA_Q_RANK, N_HEADS * LANE)
    ukv = w_ukv.reshape(MLA_KV_RANK, N_HEADS, MLA_NOPE + MLA_V)
    wk = jnp.pad(ukv[:, :, :MLA_NOPE], ((0, 0), (0, 0), (0, LANE - MLA_NOPE)))
    wukv = jnp.concatenate([wk.reshape(MLA_KV_RANK, N_HEADS * LANE),
                            ukv[:, :, MLA_NOPE:].reshape(MLA_KV_RANK, N_HEADS * MLA_V)], axis=1)
    return wlat, wkpe, wrest, wuq.astype(BF16), wukv.astype(BF16)


def kernel(x, c, positions, w_ada, b_ada, norm1_g, norm2_g, w_in, q_norm_g, w_uq, kv_norm_g, w_ukv,
           w_o_mla, w_o_sb, w_o_moba, w_out, w_ff1, w_ff2, final_norm_g):
    b, s, d = x.shape
    depth = w_ada.shape[0]
    assert d == D_MODEL and s % TM_DENSE == 0 and w_in.shape[-1] == D_IN
    rope = _rope_table(positions)
    mod = _adaln_mod(c, w_ada, b_ada)
    gf = final_norm_g.reshape(1, d)
    for l in range(depth):
        shift1, scale1, gate1, shift2, scale2, gate2 = [
            m.reshape(b, 1, d) for m in jnp.split(mod[l], 6, axis=-1)]
        wlat, wkpe, wrest, wuq, wukv = _layer_weights(w_in[l], w_uq[l], w_ukv[l])
        mla, sb, mb, gates = _inproj(
            x, shift1, scale1, norm1_g[l].reshape(1, d), wlat, wkpe, wrest,
            q_norm_g[l].reshape(1, -1), wuq, kv_norm_g[l].reshape(1, -1), wukv, rope)
        o_mla = _attention("mla", mla, N_HEADS * LANE, 4)
        o_sb = _attention("sb", sb, W_BRANCH, 2)
        o_mb = _attention("moba", mb, W_BRANCH, 2)
        x = _tail(x, gate1, o_mla, o_sb, o_mb, gates, w_o_mla[l].astype(BF16), w_o_sb[l].astype(BF16),
                  w_o_moba[l].astype(BF16), w_out[l].astype(BF16), shift2, scale2, gate2,
                  norm2_g[l].reshape(1, d), w_ff1[l].astype(BF16), w_ff2[l].astype(BF16), gf,
                  l == depth - 1)
    return x
```

```python
import functools
import math

import jax
import jax.numpy as jnp
from jax import lax
from jax.experimental import pallas as pl
from jax.experimental.pallas import tpu as pltpu

F32 = jnp.float32
BF16 = jnp.bfloat16

D_MODEL = 1024
N_HEADS = 8
MLA_NOPE = 64
MLA_ROPE = 32
MLA_V = 64
MLA_QK = MLA_NOPE + MLA_ROPE
MLA_Q_RANK = 768
MLA_KV_RANK = 256
HEAD = 64
W_BRANCH = N_HEADS * HEAD
MOBA_BLOCK = 256
MOBA_TOPK = 3
MOBA_ROT = HEAD // 4
ROPE_THETA = 500000.0
D_FF = 4 * D_MODEL
EPS = 1e-6
NEG_INF = -1e30
SB_DEAD = -104.0
LOG2E = 1.4426950408889634

LANE = 128
TQ = MOBA_BLOCK
TM_DENSE = 2 * TQ
TN_MOD = 1536
BF16_ROWS = 16
HEAD_L = HEAD + BF16_ROWS
VMEM_LIMIT = 56 * 1024 * 1024

C_KPE = MLA_Q_RANK + MLA_KV_RANK
C_REST = C_KPE + MLA_ROPE
R_MB = 3 * W_BRANCH
R_GATE = R_MB + 3 * W_BRANCH
D_IN = C_REST + R_GATE + 3 * D_MODEL
W_MLA_OUT = 2 * N_HEADS * LANE + N_HEADS * MLA_V


def _rms(xf, g):
    return xf * lax.rsqrt(jnp.mean(xf * xf, axis=-1, keepdims=True) + EPS) * g


def _dot(a, b):
    return jnp.dot(a, b, preferred_element_type=F32)


def _rope_lanes(xb, cos, sin, half, first_half):
    rot = jnp.where(first_half, pltpu.roll(xb, LANE - half, 1), pltpu.roll(xb, half, 1))
    return xb * cos + rot * sin


def _mod_kernel(c_ref, w_ref, b_ref, o_ref):
    c = c_ref[...]
    c_act = c * (1.0 / (1.0 + jnp.exp(-c)))
    o_ref[0] = _dot(c_act.astype(BF16), w_ref[0].astype(BF16)) + b_ref[0]


def _expand_rope(t, lane):
    def put(table, lo, width, src):
        moved = pltpu.roll(t, (lo - src) % LANE, 1)
        return jnp.where((lane >= lo) & (lane < lo + width), moved, table)

    h1, h2 = MLA_ROPE // 2, MOBA_ROT // 2
    one, zero = jnp.ones_like(t), jnp.zeros_like(t)
    x1 = MLA_NOPE
    cosa = put(put(one, x1, h1, 0), x1 + h1, h1, 0)
    sina = put(zero, x1 + h1, h1, h1) - put(zero, x1, h1, h1)
    cosm, sinp, sinn = one, zero, zero
    for head in range(LANE // HEAD):
        cosm = put(put(cosm, head * HEAD, h2, 2 * h1), head * HEAD + h2, h2, 2 * h1)
        sinn = put(sinn, head * HEAD, h2, 2 * h1 + h2)
        sinp = put(sinp, head * HEAD + h2, h2, 2 * h1 + h2)
    return cosa, sina, cosm, sinp - sinn


def _adaln_mod(c, w_ada, b_ada):
    depth, d, n = w_ada.shape
    b = c.shape[0]
    tn = TN_MOD
    return pl.pallas_call(
        _mod_kernel,
        grid=(depth, n // tn),
        in_specs=[
            pl.BlockSpec((b, d), lambda l, j: (0, 0)),
            pl.BlockSpec((1, d, tn), lambda l, j: (l, 0, j)),
            pl.BlockSpec((1, 1, tn), lambda l, j: (l, 0, j)),
        ],
        out_specs=pl.BlockSpec((1, b, tn), lambda l, j: (l, 0, j)),
        out_shape=jax.ShapeDtypeStruct((depth, b, n), F32),
        compiler_params=pltpu.CompilerParams(
            dimension_semantics=("arbitrary", "arbitrary"), vmem_limit_bytes=VMEM_LIMIT),
        name="adaln_mod",
    )(c, w_ada, b_ada.reshape(depth, 1, n))


def _inproj_kernel(x_ref, shift_ref, scale_ref, g1_ref, wlat_ref, wkpe_ref, w_ref, gq_ref, wuq_ref,
                   gkv_ref, wukv_ref, rope_ref, mla_ref, sb_ref, mb_ref, gate_ref):
    x = x_ref[0]
    h = _rms(x, g1_ref[...]) * (1.0 + scale_ref[0]) + shift_ref[0]
    h = h.astype(BF16)
    lane = lax.broadcasted_iota(jnp.int32, (1, LANE), 1)

    cosa, sina, cosm, sinm = _expand_rope(rope_ref[0], lane)
    mla_first = lane < MLA_NOPE + MLA_ROPE // 2
    mb_first = (lane % HEAD) < MOBA_ROT // 2
    mla_scale = LOG2E / math.sqrt(MLA_QK)
    sb_scale = 1.0 / math.sqrt(HEAD)

    qlat = _dot(h, wlat_ref[:, :MLA_Q_RANK])
    ckv = _dot(h, wlat_ref[:, MLA_Q_RANK:])
    kpe = _dot(h, wkpe_ref[...])

    sb_ref[0, :, :W_BRANCH] = (_dot(h, w_ref[:, :W_BRANCH]) * sb_scale).astype(BF16)
    sb_ref[0, :, W_BRANCH:] = _dot(h, w_ref[:, W_BRANCH:R_MB]).astype(BF16)

    q = _dot(_rms(qlat, gq_ref[...]).astype(BF16), wuq_ref[...])
    kv = _dot(_rms(ckv, gkv_ref[...]).astype(BF16), wukv_ref[...])

    for part, scale in ((0, sb_scale * LOG2E), (1, 1.0)):
        c0 = R_MB + part * W_BRANCH
        acc = _dot(h, w_ref[:, c0:c0 + W_BRANCH])
        for cb in range(W_BRANCH // LANE):
            blk = _rope_lanes(acc[:, cb * LANE:(cb + 1) * LANE], cosm, sinm, MOBA_ROT // 2, mb_first)
            mb_ref[0, :, part * W_BRANCH + cb * LANE:part * W_BRANCH + (cb + 1) * LANE] = (
                blk * scale).astype(BF16)
    mb_ref[0, :, 2 * W_BRANCH:] = _dot(h, w_ref[:, R_MB + 2 * W_BRANCH:R_GATE]).astype(BF16)

    kpe = _rope_lanes(kpe, cosa, sina, MLA_ROPE // 2, mla_first)
    k_off = N_HEADS * LANE
    for hd in range(N_HEADS):
        qb = _rope_lanes(q[:, hd * LANE:(hd + 1) * LANE], cosa, sina, MLA_ROPE // 2, mla_first)
        mla_ref[0, :, hd * LANE:(hd + 1) * LANE] = (qb * mla_scale).astype(BF16)
        kb = kv[:, hd * LANE:(hd + 1) * LANE] + kpe
        mla_ref[0, :, k_off + hd * LANE:k_off + (hd + 1) * LANE] = kb.astype(BF16)
    mla_ref[0, :, 2 * k_off:] = kv[:, k_off:].astype(BF16)

    for br in range(3):
        c0 = R_GATE + br * D_MODEL
        gl = _dot(h, w_ref[:, c0:c0 + D_MODEL])
        gate_ref[0, :, br * D_MODEL:(br + 1) * D_MODEL] = (1.0 / (1.0 + jnp.exp(-gl))).astype(BF16)


def _const_spec(shape):
    nd = len(shape)
    return pl.BlockSpec(shape, lambda *_: (0,) * nd, pipeline_mode=pl.Buffered(1))


def _inproj(x, shift, scale, g1, wlat, wkpe, wrest, gq, wuq, gkv, wukv, rope):
    b, s, d = x.shape
    tm = TQ
    row = lambda w: pl.BlockSpec((1, tm, w), lambda i, j: (i, j, 0))
    vec = pl.BlockSpec((1, 1, d), lambda i, j: (i, 0, 0))
    return pl.pallas_call(
        _inproj_kernel,
        grid=(b, s // tm),
        in_specs=[row(d), vec, vec, _const_spec((1, d)), _const_spec(wlat.shape),
                  _const_spec(wkpe.shape), _const_spec(wrest.shape), _const_spec(gq.shape),
                  _const_spec(wuq.shape), _const_spec(gkv.shape), _const_spec(wukv.shape),
                  row(LANE)],
        out_specs=[row(W_MLA_OUT), row(3 * W_BRANCH), row(3 * W_BRANCH), row(3 * D_MODEL)],
        out_shape=[jax.ShapeDtypeStruct((b, s, W_MLA_OUT), BF16),
                   jax.ShapeDtypeStruct((b, s, 3 * W_BRANCH), BF16),
                   jax.ShapeDtypeStruct((b, s, 3 * W_BRANCH), BF16),
                   jax.ShapeDtypeStruct((b, s, 3 * D_MODEL), BF16)],
        compiler_params=pltpu.CompilerParams(
            dimension_semantics=("arbitrary", "arbitrary"), vmem_limit_bytes=VMEM_LIMIT),
        name="inproj",
    )(x, shift, scale, g1, wlat, wkpe, wrest, gq, wuq, gkv, wukv, rope)


def _transpose_values(v_ref, vt_ref, nkv, stride):
    for j in range(nkv):
        for c in range(W_BRANCH // LANE):
            blk = v_ref[0, j * TQ:(j + 1) * TQ, c * LANE:(c + 1) * LANE].astype(F32).T.astype(BF16)
            for a in range(2):
                h = 2 * c + a
                vt_ref[j, h * stride:h * stride + HEAD, :] = blk[a * HEAD:(a + 1) * HEAD, :]
                if stride > HEAD:
                    vt_ref[j, h * stride + HEAD:(h + 1) * stride, :] = jnp.ones((stride - HEAD, TQ), BF16)


def _store_heads(o_ref, acc_ref, stride, row0=0):
    for c in range(W_BRANCH // LANE):
        parts = []
        for h in (2 * c, 2 * c + 1):
            part = acc_ref[h * stride:h * stride + HEAD, :]
            if stride > HEAD:
                part = part / acc_ref[h * stride + HEAD:h * stride + HEAD + 1, :]
            parts.append(part)
        o_ref[0, row0:row0 + TQ, c * LANE:(c + 1) * LANE] = (
            jnp.concatenate(parts, axis=0).T.astype(BF16))


def _stack_pair_queries(q_ref, qbd_ref, row0=0):
    pairs, w, _ = qbd_ref.shape
    zero = jnp.zeros((w // 2, TQ), BF16)
    for p in range(pairs):
        qt = q_ref[0, row0:row0 + TQ, p * w:(p + 1) * w].astype(F32).T.astype(BF16)
        qbd_ref[p, :w // 2, :TQ] = qt[:w // 2, :]
        qbd_ref[p, :w // 2, TQ:] = zero
        qbd_ref[p, w // 2:, :TQ] = zero
        qbd_ref[p, w // 2:, TQ:] = qt[w // 2:, :]


def _pair_scores(k_ref, qbd_ref, row0, p):
    w = qbd_ref.shape[1]
    return _dot(k_ref[0, pl.ds(row0, TQ), p * w:(p + 1) * w], qbd_ref[p])


def _causal(strict):
    key = lax.broadcasted_iota(jnp.int32, (TQ, TQ), 0)
    qry = lax.broadcasted_iota(jnp.int32, (TQ, TQ), 1)
    return key < qry if strict else key <= qry


def _stage_pair(k_ref, qbd_ref, s_ref, mx_ref, j, p, mask):
    pair = _pair_scores(k_ref, qbd_ref, pl.multiple_of(j * TQ, TQ), p)
    for a in range(2):
        h = 2 * p + a
        s = mask(h, pair[:, a * TQ:(a + 1) * TQ])
        s_ref[p, :, a * TQ:(a + 1) * TQ] = s
        mx_ref[h:h + 1, :] = jnp.max(s, axis=0, keepdims=True)


def _update_pair(s_ref, mx_ref, vt_blk, m_ref, acc_ref, p, first):
    for a in range(2):
        h = 2 * p + a
        rows = slice(h * HEAD_L, (h + 1) * HEAD_L)
        m_new = mx_ref[h:h + 1, :]
        if not first:
            m_old = m_ref[h:h + 1, :]
            m_new = jnp.maximum(m_old, m_new)
            alpha = jnp.exp2(m_old - m_new)
        pr = jnp.exp2(s_ref[p, :, a * TQ:(a + 1) * TQ] - m_new)
        m_ref[h:h + 1, :] = m_new
        o = _dot(vt_blk[rows, :], pr.astype(BF16))
        acc_ref[rows, :] = o if first else alpha * acc_ref[rows, :] + o


def _paired_sweep(t, stage, update, tile_by_tile):
    pairs = range(N_HEADS // 2)
    tiles = (0, 1)

    def step(nxt_slot, nxt_j, cur_slot, cur_j, first):
        for p in pairs:
            if tile_by_tile:
                for u in tiles:
                    stage(u, nxt_slot, nxt_j, False, p)
                    update(u, cur_slot, cur_j[u], first, p)
            else:
                for u in tiles:
                    stage(u, nxt_slot, nxt_j, False, p)
                for u in tiles:
                    update(u, cur_slot, cur_j[u], first, p)

    for p in pairs:
        for u in tiles:
            stage(u, 0, 2 * t + u, True, p)
    step(1, 0, 0, (2 * t, 2 * t + 1), True)

    def two_blocks(i, carry):
        j = 2 * i
        step(0, j + 1, 1, (j, j), False)
        step(1, j + 2, 0, (j + 1, j + 1), False)
        return carry

    lax.fori_loop(0, t, two_blocks, 0)
    for p in pairs:
        update(1, 1, 2 * t, False, p)


def _mla_kernel(q_ref, k_ref, v_ref, o_ref, vt_ref, qbd_ref, s_ref, mx_ref, m_ref, acc_ref, *, nkv):
    t = pl.program_id(1)

    @pl.when(t == 0)
    def _():
        _transpose_values(v_ref, vt_ref, nkv, HEAD_L)

    for u in range(2):
        _stack_pair_queries(q_ref, qbd_ref.at[u], u * TQ)

    def stage(u, slot, j, diag, p):
        mask = (lambda h, s: jnp.where(_causal(False), s, NEG_INF)) if diag else (lambda h, s: s)
        _stage_pair(k_ref, qbd_ref.at[u], s_ref.at[slot, u], mx_ref.at[slot, u], j, p, mask)

    def update(u, slot, j, first, p):
        _update_pair(s_ref.at[slot, u], mx_ref.at[slot, u], vt_ref.at[j], m_ref.at[u], acc_ref.at[u],
                     p, first)

    _paired_sweep(t, stage, update, tile_by_tile=False)
    for u in range(2):
        _store_heads(o_ref, acc_ref.at[u], HEAD_L, u * TQ)


def _sb_kernel(q_ref, k_ref, v_ref, o_ref, vt_ref, qbd_ref, run_ref, acc_ref, *, nkv):
    qi = pl.program_id(1)

    @pl.when(qi == 0)
    def _():
        _transpose_values(v_ref, vt_ref, nkv, HEAD)

    _stack_pair_queries(q_ref, qbd_ref)

    def block(j, first):
        row0 = pl.multiple_of(j * TQ, TQ)
        suffix = jnp.where(_causal(False), 1.0, 0.0).astype(BF16)
        zs = [_pair_scores(k_ref, qbd_ref, row0, p) for p in range(N_HEADS // 2)]
        zs = [zs[h // 2][:, (h % 2) * TQ:(h % 2 + 1) * TQ] for h in range(N_HEADS)]
        incls = []
        for h in range(N_HEADS):
            z = zs[h]
            neg_abs = lax.bitcast_convert_type(
                lax.bitcast_convert_type(z, jnp.uint32) | jnp.uint32(0x80000000), F32)
            drop = jnp.maximum(z, 0.0) + jnp.log(1.0 + jnp.exp(neg_abs))
            if first:
                drop = jnp.where(_causal(True), drop, 0.0)
            hi = lax.bitcast_convert_type(
                lax.bitcast_convert_type(drop, jnp.uint32) & jnp.uint32(0xFFFF0000), F32)
            both = _dot(suffix, jnp.concatenate([hi.astype(BF16), (drop - hi).astype(BF16)], axis=1))
            incls.append(both[:, :TQ] + both[:, TQ:])
        for h in range(N_HEADS):
            rows = slice(h * HEAD, (h + 1) * HEAD)
            if first:
                w = jnp.where(_causal(True), jnp.exp(zs[h] - incls[h]), 0.0)
                run_ref[h:h + 1, :] = -incls[h][0:1, :]
                acc_ref[rows, :] = _dot(vt_ref[j, rows, :], w.astype(BF16))
            else:
                run = run_ref[h:h + 1, :]
                w = jnp.exp((zs[h] + run) - incls[h])
                run_ref[h:h + 1, :] = run - incls[h][0:1, :]
                acc_ref[rows, :] = acc_ref[rows, :] + _dot(vt_ref[j, rows, :], w.astype(BF16))

    def live(carry):
        i, top = carry
        return (i < qi) & (top > SB_DEAD)

    def body(carry):
        i, _ = carry
        block(qi - 1 - i, False)
        return i + 1, jnp.max(run_ref[...])

    block(qi, True)
    lax.while_loop(live, body, (jnp.int32(0), jnp.max(run_ref[...])))
    _store_heads(o_ref, acc_ref, HEAD)


def _top_blocks(gate, blk, own):
    taken = -3.0e38
    sel = jnp.zeros(gate.shape, F32)
    for r in range(MOBA_TOPK):
        best = jnp.max(gate, axis=0, keepdims=True)
        first = jnp.min(jnp.where(gate == best, blk, gate.shape[0]), axis=0, keepdims=True)
        chosen = blk == first
        sel = jnp.where(chosen, jnp.where(r < own, 1.0, 0.0), sel)
        gate = jnp.where(chosen, taken, gate)
    return sel


def _moba_kernel(q_ref, k_ref, v_ref, o_ref, vt_ref, qbd_ref, s_ref, mx_ref, m_ref, acc_ref, km_ref,
                 sel_ref, *, nkv, nbp):
    t = pl.program_id(1)
    lane = lax.broadcasted_iota(jnp.int32, (1, W_BRANCH), 1)

    @pl.when(t == 0)
    def _():
        _transpose_values(v_ref, vt_ref, nkv, HEAD_L)
        km_ref[...] = jnp.zeros_like(km_ref)
        for j in range(nkv):
            mean = jnp.mean(k_ref[0, j * TQ:(j + 1) * TQ, :].astype(F32), axis=0, keepdims=True)
            for h in range(N_HEADS):
                in_head = (lane >= h * HEAD) & (lane < (h + 1) * HEAD)
                km_ref[h * nbp + j:h * nbp + j + 1, :] = jnp.where(in_head, mean, 0.0)

    blk = lax.broadcasted_iota(jnp.int32, (nbp, TQ), 0)
    all_gates = lax.dot_general(km_ref[...], q_ref[0].astype(F32), (((1,), (1,)), ((), ())),
                                precision=lax.Precision.HIGHEST, preferred_element_type=F32)
    for u in range(2):
        _stack_pair_queries(q_ref, qbd_ref.at[u], u * TQ)
        own = 2 * t + u
        gates = all_gates[:, u * TQ:(u + 1) * TQ]
        for h in range(N_HEADS):
            gate = jnp.where(blk < own, gates[h * nbp:(h + 1) * nbp, :], NEG_INF)
            sel_ref[u, h * nbp:(h + 1) * nbp, :] = _top_blocks(gate, blk, own)

    def stage(u, slot, j, diag, p):
        if diag:
            mask = lambda h, s: jnp.where(_causal(False), s, NEG_INF)
        else:
            mask = lambda h, s: jnp.where(sel_ref[u, pl.ds(h * nbp + j, 1), :] > 0.5, s, NEG_INF)
        _stage_pair(k_ref, qbd_ref.at[u], s_ref.at[slot, u], mx_ref.at[slot, u], j, p, mask)

    def update(u, slot, j, first, p):
        _update_pair(s_ref.at[slot, u], mx_ref.at[slot, u], vt_ref.at[j], m_ref.at[u], acc_ref.at[u],
                     p, first)

    _paired_sweep(t, stage, update, tile_by_tile=True)
    for u in range(2):
        _store_heads(o_ref, acc_ref.at[u], HEAD_L, u * TQ)


def _attention(kind, src, qk_width, v_col):
    b, s, _ = src.shape
    nkv = s // TQ
    pair_w = 2 * qk_width // N_HEADS
    if kind == "sb":
        tiles = 1
        rows = N_HEADS * HEAD
        body = functools.partial(_sb_kernel, nkv=nkv)
        scratch = [pltpu.VMEM((nkv, rows, TQ), BF16), pltpu.VMEM((N_HEADS // 2, pair_w, 2 * TQ), BF16),
                   pltpu.VMEM((N_HEADS, TQ), F32), pltpu.VMEM((rows, TQ), F32)]
    else:
        tiles = 2
        rows = N_HEADS * HEAD_L
        scratch = [pltpu.VMEM((nkv, rows, TQ), BF16),
                   pltpu.VMEM((tiles, N_HEADS // 2, pair_w, 2 * TQ), BF16),
                   pltpu.VMEM((2, tiles, N_HEADS // 2, TQ, 2 * TQ), F32),
                   pltpu.VMEM((2, tiles, N_HEADS, TQ), F32),
                   pltpu.VMEM((tiles, N_HEADS, TQ), F32),
                   pltpu.VMEM((tiles, rows, TQ), F32)]
        if kind == "mla":
            body = functools.partial(_mla_kernel, nkv=nkv)
        else:
            nbp = -(-nkv // 8) * 8
            body = functools.partial(_moba_kernel, nkv=nkv, nbp=nbp)
            scratch += [pltpu.VMEM((N_HEADS * nbp, W_BRANCH), F32),
                        pltpu.VMEM((tiles, N_HEADS * nbp, TQ), F32)]
    tq = tiles * TQ
    return pl.pallas_call(
        body,
        grid=(b, s // tq),
        in_specs=[
            pl.BlockSpec((1, tq, qk_width), lambda i, t: (i, t, 0)),
            pl.BlockSpec((1, s, qk_width), lambda i, t: (i, 0, 1)),
            pl.BlockSpec((1, s, W_BRANCH), lambda i, t: (i, 0, v_col)),
        ],
        out_specs=pl.BlockSpec((1, tq, W_BRANCH), lambda i, t: (i, t, 0)),
        out_shape=jax.ShapeDtypeStruct((b, s, W_BRANCH), BF16),
        scratch_shapes=scratch,
        compiler_params=pltpu.CompilerParams(
            dimension_semantics=("arbitrary", "arbitrary"), vmem_limit_bytes=VMEM_LIMIT),
        name="attn_" + kind,
    )(src, src, src)


def _tail_kernel(x_ref, gate1_ref, oa_ref, ob_ref, oc_ref, g_ref, wa_ref, wb_ref, wc_ref, wo_ref,
                 shift_ref, scale_ref, gate2_ref, g2_ref, w1_ref, w2_ref, gf_ref, o_ref, *, final_norm):
    merged = g_ref[0, :, :D_MODEL].astype(F32) * _dot(oa_ref[0], wa_ref[...])
    merged = merged + g_ref[0, :, D_MODEL:2 * D_MODEL].astype(F32) * _dot(ob_ref[0], wb_ref[...])
    merged = merged + g_ref[0, :, 2 * D_MODEL:].astype(F32) * _dot(oc_ref[0], wc_ref[...])
    x = x_ref[0] + gate1_ref[0] * _dot(merged.astype(BF16), wo_ref[...])

    h = (_rms(x, g2_ref[...]) * (1.0 + scale_ref[0]) + shift_ref[0]).astype(BF16)
    acc = jnp.zeros_like(x)
    chunk = D_MODEL
    for c in range(D_FF // chunk):
        u = jnp.maximum(_dot(h, w1_ref[:, c * chunk:(c + 1) * chunk]), 0.0)
        acc = acc + _dot((u * u).astype(BF16), w2_ref[c * chunk:(c + 1) * chunk, :])
    y = x + gate2_ref[0] * acc
    if final_norm:
        y = _rms(y, gf_ref[...])
    o_ref[0] = y


def _tail(x, gate1, oa, ob, oc, gates, wa, wb, wc, wo, shift, scale, gate2, g2, w1, w2, gf, final_norm):
    b, s, d = x.shape
    tm = TM_DENSE
    row = lambda w: pl.BlockSpec((1, tm, w), lambda i, j: (i, j, 0))
    vec = pl.BlockSpec((1, 1, d), lambda i, j: (i, 0, 0))
    return pl.pallas_call(
        functools.partial(_tail_kernel, final_norm=final_norm),
        grid=(b, s // tm),
        in_specs=[row(d), vec, row(W_BRANCH), row(W_BRANCH), row(W_BRANCH), row(3 * d),
                  _const_spec(wa.shape), _const_spec(wb.shape), _const_spec(wc.shape),
                  _const_spec(wo.shape), vec, vec, vec, _const_spec((1, d)), _const_spec(w1.shape),
                  _const_spec(w2.shape), _const_spec((1, d))],
        out_specs=row(d),
        out_shape=jax.ShapeDtypeStruct((b, s, d), F32),
        compiler_params=pltpu.CompilerParams(
            dimension_semantics=("arbitrary", "arbitrary"), vmem_limit_bytes=VMEM_LIMIT),
        name="tail",
    )(x, gate1, oa, ob, oc, gates, wa, wb, wc, wo, shift, scale, gate2, g2, w1, w2, gf)


def _rope_table(positions):
    pos = positions.astype(F32)[..., None]
    parts = []
    for rot_dim in (MLA_ROPE, MOBA_ROT):
        inv = ROPE_THETA ** (-jnp.arange(0, rot_dim, 2, dtype=F32) / rot_dim)
        parts += [jnp.cos(pos * inv), jnp.sin(pos * inv)]
    used = MLA_ROPE + MOBA_ROT
    return jnp.pad(jnp.concatenate(parts, axis=-1), ((0, 0), (0, 0), (0, LANE - used)))


def _layer_weights(w_in, w_uq, w_ukv):
    wlat = w_in[:, :C_KPE].astype(BF16)
    wkpe = jnp.pad(w_in[:, C_KPE:C_REST], ((0, 0), (MLA_NOPE, LANE - MLA_QK))).astype(BF16)
    wrest = w_in[:, C_REST:].astype(BF16)
    wuq = jnp.pad(w_uq.reshape(MLA_Q_RANK, N_HEADS, MLA_QK),
                  ((0, 0), (0, 0), (0, LANE - MLA_QK))).reshape(MLA_Q_RANK, N_HEADS * LANE)
    ukv = w_ukv.reshape(MLA_KV_RANK, N_HEADS, MLA_NOPE + MLA_V)
    wk = jnp.pad(ukv[:, :, :MLA_NOPE], ((0, 0), (0, 0), (0, LANE - MLA_NOPE)))
    wukv = jnp.concatenate([wk.reshape(MLA_KV_RANK, N_HEADS * LANE),
                            ukv[:, :, MLA_NOPE:].reshape(MLA_KV_RANK, N_HEADS * MLA_V)], axis=1)
    return wlat, wkpe, wrest, wuq.astype(BF16), wukv.astype(BF16)


def kernel(x, c, positions, w_ada, b_ada, norm1_g, norm2_g, w_in, q_norm_g, w_uq, kv_norm_g, w_ukv,
           w_o_mla, w_o_sb, w_o_moba, w_out, w_ff1, w_ff2, final_norm_g):
    b, s, d = x.shape
    depth = w_ada.shape[0]
    assert d == D_MODEL and s % TM_DENSE == 0 and w_in.shape[-1] == D_IN
    rope = _rope_table(positions)
    mod = _adaln_mod(c, w_ada, b_ada)
    gf = final_norm_g.reshape(1, d)
    for l in range(depth):
        shift1, scale1, gate1, shift2, scale2, gate2 = [
            m.reshape(b, 1, d) for m in jnp.split(mod[l], 6, axis=-1)]
        wlat, wkpe, wrest, wuq, wukv = _layer_weights(w_in[l], w_uq[l], w_ukv[l])
        mla, sb, mb, gates = _inproj(
            x, shift1, scale1, norm1_g[l].reshape(1, d), wlat, wkpe, wrest,
            q_norm_g[l].reshape(1, -1), wuq, kv_norm_g[l].reshape(1, -1), wukv, rope)
        o_mla = _attention("mla", mla, N_HEADS * LANE, 4)
        o_sb = _attention("sb", sb, W_BRANCH, 2)
        o_mb = _attention("moba", mb, W_BRANCH, 2)
        x = _tail(x, gate1, o_mla, o_sb, o_mb, gates, w_o_mla[l].astype(BF16), w_o_sb[l].astype(BF16),
                  w_o_moba[l].astype(BF16), w_out[l].astype(BF16), shift2, scale2, gate2,
                  norm2_g[l].reshape(1, d), w_ff1[l].astype(BF16), w_ff2[l].astype(BF16), gf,
                  l == depth - 1)
    return x
```
